```python
import jax, jax.numpy as jnp
from jax import lax
import numpy as np

D_MODEL = 2048
BATCH = 4
SEQ = 2048
DEPTH = 4
DEC_BATCH = 32
DEC_SEQ = 32
PAST_LEN = 2048

CHUNK = 64
N_REC_LAYERS = (DEPTH + 1) // 2
N_ATT_LAYERS = DEPTH // 2

D_RNN = D_MODEL
RNN_BLOCKS = 16
RNN_BLOCK_W = D_RNN // RNN_BLOCKS
CONV_W = 4
RG_C = 8.0

D_POOL = D_MODEL // 2
POOL_WINDOWS = (2, 4, 8, 16)
POOL_GROUPS = len(POOL_WINDOWS)
POOL_GW = D_POOL // POOL_GROUPS
POOL_BUF = max(POOL_WINDOWS) - 1

N_HEADS = 16
HEAD_DIM = 128
D_ATT = N_HEADS * HEAD_DIM
Q_BLOCK = 128
EPS = 1e-6

kernel_name = "hybrid_rglru_pool_stickbreaking_stream_step"


def rms_norm(x, g):
    x32 = x.astype(jnp.float32)
    y = x32 * lax.rsqrt(jnp.mean(x32 * x32, axis=-1, keepdims=True) + EPS) * g.astype(jnp.float32)
    return y.astype(x.dtype)


def causal_conv(x, buf, w, b):
    T = x.shape[1]
    xp = jnp.concatenate([buf.astype(x.dtype), x], axis=1)
    y = b.astype(x.dtype) + xp[:, 0:T] * w[0]
    for k in range(1, CONV_W):
        y = y + xp[:, k:k + T] * w[k]
    return y, xp[:, -(CONV_W - 1):]


def rg_lru(x, h0, w_r, b_r, w_i, b_i, lam):
    B, T, _ = x.shape
    f32 = jnp.float32
    x32 = x.astype(f32)
    xb = x32.reshape(B, T, RNN_BLOCKS, RNN_BLOCK_W)
    r = jax.nn.sigmoid(jnp.einsum('btnc,ncd->btnd', xb, w_r.astype(f32)).reshape(B, T, D_RNN) + b_r.astype(f32))
    i = jax.nn.sigmoid(jnp.einsum('btnc,ncd->btnd', xb, w_i.astype(f32)).reshape(B, T, D_RNN) + b_i.astype(f32))
    log_a = -RG_C * r * jax.nn.softplus(-lam.astype(f32))
    a = jnp.exp(log_a)
    u = jnp.sqrt(-jnp.expm1(2.0 * log_a)) * (i * x32)

    def step(h, au):
        a_t, u_t = au
        h = a_t * h + u_t
        return h, h

    h_last, hs = lax.scan(step, h0.astype(f32), (jnp.swapaxes(a, 0, 1), jnp.swapaxes(u, 0, 1)))
    return jnp.swapaxes(hs, 0, 1), h_last


def pool_mix(xb, buf, pos0, w_pool, scale):
    B, T, _ = xb.shape
    f32 = jnp.float32
    xp = jnp.concatenate([buf.astype(f32), xb.astype(f32)], axis=1)
    cs = jnp.concatenate([jnp.zeros((B, 1, D_POOL), f32), jnp.cumsum(xp, axis=1)], axis=1)
    pos = pos0 + jnp.arange(T)
    P = POOL_BUF
    outs = []
    for g, w in enumerate(POOL_WINDOWS):
        sl = slice(g * POOL_GW, (g + 1) * POOL_GW)
        win = cs[:, P + 1:P + 1 + T, sl] - cs[:, P + 1 - w:P + 1 - w + T, sl]
        cnt = jnp.minimum(w, pos + 1).astype(f32)[None, :, None]
        outs.append(win / cnt - xp[:, P:, sl])
    m = jnp.stack(outs, axis=2)
    y = jnp.einsum('btgc,gcd->btgd', m, w_pool.astype(f32)).reshape(B, T, D_POOL) * scale.astype(f32)
    return y, xp[:, -P:]


def rec_layer(x, h0, conv_buf, pool_buf, pos0, g, w_in, conv_w, conv_b, w_r, b_r, w_i, b_i, lam,
              pool_w, pool_scale, w_out):
    h = rms_norm(x, g)
    proj = h @ w_in
    xa, ga, xb, gb = jnp.split(proj, [D_RNN, 2 * D_RNN, 2 * D_RNN + D_POOL], axis=-1)
    xa_c, conv_new = causal_conv(xa, conv_buf, conv_w, conv_b)
    ya, h_new = rg_lru(xa_c, h0, w_r, b_r, w_i, b_i, lam)
    yb, pool_new = pool_mix(xb, pool_buf, pos0, pool_w, pool_scale)
    mixed = jnp.concatenate([ya.astype(x.dtype) * jax.nn.silu(ga),
                             yb.astype(x.dtype) * jax.nn.silu(gb)], axis=-1)
    return x + mixed @ w_out, h_new.astype(x.dtype), conv_new.astype(x.dtype), pool_new.astype(x.dtype)


def sb_attend(q, k, v, q_pos, k_pos):
    f32 = jnp.float32
    z = jnp.einsum('bqhd,bkhd->bhqk', q.astype(f32), k.astype(f32)) * (HEAD_DIM ** -0.5)
    mask = (k_pos[None, :] < q_pos[:, None])[None, None]
    log_1m = jnp.where(mask, -jax.nn.softplus(z), 0.0)
    suffix = lax.cumsum(log_1m, axis=3, reverse=True) - log_1m
    w = jnp.where(mask, jnp.exp(jax.nn.log_sigmoid(z) + suffix), 0.0)
    return jnp.einsum('bhqk,bkhd->bqhd', w, v.astype(f32))


def att_inputs(x, g, w_in):
    B, T, _ = x.shape
    proj = rms_norm(x, g) @ w_in
    q, k, v, gt = jnp.split(proj, 4, axis=-1)
    shp = (B, T, N_HEADS, HEAD_DIM)
    return q.reshape(shp), k.reshape(shp), v.reshape(shp), gt


def att_layer_prompt(x, g, w_in, w_out):
    B, T, _ = x.shape
    q, k, v, gt = att_inputs(x, g, w_in)
    nb = T // Q_BLOCK
    qb = jnp.swapaxes(q.reshape(B, nb, Q_BLOCK, N_HEADS, HEAD_DIM), 0, 1)
    qpos = jnp.arange(T).reshape(nb, Q_BLOCK)
    kpos = jnp.arange(T)
    o = lax.map(lambda a: sb_attend(a[0], k, v, a[1], kpos), (qb, qpos))
    o = jnp.swapaxes(o, 0, 1).reshape(B, T, D_ATT)
    return x + (o.astype(x.dtype) * jax.nn.silu(gt)) @ w_out, k, v


def att_layer_sample(x, k_cache, v_cache, g, w_in, w_out):
    B, T, _ = x.shape
    q, k, v, gt = att_inputs(x, g, w_in)
    P = k_cache.shape[1]
    k_all = jnp.concatenate([k_cache.astype(k.dtype), k], axis=1)
    v_all = jnp.concatenate([v_cache.astype(v.dtype), v], axis=1)
    o = sb_attend(q, k_all, v_all, P + jnp.arange(T), jnp.arange(P + T)).reshape(B, T, D_ATT)
    return x + (o.astype(x.dtype) * jax.nn.silu(gt)) @ w_out, k, v


def setup_inputs(seed: int = 0) -> dict:
    key = jax.random.key(seed)
    ks = jax.random.split(key, 24)
    f32 = jnp.float32

    def nrm(k, shape, scale):
        return jax.random.normal(k, shape, f32) * scale

    d_in_rec = 2 * D_RNN + 2 * D_POOL
    u = jax.random.uniform(ks[12], (N_REC_LAYERS, D_RNN), f32, 0.9, 0.999)
    a0 = u ** (1.0 / RG_C)
    rg_lambda = jnp.log(a0) - jnp.log1p(-a0)
    return {
        'x_prompt': nrm(ks[0], (BATCH, SEQ, D_MODEL), 1.0),
        'x_sample': nrm(ks[1], (DEC_BATCH, DEC_SEQ, D_MODEL), 1.0),
        'cache_k': nrm(ks[2], (N_ATT_LAYERS, DEC_BATCH, PAST_LEN, N_HEADS, HEAD_DIM), 1.0),
        'cache_v': nrm(ks[3], (N_ATT_LAYERS, DEC_BATCH, PAST_LEN, N_HEADS, HEAD_DIM), 1.0),
        'state_h': nrm(ks[4], (N_REC_LAYERS, DEC_BATCH, D_RNN), 0.5),
        'state_conv': nrm(ks[5], (N_REC_LAYERS, DEC_BATCH, CONV_W - 1, D_RNN), 1.0),
        'state_pool': nrm(ks[6], (N_REC_LAYERS, DEC_BATCH, POOL_BUF, D_POOL), 1.0),
        'norm_rec': 1.0 + nrm(ks[7], (N_REC_LAYERS, D_MODEL), 0.05),
        'w_in_rec': nrm(ks[8], (N_REC_LAYERS, D_MODEL, d_in_rec), D_MODEL ** -0.5),
        'conv_w': nrm(ks[9], (N_REC_LAYERS, CONV_W, D_RNN), CONV_W ** -0.5),
        'conv_b': nrm(ks[10], (N_REC_LAYERS, D_RNN), 0.02),
        'gate_r_w': nrm(ks[11], (N_REC_LAYERS, RNN_BLOCKS, RNN_BLOCK_W, RNN_BLOCK_W), RNN_BLOCK_W ** -0.5),
        'gate_r_b': nrm(ks[13], (N_REC_LAYERS, D_RNN), 0.1),
        'gate_i_w': nrm(ks[14], (N_REC_LAYERS, RNN_BLOCKS, RNN_BLOCK_W, RNN_BLOCK_W), RNN_BLOCK_W ** -0.5),
        'gate_i_b': nrm(ks[15], (N_REC_LAYERS, D_RNN), 0.1),
        'rg_lambda': rg_lambda,
        'pool_w': nrm(ks[16], (N_REC_LAYERS, POOL_GROUPS, POOL_GW, POOL_GW), POOL_GW ** -0.5),
        'pool_scale': 1.0 + nrm(ks[17], (N_REC_LAYERS, D_POOL), 0.1),
        'w_out_rec': nrm(ks[18], (N_REC_LAYERS, D_RNN + D_POOL, D_MODEL), (D_RNN + D_POOL) ** -0.5),
        'norm_att': 1.0 + nrm(ks[19], (N_ATT_LAYERS, D_MODEL), 0.05),
        'w_in_att': nrm(ks[20], (N_ATT_LAYERS, D_MODEL, 4 * D_ATT), D_MODEL ** -0.5),
        'w_out_att': nrm(ks[21], (N_ATT_LAYERS, D_ATT, D_MODEL), D_ATT ** -0.5),
        'norm_final': 1.0 + nrm(ks[22], (D_MODEL,), 0.05),
    }


def reference(x_prompt, x_sample, cache_k, cache_v, state_h, state_conv, state_pool,
              norm_rec, w_in_rec, conv_w, conv_b, gate_r_w, gate_r_b, gate_i_w, gate_i_b, rg_lambda,
              pool_w, pool_scale, w_out_rec, norm_att, w_in_att, w_out_att, norm_final):
    xp, xs = x_prompt, x_sample
    Bp = xp.shape[0]
    kp_l, vp_l, ks_l, vs_l = [], [], [], []
    hp_l, cp_l, pp_l, hs_l, cs_l, ps_l = [], [], [], [], [], []
    for l in range(DEPTH):
        j = l // 2
        if l % 2 == 0:
            prm = (norm_rec[j], w_in_rec[j], conv_w[j], conv_b[j], gate_r_w[j], gate_r_b[j],
                   gate_i_w[j], gate_i_b[j], rg_lambda[j], pool_w[j], pool_scale[j], w_out_rec[j])
            xp, hp, cp, pp = rec_layer(
                xp, jnp.zeros((Bp, D_RNN), xp.dtype), jnp.zeros((Bp, CONV_W - 1, D_RNN), xp.dtype),
                jnp.zeros((Bp, POOL_BUF, D_POOL), xp.dtype), 0, *prm)
            xs, hs, cs, ps = rec_layer(xs, state_h[j], state_conv[j], state_pool[j], PAST_LEN, *prm)
            hp_l.append(hp); cp_l.append(cp); pp_l.append(pp)
            hs_l.append(hs); cs_l.append(cs); ps_l.append(ps)
        else:
            xp, kp, vp = att_layer_prompt(xp, norm_att[j], w_in_att[j], w_out_att[j])
            xs, kn, vn = att_layer_sample(xs, cache_k[j], cache_v[j], norm_att[j], w_in_att[j], w_out_att[j])
            kp_l.append(kp); vp_l.append(vp); ks_l.append(kn); vs_l.append(vn)
    y_prompt = rms_norm(xp, norm_final)
    y_sample = rms_norm(xs, norm_final)
    return (y_prompt, y_sample,
            jnp.stack(kp_l), jnp.stack(vp_l), jnp.stack(hp_l), jnp.stack(cp_l), jnp.stack(pp_l),
            jnp.stack(ks_l), jnp.stack(vs_l), jnp.stack(hs_l), jnp.stack(cs_l), jnp.stack(ps_l))
```

```python
import functools

import jax
import jax.numpy as jnp
from jax import lax
from jax.experimental import pallas as pl
from jax.experimental.pallas import tpu as pltpu

F32 = jnp.float32
BF16 = jnp.bfloat16

EPS = 1e-6
RG_C = 8.0
CONV_W = 4
POOL_WINDOWS = (2, 4, 8, 16)
POOL_BUF = max(POOL_WINDOWS) - 1
HEAD_DIM = 128
ATT_SCALE = HEAD_DIM ** -0.5

LANES = 128
SUBLANES = 8
VMEM_LIMIT_CAP = 60000 * 1024
VMEM_SLACK = 8 * 1024 * 1024


def _params(semantics, buffer_bytes):
    limit = min(VMEM_LIMIT_CAP, buffer_bytes + VMEM_SLACK)
    return pltpu.CompilerParams(dimension_semantics=semantics, vmem_limit_bytes=limit)


def _tile(n, pref):
    t = min(n, pref)
    while n % t:
        t //= 2
    return t


def _sigmoid(x):
    return 1.0 / (1.0 + jnp.exp(-x))


def _softplus(x):
    return jnp.maximum(x, 0.0) + jnp.log(1.0 + jnp.exp(-jnp.abs(x)))


NORM_ROWS = 128


def _norm_mm_kernel(x_ref, g_ref, w_ref, *refs, n_out):
    out_refs, xn_ref = refs[:n_out], refs[n_out]
    tm = x_ref.shape[0]

    @pl.when(pl.program_id(1) == 0)
    def _():
        g = g_ref[...]

        def body(c, _):
            rows = pl.ds(pl.multiple_of(c * NORM_ROWS, NORM_ROWS), NORM_ROWS)
            x = x_ref[rows, :]
            ms = jnp.mean(x * x, axis=-1, keepdims=True)
            xn_ref[rows, :] = (x * lax.rsqrt(ms + EPS) * g).astype(BF16)
            return 0

        lax.fori_loop(0, tm // NORM_ROWS, body, 0)

    acc = jnp.dot(xn_ref[...], w_ref[...], preferred_element_type=F32)
    for o in out_refs:
        o[...] = acc.astype(o.dtype)


def _norm_matmul(x, g, w, out_dtypes, *, tm_pref=1024, tn_pref=512):
    m, k = x.shape
    n = w.shape[1]
    tm, tn = _tile(m, tm_pref), _tile(n, tn_pref)
    out_bytes = sum(jnp.dtype(d).itemsize for d in out_dtypes)
    vmem = 2 * tm * k * 4 + tm * k * 2 + 2 * k * tn * 2 + 2 * tm * tn * out_bytes
    outs = pl.pallas_call(
        functools.partial(_norm_mm_kernel, n_out=len(out_dtypes)),
        grid=(m // tm, n // tn),
        in_specs=[pl.BlockSpec((tm, k), lambda i, j: (i, 0)),
                  pl.BlockSpec((1, k), lambda i, j: (0, 0)),
                  pl.BlockSpec((k, tn), lambda i, j: (0, j))],
        out_specs=[pl.BlockSpec((tm, tn), lambda i, j: (i, j)) for _ in out_dtypes],
        out_shape=[jax.ShapeDtypeStruct((m, n), d) for d in out_dtypes],
        scratch_shapes=[pltpu.VMEM((tm, k), BF16)],
        compiler_params=_params(("arbitrary", "arbitrary"), vmem),
        name="norm_matmul",
    )(x, g.reshape(1, k), w)
    return outs


def _mm_res_kernel(*refs, n_pairs):
    lhs, ws = refs[:n_pairs], refs[n_pairs:2 * n_pairs]
    res_ref, out_ref = refs[2 * n_pairs], refs[2 * n_pairs + 1]
    acc = res_ref[...]
    for l, w in zip(lhs, ws):
        acc = acc + jnp.dot(l[...], w[...], preferred_element_type=F32)
    out_ref[...] = acc


def _matmul_residual(lhs_list, w_list, res, *, tm_pref=512, tn_pref=1024):
    m, n = res.shape
    tm, tn = _tile(m, tm_pref), _tile(n, tn_pref)
    ks = [l.shape[1] for l in lhs_list]
    vmem = sum(2 * tm * k * 2 + 2 * k * tn * 2 for k in ks) + 4 * tm * tn * 4
    in_specs = ([pl.BlockSpec((tm, k), lambda i, j: (i, 0)) for k in ks]
                + [pl.BlockSpec((k, tn), lambda i, j: (0, j)) for k in ks]
                + [pl.BlockSpec((tm, tn), lambda i, j: (i, j))])
    return pl.pallas_call(
        functools.partial(_mm_res_kernel, n_pairs=len(ks)),
        grid=(m // tm, n // tn),
        in_specs=in_specs,
        out_specs=pl.BlockSpec((tm, tn), lambda i, j: (i, j)),
        out_shape=jax.ShapeDtypeStruct((m, n), F32),
        compiler_params=_params(("arbitrary", "arbitrary"), vmem),
        name="matmul_residual",
    )(*lhs_list, *w_list, res)


def _final_norm_kernel(x_ref, g_ref, o_ref):
    x = x_ref[...]
    ms = jnp.mean(x * x, axis=-1, keepdims=True)
    o_ref[...] = x * lax.rsqrt(ms + EPS) * g_ref[...]


def _final_norm(x, g):
    m, d = x.shape
    tm = _tile(m, 256)
    return pl.pallas_call(
        _final_norm_kernel,
        grid=(m // tm,),
        in_specs=[pl.BlockSpec((tm, d), lambda i: (i, 0)), pl.BlockSpec((1, d), lambda i: (0, 0))],
        out_specs=pl.BlockSpec((tm, d), lambda i: (i, 0)),
        out_shape=jax.ShapeDtypeStruct((m, d), F32),
        compiler_params=_params(("arbitrary",), 4 * tm * d * 4),
        name="final_norm",
    )(x, g.reshape(1, d))


SCAN_ROWS = 256
SCAN_STREAMS = 4


def _scan_tiles(a, u, carry):
    r = a.shape[0]
    nt = r // SUBLANES
    a3 = a.reshape(nt, SUBLANES, LANES)
    u3 = u.reshape(nt, SUBLANES, LANES)
    sub = lax.broadcasted_iota(jnp.int32, a3.shape, 1)
    for d in (1, 2, 4):
        keep = sub >= d
        a_prev = pltpu.roll(a3, d, 1)
        u_prev = pltpu.roll(u3, d, 1)
        u3 = jnp.where(keep, a3 * u_prev + u3, u3)
        a3 = jnp.where(keep, a3 * a_prev, a3)
    hs = []
    for t in range(nt):
        h_t = u3[t] + a3[t] * carry
        carry = h_t[SUBLANES - 1:SUBLANES, :]
        hs.append(h_t)
    return jnp.concatenate(hs, axis=0), carry


def _rglru_kernel(xa_ref, ga_ref, h0_ref, sc_ref, cw_ref, cb_ref, wr_ref, br_ref, wi_ref, bi_ref, lam_ref,
                  out_ref, hp_ref, hs_ref, a_scr, u_scr, *, bp, t, bs, ts):
    mp = bp * t
    m = mp + bs * ts
    cw = cw_ref[...]
    cb = cb_ref[...]

    def conv(ext, n):
        y = cb + ext[5:5 + n] * cw[0:1]
        for k in range(1, CONV_W):
            y = y + ext[5 + k:5 + k + n] * cw[k:k + 1]
        return y

    rp = _tile(t, SCAN_ROWS)
    for b in range(bp):
        def conv_body(c, tail, b=b):
            rows = pl.ds(pl.multiple_of(b * t + c * rp, SUBLANES), rp)
            x = xa_ref[rows, :]
            u_scr[rows, :] = conv(jnp.concatenate([tail, x], axis=0), rp)
            return x[rp - SUBLANES:, :]

        lax.fori_loop(0, t // rp, conv_body, jnp.zeros((SUBLANES, LANES), F32))

    def conv_sample(s, _):
        rows = pl.ds(pl.multiple_of(mp + s * ts, SUBLANES), ts)
        x = xa_ref[rows, :]
        u_scr[rows, :] = conv(jnp.concatenate([sc_ref[s], x], axis=0), ts)
        return 0

    lax.fori_loop(0, bs, conv_sample, 0)

    wr = wr_ref[...].astype(BF16)
    wi = wi_ref[...].astype(BF16)
    br = br_ref[...]
    bi = bi_ref[...]
    neg_c_sp = -RG_C * _softplus(-lam_ref[...])
    rg = _tile(m, SCAN_ROWS)

    def gate_body(c, _):
        rows = pl.ds(pl.multiple_of(c * rg, SUBLANES), rg)
        xc = u_scr[rows, :]
        xb = xc.astype(BF16)
        r = _sigmoid(jnp.dot(xb, wr, preferred_element_type=F32) + br)
        i = _sigmoid(jnp.dot(xb, wi, preferred_element_type=F32) + bi)
        log_a = r * neg_c_sp
        a = jnp.exp(log_a)
        a_scr[rows, :] = a
        u_scr[rows, :] = jnp.sqrt(-jnp.tanh(log_a) * (1.0 + a * a)) * (i * xc)
        return 0

    lax.fori_loop(0, m // rg, gate_body, 0)

    def emit(rows, a, u, carry):
        h, carry = _scan_tiles(a, u, carry)
        u_scr[rows, :] = h
        g = ga_ref[rows, :]
        out_ref[rows, :] = (h * (g * _sigmoid(g))).astype(out_ref.dtype)
        return carry

    def scan_prompt(c, carries):
        new = []
        for b in range(bp):
            rows = pl.ds(pl.multiple_of(b * t + c * rp, SUBLANES), rp)
            new.append(emit(rows, a_scr[rows, :], u_scr[rows, :], carries[b]))
        return tuple(new)

    lax.fori_loop(0, t // rp, scan_prompt, tuple(jnp.zeros((1, LANES), F32) for _ in range(bp)))

    ns = _tile(bs, SCAN_STREAMS)

    def scan_sample(c, _):
        for k in range(ns):
            s = c * ns + k
            rows = pl.ds(pl.multiple_of(mp + s * ts, SUBLANES), ts)
            emit(rows, a_scr[rows, :], u_scr[rows, :], h0_ref[pl.ds(s, 1), :])
        return 0

    lax.fori_loop(0, bs // ns, scan_sample, 0)

    hp_ref[...] = u_scr[pl.ds(t - 1, bp, stride=t), :]
    hs_ref[...] = u_scr[pl.ds(mp + ts - 1, bs, stride=ts), :]


def _rglru(proj, h0, sc_pad, cw, cb, wr, br, wi, bi, lam, *, bp, t, bs, ts, d_rnn):
    m = proj.shape[0]
    nb = d_rnn // LANES
    row = lambda v: v.reshape(1, d_rnn)
    vec_spec = pl.BlockSpec((1, LANES), lambda n: (0, n))
    w_spec = pl.BlockSpec((None, LANES, LANES), lambda n: (n, 0, 0))
    vmem = 2 * (2 * m * LANES * 4 + m * LANES * 2) + 2 * m * LANES * 4
    return pl.pallas_call(
        functools.partial(_rglru_kernel, bp=bp, t=t, bs=bs, ts=ts),
        grid=(nb,),
        in_specs=[pl.BlockSpec((m, LANES), lambda n: (0, n)),
                  pl.BlockSpec((m, LANES), lambda n: (0, nb + n)),
                  pl.BlockSpec((bs, LANES), lambda n: (0, n)),
                  pl.BlockSpec((bs, SUBLANES, LANES), lambda n: (0, 0, n)),
                  pl.BlockSpec((CONV_W, LANES), lambda n: (0, n)),
                  vec_spec, w_spec, vec_spec, w_spec, vec_spec, vec_spec],
        out_specs=[pl.BlockSpec((m, LANES), lambda n: (0, n)),
                   pl.BlockSpec((bp, LANES), lambda n: (0, n)),
                   pl.BlockSpec((bs, LANES), lambda n: (0, n))],
        out_shape=[jax.ShapeDtypeStruct((m, d_rnn), BF16),
                   jax.ShapeDtypeStruct((bp, d_rnn), F32),
                   jax.ShapeDtypeStruct((bs, d_rnn), F32)],
        scratch_shapes=[pltpu.VMEM((m, LANES), F32), pltpu.VMEM((m, LANES), F32)],
        compiler_params=_params(("arbitrary",), vmem),
        name="rglru_mixer",
    )(proj, proj, h0, sc_pad, cw, row(cb), wr, row(br), wi, row(bi), row(lam))


POOL_TAIL = 16
POOL_ROWS = 128


def _pool_kernel(xb_ref, gb_ref, sp_ref, pw_ref, ps_ref, out_ref, m_scr, tail_scr, *, npb, bpt, ts):
    g = pl.program_id(0)
    rb = pl.program_id(1)
    rbk, gw = xb_ref.shape
    wf = lax.shift_left(jnp.int32(2), g).astype(F32)

    def window_means(ext, n, pos0):
        s2 = ext[1:] + ext[:-1]
        s4 = s2[2:] + s2[:-2]
        s8 = s4[4:] + s4[:-4]
        s16 = s8[8:] + s8[:-8]
        x = ext[POOL_TAIL:]
        win = jnp.where(g == 0, s2[POOL_TAIL - 1:],
                        jnp.where(g == 1, s4[POOL_TAIL - 3:],
                                  jnp.where(g == 2, s8[POOL_TAIL - 7:], s16[POOL_TAIL - 15:])))
        if pos0 is None:
            cnt = wf
        else:
            pos = pos0 + lax.broadcasted_iota(jnp.int32, (n, gw), 0)
            cnt = jnp.minimum(wf, (pos + 1).astype(F32))
        return win / cnt - x

    @pl.when(rb < npb)
    def _():
        blk = rb % bpt

        @pl.when(blk == 0)
        def _():
            tail_scr[...] = jnp.zeros_like(tail_scr)

        rc = _tile(rbk, POOL_ROWS)

        def body(c, tail):
            rows = pl.ds(pl.multiple_of(c * rc, SUBLANES), rc)
            x = xb_ref[rows, :]
            ext = jnp.concatenate([tail, x], axis=0)
            m_scr[rows, :] = window_means(ext, rc, blk * rbk + c * rc).astype(BF16)
            return x[rc - POOL_TAIL:, :]

        tail_scr[...] = lax.fori_loop(0, rbk // rc, body, tail_scr[...])

    @pl.when(rb >= npb)
    def _():
        s0 = (rb - npb) * (rbk // ts)

        def body(s, _):
            rows = pl.ds(pl.multiple_of(s * ts, SUBLANES), ts)
            ext = jnp.concatenate([sp_ref[s0 + s], xb_ref[rows, :]], axis=0)
            m_scr[rows, :] = window_means(ext, ts, None).astype(BF16)
            return 0

        lax.fori_loop(0, rbk // ts, body, 0)

    y = jnp.dot(m_scr[...], pw_ref[...].astype(BF16), preferred_element_type=F32) * ps_ref[...]
    gt = gb_ref[...]
    out_ref[...] = (y * (gt * _sigmoid(gt))).astype(out_ref.dtype)


def _pool(proj, sp_pad, pw, ps, *, bp, t, bs, ts, d_rnn, d_pool):
    m = proj.shape[0]
    ng = len(POOL_WINDOWS)
    gw = d_pool // ng
    ms = bs * ts
    rbk = min(1024, t, ms)
    while t % rbk or ms % rbk:
        rbk //= 2
    npb = bp * t // rbk
    xcol = 2 * d_rnn // gw
    gcol = (2 * d_rnn + d_pool) // gw
    vmem = 2 * (2 * rbk * gw * 4 + rbk * gw * 2) + rbk * gw * 2 + 2 * bs * POOL_TAIL * gw * 4 + 2 * gw * gw * 4
    return pl.pallas_call(
        functools.partial(_pool_kernel, npb=npb, bpt=t // rbk, ts=ts),
        grid=(ng, m // rbk),
        in_specs=[pl.BlockSpec((rbk, gw), lambda g, r: (r, xcol + g)),
                  pl.BlockSpec((rbk, gw), lambda g, r: (r, gcol + g)),
                  pl.BlockSpec((bs, POOL_TAIL, gw), lambda g, r: (0, 0, g)),
                  pl.BlockSpec((None, gw, gw), lambda g, r: (g, 0, 0)),
                  pl.BlockSpec((1, gw), lambda g, r: (0, g))],
        out_specs=pl.BlockSpec((rbk, gw), lambda g, r: (r, g)),
        out_shape=jax.ShapeDtypeStruct((m, d_pool), BF16),
        scratch_shapes=[pltpu.VMEM((rbk, gw), BF16), pltpu.VMEM((POOL_TAIL, gw), F32)],
        compiler_params=_params(("arbitrary", "arbitrary"), vmem),
        name="pool_mixer",
    )(proj, proj, sp_pad, pw, ps.reshape(1, d_pool))


def _suffix_matrix(n):
    j = lax.broadcasted_iota(jnp.int32, (n, n), 0)
    s = lax.broadcasted_iota(jnp.int32, (n, n), 1)
    return jnp.concatenate([(j > s).astype(BF16), jnp.ones((n, LANES), BF16)], axis=1)


def _split_dot(x, u):
    hi = x.astype(BF16)
    lo = (x - hi.astype(F32)).astype(BF16)
    return jnp.dot(hi, u, preferred_element_type=F32) + jnp.dot(lo, u, preferred_element_type=F32)


def _qk(q, k):
    return lax.dot_general(q, k, (((1,), (1,)), ((), ())), preferred_element_type=F32) * ATT_SCALE


def _attn_prompt_kernel(q_ref, k_ref, v_ref, gt_ref, u_ref, o_ref):
    qi = pl.program_id(2)
    tq = q_ref.shape[0]
    q = q_ref[...]
    u = u_ref[...]

    def tile(j, acc, carry, masked):
        rows = pl.ds(pl.multiple_of(j * tq, tq), tq)
        z = _qk(q, k_ref[rows, :])
        sp = _softplus(z)
        l1m = -sp
        if masked:
            mask = (lax.broadcasted_iota(jnp.int32, z.shape, 1) < lax.broadcasted_iota(jnp.int32, z.shape, 0))
            l1m = jnp.where(mask, l1m, 0.0)
        st = _split_dot(l1m, u)
        suf, tot = st[:, :tq], st[:, tq:]
        w = jnp.exp(z - sp + suf + jnp.concatenate([carry] * (tq // LANES), axis=1))
        if masked:
            w = jnp.where(mask, w, 0.0)
        acc = acc + jnp.dot(w.astype(BF16), v_ref[rows, :], preferred_element_type=F32)
        return acc, carry + tot

    zero = jnp.zeros((tq, LANES), F32)
    acc, carry = tile(qi, zero, zero, True)
    acc, carry = lax.fori_loop(0, qi, lambda i, c: tile(qi - 1 - i, c[0], c[1], False), (acc, carry))
    g = gt_ref[...]
    o_ref[...] = (acc * (g * _sigmoid(g))).astype(o_ref.dtype)


def _attn_prompt(qkv, proj, *, bp, t, n_heads):
    tq = _tile(t, 256)
    nq = t // tq
    d_att = n_heads * HEAD_DIM
    u = _suffix_matrix(tq)
    vmem = 2 * (tq * LANES * 2 + 2 * t * LANES * 2 + tq * LANES * 4 + tq * (tq + LANES) * 2 + tq * LANES * 2)
    return pl.pallas_call(
        _attn_prompt_kernel,
        grid=(bp, n_heads, nq),
        in_specs=[pl.BlockSpec((tq, LANES), lambda b, h, i: (b * nq + i, h)),
                  pl.BlockSpec((t, LANES), lambda b, h, i: (b, n_heads + h)),
                  pl.BlockSpec((t, LANES), lambda b, h, i: (b, 2 * n_heads + h)),
                  pl.BlockSpec((tq, LANES), lambda b, h, i: (b * nq + i, 3 * n_heads + h)),
                  pl.BlockSpec((tq, tq + LANES), lambda b, h, i: (0, 0))],
        out_specs=pl.BlockSpec((tq, LANES), lambda b, h, i: (b * nq + i, h)),
        out_shape=jax.ShapeDtypeStruct((bp * t, d_att), BF16),
        compiler_params=_params(("arbitrary", "arbitrary", "arbitrary"), vmem),
        name="attn_prompt",
    )(qkv, qkv, qkv, proj, u)


def _attn_sample_kernel(q_ref, kn_ref, vn_ref, gt_ref, kc_ref, vc_ref, u_ref, o_ref):
    ts = q_ref.shape[0]
    p = kc_ref.shape[0]
    nblk = p // LANES
    q = q_ref[...]
    u = u_ref[...]
    pad = jnp.zeros((LANES - ts, LANES), BF16)
    kn = jnp.concatenate([kn_ref[...], pad], axis=0)
    vn = jnp.concatenate([vn_ref[...], pad], axis=0)
    kc = kc_ref[...].astype(BF16)
    vc = vc_ref[...].astype(BF16)

    z = jnp.concatenate([_qk(q, kc), _qk(q, kn)], axis=1)
    sp = _softplus(z)
    col = lax.broadcasted_iota(jnp.int32, z.shape, 1)
    row = lax.broadcasted_iota(jnp.int32, z.shape, 0)
    mask = col < p + row
    l1m = jnp.where(mask, -sp, 0.0)
    blocks = [l1m[:, LANES * j:LANES * (j + 1)] for j in range(nblk + 1)]
    st = _split_dot(jnp.concatenate(blocks, axis=0), u)
    sufs, carries = [None] * (nblk + 1), [None] * (nblk + 1)
    carry = jnp.zeros((ts, LANES), F32)
    for j in range(nblk, -1, -1):
        sufs[j] = st[ts * j:ts * (j + 1), :LANES]
        carries[j] = carry
        carry = carry + st[ts * j:ts * (j + 1), LANES:]
    w = jnp.exp(z - sp + jnp.concatenate(sufs, axis=1) + jnp.concatenate(carries, axis=1))
    w = jnp.where(mask, w, 0.0).astype(BF16)
    acc = (jnp.dot(w[:, :p], vc, preferred_element_type=F32)
           + jnp.dot(w[:, p:], vn, preferred_element_type=F32))
    g = gt_ref[...]
    o_ref[...] = (acc * (g * _sigmoid(g))).astype(o_ref.dtype)


def _attn_sample(qkv, proj, cache_k, cache_v, *, mp, bs, ts, n_heads):
    p = cache_k.shape[1]
    d_att = n_heads * HEAD_DIM
    r0 = mp // ts
    u = _suffix_matrix(LANES)
    vmem = 2 * (2 * p * LANES * 4 + 3 * ts * LANES * 2 + ts * LANES * 4 + LANES * 2 * LANES * 2 + ts * LANES * 2)
    new_spec = lambda c: pl.BlockSpec((ts, LANES), lambda b, h: (r0 + b, c * n_heads + h))
    cache_spec = pl.BlockSpec((None, p, LANES), lambda b, h: (b, 0, h))
    return pl.pallas_call(
        _attn_sample_kernel,
        grid=(bs, n_heads),
        in_specs=[new_spec(0), new_spec(1), new_spec(2), new_spec(3), cache_spec, cache_spec,
                  pl.BlockSpec((LANES, 2 * LANES), lambda b, h: (0, 0))],
        out_specs=pl.BlockSpec((ts, LANES), lambda b, h: (b, h)),
        out_shape=jax.ShapeDtypeStruct((bs * ts, d_att), BF16),
        compiler_params=_params(("arbitrary", "arbitrary"), vmem),
        name="attn_sample",
    )(qkv, qkv, qkv, proj, cache_k.reshape(bs, p, d_att), cache_v.reshape(bs, p, d_att), u)


@jax.jit
def _step(x_prompt, x_sample, cache_k, cache_v, state_h, state_conv, state_pool,
          norm_rec, w_in_rec, conv_w, conv_b, gate_r_w, gate_r_b, gate_i_w, gate_i_b, rg_lambda,
          pool_w, pool_scale, w_out_rec, norm_att, w_in_att, w_out_att, norm_final):
    bp, t, d = x_prompt.shape
    bs, ts, _ = x_sample.shape
    n_rec, n_att = norm_rec.shape[0], norm_att.shape[0]
    d_rnn = state_h.shape[-1]
    d_pool = state_pool.shape[-1]
    n_heads = cache_k.shape[3]
    d_att = n_heads * HEAD_DIM
    mp = bp * t
    assert ts >= POOL_BUF and ts % SUBLANES == 0 and t % ts == 0 and cache_k.shape[4] == HEAD_DIM
    assert gate_r_w.shape[2] == LANES and d_pool // len(POOL_WINDOWS) == 2 * LANES

    x = jnp.concatenate([x_prompt.reshape(mp, d), x_sample.reshape(bs * ts, d)], axis=0)
    sc_pad = jnp.pad(state_conv, ((0, 0), (0, 0), (SUBLANES - (CONV_W - 1), 0), (0, 0)))
    sp_pad = jnp.pad(state_pool, ((0, 0), (0, 0), (POOL_TAIL - POOL_BUF, 0), (0, 0)))

    outs = {k: [] for k in ("kp", "vp", "hp", "cp", "pp", "ks", "vs", "hs", "cs", "ps")}
    for layer in range(n_rec + n_att):
        j = layer // 2
        if layer % 2 == 0:
            (proj,) = _norm_matmul(x, norm_rec[j], w_in_rec[j].astype(BF16), (F32,))
            ya, hp, hs = _rglru(proj, state_h[j], sc_pad[j], conv_w[j], conv_b[j], gate_r_w[j], gate_r_b[j],
                                gate_i_w[j], gate_i_b[j], rg_lambda[j], bp=bp, t=t, bs=bs, ts=ts, d_rnn=d_rnn)
            yb = _pool(proj, sp_pad[j], pool_w[j], pool_scale[j], bp=bp, t=t, bs=bs, ts=ts,
                       d_rnn=d_rnn, d_pool=d_pool)
            w_out = w_out_rec[j].astype(BF16)
            x = _matmul_residual([ya, yb], [w_out[:d_rnn], w_out[d_rnn:]], x)
            pp3 = proj[:mp].reshape(bp, t, -1)
            ps3 = proj[mp:].reshape(bs, ts, -1)
            outs["hp"].append(hp)
            outs["hs"].append(hs)
            outs["cp"].append(pp3[:, t - (CONV_W - 1):, :d_rnn])
            outs["cs"].append(ps3[:, ts - (CONV_W - 1):, :d_rnn])
            outs["pp"].append(pp3[:, t - POOL_BUF:, 2 * d_rnn:2 * d_rnn + d_pool])
            outs["ps"].append(ps3[:, ts - POOL_BUF:, 2 * d_rnn:2 * d_rnn + d_pool])
        else:
            proj, qkv = _norm_matmul(x, norm_att[j], w_in_att[j].astype(BF16), (F32, BF16))
            op = _attn_prompt(qkv, proj, bp=bp, t=t, n_heads=n_heads)
            os_ = _attn_sample(qkv, proj, cache_k[j], cache_v[j], mp=mp, bs=bs, ts=ts, n_heads=n_heads)
            x = _matmul_residual([jnp.concatenate([op, os_], axis=0)], [w_out_att[j].astype(BF16)], x)
            k_all = proj[:, d_att:2 * d_att]
            v_all = proj[:, 2 * d_att:3 * d_att]
            outs["kp"].append(k_all[:mp].reshape(bp, t, n_heads, HEAD_DIM))
            outs["vp"].append(v_all[:mp].reshape(bp, t, n_heads, HEAD_DIM))
            outs["ks"].append(k_all[mp:].reshape(bs, ts, n_heads, HEAD_DIM))
            outs["vs"].append(v_all[mp:].reshape(bs, ts, n_heads, HEAD_DIM))

    y = _final_norm(x, norm_final)
    st = {k: jnp.stack(v) for k, v in outs.items()}
    return (y[:mp].reshape(bp, t, d), y[mp:].reshape(bs, ts, d),
            st["kp"], st["vp"], st["hp"], st["cp"], st["pp"],
            st["ks"], st["vs"], st["hs"], st["cs"], st["ps"])


def kernel(x_prompt, x_sample, cache_k, cache_v, state_h, state_conv, state_pool, norm_rec, w_in_rec, conv_w, conv_b, gate_r_w, gate_r_b, gate_i_w, gate_i_b, rg_lambda, pool_w, pool_scale, w_out_rec, norm_att, w_in_att, w_out_att, norm_final):
    return _step(x_prompt, x_sample, cache_k, cache_v, state_h, state_conv, state_pool, norm_rec, w_in_rec,
                 conv_w, conv_b, gate_r_w, gate_r_b, gate_i_w, gate_i_b, rg_lambda, pool_w, pool_scale,
                 w_out_rec, norm_att, w_in_att, w_out_att, norm_final)
```

```python
import functools

import jax
import jax.numpy as jnp
from jax import lax
from jax.experimental import pallas as pl
from jax.experimental.pallas import tpu as pltpu

F32 = jnp.float32
BF16 = jnp.bfloat16

EPS = 1e-6
RG_C = 8.0
CONV_W = 4
POOL_WINDOWS = (2, 4, 8, 16)
POOL_BUF = max(POOL_WINDOWS) - 1
HEAD_DIM = 128
ATT_SCALE = HEAD_DIM ** -0.5
EXP_UNDERFLOW = -105.0

LANES = 128
SUBLANES = 8
VMEM_LIMIT_CAP = 60000 * 1024
VMEM_SLACK = 8 * 1024 * 1024


def _params(semantics, buffer_bytes):
    limit = min(VMEM_LIMIT_CAP, buffer_bytes + VMEM_SLACK)
    return pltpu.CompilerParams(dimension_semantics=semantics, vmem_limit_bytes=limit)


def _tile(n, pref):
    t = min(n, pref)
    while n % t:
        t //= 2
    return t


def _sigmoid(x):
    return 1.0 / (1.0 + jnp.exp(-x))


def _softplus(x):
    return jnp.maximum(x, 0.0) + jnp.log(1.0 + jnp.exp(-jnp.abs(x)))


NORM_ROWS = 128


def _norm_mm_kernel(x_ref, g_ref, w_ref, *refs, n_out):
    out_refs, xn_ref = refs[:n_out], refs[n_out]
    tm = x_ref.shape[0]

    @pl.when(pl.program_id(1) == 0)
    def _():
        g = g_ref[...]

        def body(c, _):
            rows = pl.ds(pl.multiple_of(c * NORM_ROWS, NORM_ROWS), NORM_ROWS)
            x = x_ref[rows, :]
            ms = jnp.mean(x * x, axis=-1, keepdims=True)
            xn_ref[rows, :] = (x * lax.rsqrt(ms + EPS) * g).astype(BF16)
            return 0

        lax.fori_loop(0, tm // NORM_ROWS, body, 0)

    acc = jnp.dot(xn_ref[...], w_ref[...], preferred_element_type=F32)
    for o in out_refs:
        o[...] = acc.astype(o.dtype)


def _norm_matmul(x, g, w, out_dtypes, *, tm_pref=1024, tn_pref=512):
    m, k = x.shape
    n = w.shape[1]
    tm, tn = _tile(m, tm_pref), _tile(n, tn_pref)
    out_bytes = sum(jnp.dtype(d).itemsize for d in out_dtypes)
    vmem = 2 * tm * k * 4 + tm * k * 2 + 2 * k * tn * 2 + 2 * tm * tn * out_bytes
    outs = pl.pallas_call(
        functools.partial(_norm_mm_kernel, n_out=len(out_dtypes)),
        grid=(m // tm, n // tn),
        in_specs=[pl.BlockSpec((tm, k), lambda i, j: (i, 0)),
                  pl.BlockSpec((1, k), lambda i, j: (0, 0)),
                  pl.BlockSpec((k, tn), lambda i, j: (0, j))],
        out_specs=[pl.BlockSpec((tm, tn), lambda i, j: (i, j)) for _ in out_dtypes],
        out_shape=[jax.ShapeDtypeStruct((m, n), d) for d in out_dtypes],
        scratch_shapes=[pltpu.VMEM((tm, k), BF16)],
        compiler_params=_params(("arbitrary", "arbitrary"), vmem),
        name="norm_matmul",
    )(x, g.reshape(1, k), w)
    return outs


def _mm_res_kernel(*refs, n_pairs):
    lhs, ws = refs[:n_pairs], refs[n_pairs:2 * n_pairs]
    res_ref, out_ref = refs[2 * n_pairs], refs[2 * n_pairs + 1]
    acc = res_ref[...]
    for l, w in zip(lhs, ws):
        acc = acc + jnp.dot(l[...], w[...], preferred_element_type=F32)
    out_ref[...] = acc


def _matmul_residual(lhs_list, w_list, res, *, tm_pref=512, tn_pref=1024):
    m, n = res.shape
    tm, tn = _tile(m, tm_pref), _tile(n, tn_pref)
    ks = [l.shape[1] for l in lhs_list]
    vmem = sum(2 * tm * k * 2 + 2 * k * tn * 2 for k in ks) + 4 * tm * tn * 4
    in_specs = ([pl.BlockSpec((tm, k), lambda i, j: (i, 0)) for k in ks]
                + [pl.BlockSpec((k, tn), lambda i, j: (0, j)) for k in ks]
                + [pl.BlockSpec((tm, tn), lambda i, j: (i, j))])
    return pl.pallas_call(
        functools.partial(_mm_res_kernel, n_pairs=len(ks)),
        grid=(m // tm, n // tn),
        in_specs=in_specs,
        out_specs=pl.BlockSpec((tm, tn), lambda i, j: (i, j)),
        out_shape=jax.ShapeDtypeStruct((m, n), F32),
        compiler_params=_params(("arbitrary", "arbitrary"), vmem),
        name="matmul_residual",
    )(*lhs_list, *w_list, res)


def _final_norm_kernel(x_ref, g_ref, o_ref):
    x = x_ref[...]
    ms = jnp.mean(x * x, axis=-1, keepdims=True)
    o_ref[...] = x * lax.rsqrt(ms + EPS) * g_ref[...]


def _final_norm(x, g):
    m, d = x.shape
    tm = _tile(m, 256)
    return pl.pallas_call(
        _final_norm_kernel,
        grid=(m // tm,),
        in_specs=[pl.BlockSpec((tm, d), lambda i: (i, 0)), pl.BlockSpec((1, d), lambda i: (0, 0))],
        out_specs=pl.BlockSpec((tm, d), lambda i: (i, 0)),
        out_shape=jax.ShapeDtypeStruct((m, d), F32),
        compiler_params=_params(("arbitrary",), 4 * tm * d * 4),
        name="final_norm",
    )(x, g.reshape(1, d))


SCAN_ROWS = 256
SCAN_STREAMS = 4


def _scan_tiles(a, u, carry):
    r = a.shape[0]
    nt = r // SUBLANES
    a3 = a.reshape(nt, SUBLANES, LANES)
    u3 = u.reshape(nt, SUBLANES, LANES)
    sub = lax.broadcasted_iota(jnp.int32, a3.shape, 1)
    for d in (1, 2, 4):
        keep = sub >= d
        a_prev = pltpu.roll(a3, d, 1)
        u_prev = pltpu.roll(u3, d, 1)
        u3 = jnp.where(keep, a3 * u_prev + u3, u3)
        a3 = jnp.where(keep, a3 * a_prev, a3)
    hs = []
    for t in range(nt):
        h_t = u3[t] + a3[t] * carry
        carry = h_t[SUBLANES - 1:SUBLANES, :]
        hs.append(h_t)
    return jnp.concatenate(hs, axis=0), carry


def _rglru_kernel(xa_ref, ga_ref, h0_ref, sc_ref, cw_ref, cb_ref, wr_ref, br_ref, wi_ref, bi_ref, lam_ref,
                  out_ref, hp_ref, hs_ref, a_scr, u_scr, *, bp, t, bs, ts):
    mp = bp * t
    m = mp + bs * ts
    cw = cw_ref[...]
    cb = cb_ref[...]

    def conv(ext, n):
        y = cb + ext[5:5 + n] * cw[0:1]
        for k in range(1, CONV_W):
            y = y + ext[5 + k:5 + k + n] * cw[k:k + 1]
        return y

    rp = _tile(t, SCAN_ROWS)
    for b in range(bp):
        def conv_body(c, tail, b=b):
            rows = pl.ds(pl.multiple_of(b * t + c * rp, SUBLANES), rp)
            x = xa_ref[rows, :]
            u_scr[rows, :] = conv(jnp.concatenate([tail, x], axis=0), rp)
            return x[rp - SUBLANES:, :]

        lax.fori_loop(0, t // rp, conv_body, jnp.zeros((SUBLANES, LANES), F32))

    def conv_sample(s, _):
        rows = pl.ds(pl.multiple_of(mp + s * ts, SUBLANES), ts)
        x = xa_ref[rows, :]
        u_scr[rows, :] = conv(jnp.concatenate([sc_ref[s], x], axis=0), ts)
        return 0

    lax.fori_loop(0, bs, conv_sample, 0)

    wr = wr_ref[...].astype(BF16)
    wi = wi_ref[...].astype(BF16)
    br = br_ref[...]
    bi = bi_ref[...]
    neg_c_sp = -RG_C * _softplus(-lam_ref[...])
    rg = _tile(m, SCAN_ROWS)

    def gate_body(c, _):
        rows = pl.ds(pl.multiple_of(c * rg, SUBLANES), rg)
        xc = u_scr[rows, :]
        xb = xc.astype(BF16)
        r = _sigmoid(jnp.dot(xb, wr, preferred_element_type=F32) + br)
        i = _sigmoid(jnp.dot(xb, wi, preferred_element_type=F32) + bi)
        log_a = r * neg_c_sp
        a = jnp.exp(log_a)
        a_scr[rows, :] = a
        u_scr[rows, :] = jnp.sqrt(-jnp.tanh(log_a) * (1.0 + a * a)) * (i * xc)
        return 0

    lax.fori_loop(0, m // rg, gate_body, 0)

    def emit(rows, a, u, carry):
        h, carry = _scan_tiles(a, u, carry)
        u_scr[rows, :] = h
        g = ga_ref[rows, :]
        out_ref[rows, :] = (h * (g * _sigmoid(g))).astype(out_ref.dtype)
        return carry

    def scan_prompt(c, carries):
        new = []
        for b in range(bp):
            rows = pl.ds(pl.multiple_of(b * t + c * rp, SUBLANES), rp)
            new.append(emit(rows, a_scr[rows, :], u_scr[rows, :], carries[b]))
        return tuple(new)

    lax.fori_loop(0, t // rp, scan_prompt, tuple(jnp.zeros((1, LANES), F32) for _ in range(bp)))

    ns = _tile(bs, SCAN_STREAMS)

    def scan_sample(c, _):
        for k in range(ns):
            s = c * ns + k
            rows = pl.ds(pl.multiple_of(mp + s * ts, SUBLANES), ts)
            emit(rows, a_scr[rows, :], u_scr[rows, :], h0_ref[pl.ds(s, 1), :])
        return 0

    lax.fori_loop(0, bs // ns, scan_sample, 0)

    hp_ref[...] = u_scr[pl.ds(t - 1, bp, stride=t), :]
    hs_ref[...] = u_scr[pl.ds(mp + ts - 1, bs, stride=ts), :]


def _rglru(proj, h0, sc_pad, cw, cb, wr, br, wi, bi, lam, *, bp, t, bs, ts, d_rnn):
    m = proj.shape[0]
    nb = d_rnn // LANES
    row = lambda v: v.reshape(1, d_rnn)
    vec_spec = pl.BlockSpec((1, LANES), lambda n: (0, n))
    w_spec = pl.BlockSpec((None, LANES, LANES), lambda n: (n, 0, 0))
    vmem = 2 * (2 * m * LANES * 4 + m * LANES * 2) + 2 * m * LANES * 4
    return pl.pallas_call(
        functools.partial(_rglru_kernel, bp=bp, t=t, bs=bs, ts=ts),
        grid=(nb,),
        in_specs=[pl.BlockSpec((m, LANES), lambda n: (0, n)),
                  pl.BlockSpec((m, LANES), lambda n: (0, nb + n)),
                  pl.BlockSpec((bs, LANES), lambda n: (0, n)),
                  pl.BlockSpec((bs, SUBLANES, LANES), lambda n: (0, 0, n)),
                  pl.BlockSpec((CONV_W, LANES), lambda n: (0, n)),
                  vec_spec, w_spec, vec_spec, w_spec, vec_spec, vec_spec],
        out_specs=[pl.BlockSpec((m, LANES), lambda n: (0, n)),
                   pl.BlockSpec((bp, LANES), lambda n: (0, n)),
                   pl.BlockSpec((bs, LANES), lambda n: (0, n))],
        out_shape=[jax.ShapeDtypeStruct((m, d_rnn), BF16),
                   jax.ShapeDtypeStruct((bp, d_rnn), F32),
                   jax.ShapeDtypeStruct((bs, d_rnn), F32)],
        scratch_shapes=[pltpu.VMEM((m, LANES), F32), pltpu.VMEM((m, LANES), F32)],
        compiler_params=_params(("arbitrary",), vmem),
        name="rglru_mixer",
    )(proj, proj, h0, sc_pad, cw, row(cb), wr, row(br), wi, row(bi), row(lam))


POOL_TAIL = 16
POOL_ROWS = 128


def _pool_kernel(xb_ref, gb_ref, sp_ref, pw_ref, ps_ref, out_ref, m_scr, tail_scr, *, npb, bpt, ts):
    g = pl.program_id(0)
    rb = pl.program_id(1)
    rbk, gw = xb_ref.shape
    wf = lax.shift_left(jnp.int32(2), g).astype(F32)

    def window_means(ext, n, pos0):
        s2 = ext[1:] + ext[:-1]
        s4 = s2[2:] + s2[:-2]
        s8 = s4[4:] + s4[:-4]
        s16 = s8[8:] + s8[:-8]
        x = ext[POOL_TAIL:]
        win = jnp.where(g == 0, s2[POOL_TAIL - 1:],
                        jnp.where(g == 1, s4[POOL_TAIL - 3:],
                                  jnp.where(g == 2, s8[POOL_TAIL - 7:], s16[POOL_TAIL - 15:])))
        if pos0 is None:
            cnt = wf
        else:
            pos = pos0 + lax.broadcasted_iota(jnp.int32, (n, gw), 0)
            cnt = jnp.minimum(wf, (pos + 1).astype(F32))
        return win / cnt - x

    @pl.when(rb < npb)
    def _():
        blk = rb % bpt

        @pl.when(blk == 0)
        def _():
            tail_scr[...] = jnp.zeros_like(tail_scr)

        rc = _tile(rbk, POOL_ROWS)

        def body(c, tail):
            rows = pl.ds(pl.multiple_of(c * rc, SUBLANES), rc)
            x = xb_ref[rows, :]
            ext = jnp.concatenate([tail, x], axis=0)
            m_scr[rows, :] = window_means(ext, rc, blk * rbk + c * rc).astype(BF16)
            return x[rc - POOL_TAIL:, :]

        tail_scr[...] = lax.fori_loop(0, rbk // rc, body, tail_scr[...])

    @pl.when(rb >= npb)
    def _():
        s0 = (rb - npb) * (rbk // ts)

        def body(s, _):
            rows = pl.ds(pl.multiple_of(s * ts, SUBLANES), ts)
            ext = jnp.concatenate([sp_ref[s0 + s], xb_ref[rows, :]], axis=0)
            m_scr[rows, :] = window_means(ext, ts, None).astype(BF16)
            return 0

        lax.fori_loop(0, rbk // ts, body, 0)

    y = jnp.dot(m_scr[...], pw_ref[...].astype(BF16), preferred_element_type=F32) * ps_ref[...]
    gt = gb_ref[...]
    out_ref[...] = (y * (gt * _sigmoid(gt))).astype(out_ref.dtype)


def _pool(proj, sp_pad, pw, ps, *, bp, t, bs, ts, d_rnn, d_pool):
    m = proj.shape[0]
    ng = len(POOL_WINDOWS)
    gw = d_pool // ng
    ms = bs * ts
    rbk = min(1024, t, ms)
    while t % rbk or ms % rbk:
        rbk //= 2
    npb = bp * t // rbk
    xcol = 2 * d_rnn // gw
    gcol = (2 * d_rnn + d_pool) // gw
    vmem = 2 * (2 * rbk * gw * 4 + rbk * gw * 2) + rbk * gw * 2 + 2 * bs * POOL_TAIL * gw * 4 + 2 * gw * gw * 4
    return pl.pallas_call(
        functools.partial(_pool_kernel, npb=npb, bpt=t // rbk, ts=ts),
        grid=(ng, m // rbk),
        in_specs=[pl.BlockSpec((rbk, gw), lambda g, r: (r, xcol + g)),
                  pl.BlockSpec((rbk, gw), lambda g, r: (r, gcol + g)),
                  pl.BlockSpec((bs, POOL_TAIL, gw), lambda g, r: (0, 0, g)),
                  pl.BlockSpec((None, gw, gw), lambda g, r: (g, 0, 0)),
                  pl.BlockSpec((1, gw), lambda g, r: (0, g))],
        out_specs=pl.BlockSpec((rbk, gw), lambda g, r: (r, g)),
        out_shape=jax.ShapeDtypeStruct((m, d_pool), BF16),
        scratch_shapes=[pltpu.VMEM((rbk, gw), BF16), pltpu.VMEM((POOL_TAIL, gw), F32)],
        compiler_params=_params(("arbitrary", "arbitrary"), vmem),
        name="pool_mixer",
    )(proj, proj, sp_pad, pw, ps.reshape(1, d_pool))


def _suffix_matrix(n):
    j = lax.broadcasted_iota(jnp.int32, (n, n), 0)
    s = lax.broadcasted_iota(jnp.int32, (n, n), 1)
    return jnp.concatenate([(j > s).astype(BF16), jnp.ones((n, LANES), BF16)], axis=1)


def _split_dot(x, u):
    hi = x.astype(BF16)
    lo = (x - hi.astype(F32)).astype(BF16)
    return jnp.dot(hi, u, preferred_element_type=F32) + jnp.dot(lo, u, preferred_element_type=F32)


def _qk(q, k):
    return lax.dot_general(q, k, (((1,), (1,)), ((), ())), preferred_element_type=F32) * ATT_SCALE


def _strictly_earlier(shape):
    return lax.broadcasted_iota(jnp.int32, shape, 1) < lax.broadcasted_iota(jnp.int32, shape, 0)


def _sb_tile(q, k, v, u, carry, mask=None):
    tk = k.shape[0]
    z = _qk(q, k)
    sp = _softplus(z)
    l1m = -sp if mask is None else jnp.where(mask, -sp, 0.0)
    st = _split_dot(l1m, u)
    w = jnp.exp(z - sp + st[:, :tk] + jnp.concatenate([carry] * (tk // LANES), axis=1))
    if mask is not None:
        w = jnp.where(mask, w, 0.0)
    return jnp.dot(w.astype(BF16), v, preferred_element_type=F32), st[:, tk:]


def _attn_prompt_kernel(q_ref, k_ref, v_ref, gt_ref, u_ref, o_ref, *, tq):
    u = u_ref[...]
    zero = jnp.zeros((tq, LANES), F32)
    mask = _strictly_earlier((tq, tq))

    def kv(j):
        rows = pl.ds(pl.multiple_of(j * tq, tq), tq)
        return k_ref[rows, :], v_ref[rows, :]

    def qblock(qi, _):
        rows = pl.ds(pl.multiple_of(qi * tq, tq), tq)
        q = q_ref[rows, :]
        pv0, tot0 = _sb_tile(q, *kv(qi), u, zero, mask)
        pv1, tot1 = _sb_tile(q, *kv(jnp.maximum(qi - 1, 0)), u, tot0)
        has_prev = qi > 0
        acc = pv0 + jnp.where(has_prev, pv1, 0.0)
        carry = tot0 + jnp.where(has_prev, tot1, 0.0)

        def more(s):
            return jnp.logical_and(s[0] >= 0, jnp.max(s[2]) >= EXP_UNDERFLOW)

        def older(s):
            j, acc, carry = s
            pv, tot = _sb_tile(q, *kv(j), u, carry)
            return j - 1, acc + pv, carry + tot

        _, acc, _ = lax.while_loop(more, older, (qi - 2, acc, carry))
        g = gt_ref[rows, :]
        o_ref[rows, :] = (acc * (g * _sigmoid(g))).astype(o_ref.dtype)
        return 0

    lax.fori_loop(0, q_ref.shape[0] // tq, qblock, 0)


def _attn_prompt(qkv, proj, *, bp, t, n_heads):
    tq = _tile(t, 256)
    d_att = n_heads * HEAD_DIM
    u = _suffix_matrix(tq)
    vmem = 2 * (3 * t * LANES * 2 + t * LANES * 4 + tq * (tq + LANES) * 2 + t * LANES * 2)
    col = lambda c: pl.BlockSpec((t, LANES), lambda b, h: (b, c * n_heads + h))
    return pl.pallas_call(
        functools.partial(_attn_prompt_kernel, tq=tq),
        grid=(bp, n_heads),
        in_specs=[col(0), col(1), col(2), col(3), pl.BlockSpec((tq, tq + LANES), lambda b, h: (0, 0))],
        out_specs=pl.BlockSpec((t, LANES), lambda b, h: (b, h)),
        out_shape=jax.ShapeDtypeStruct((bp * t, d_att), BF16),
        compiler_params=_params(("arbitrary", "arbitrary"), vmem),
        name="attn_prompt",
    )(qkv, qkv, qkv, proj, u)


def _attn_sample_kernel(q_ref, kn_ref, vn_ref, gt_ref, kc_hbm, vc_hbm, un_ref, uc_ref, o_ref,
                        kfirst, vfirst, kmore, vmore, acc_scr, car_scr, sem, *, layer, pc):
    b = pl.program_id(0)
    n_heads, ts = acc_scr.shape[0], acc_scr.shape[1]
    newest = kc_hbm.shape[2] // pc - 1
    slot = b % 2

    def copies(stream, chunk, kdst, vdst, ksem, vsem):
        pos = pl.ds(chunk * pc, pc)
        out = []
        for h in range(n_heads):
            out.append(pltpu.make_async_copy(kc_hbm.at[layer, stream, pos, h, :], kdst.at[h], ksem))
            out.append(pltpu.make_async_copy(vc_hbm.at[layer, stream, pos, h, :], vdst.at[h], vsem))
        return out

    def first_copies(stream, s):
        return copies(stream, newest, kfirst.at[s], vfirst.at[s], sem.at[s, 0], sem.at[s, 1])

    @pl.when(b == 0)
    def _():
        for c in first_copies(0, 0):
            c.start()

    @pl.when(b + 1 < pl.num_programs(0))
    def _():
        for c in first_copies(b + 1, 1 - slot):
            c.start()

    def head(ref, h):
        return ref[:, h * HEAD_DIM:(h + 1) * HEAD_DIM]

    un = un_ref[...]
    pad = jnp.zeros((LANES - ts, HEAD_DIM), BF16)
    mask = _strictly_earlier((ts, LANES))
    zero = jnp.zeros((ts, LANES), F32)
    for h in range(n_heads):
        kn = jnp.concatenate([head(kn_ref, h), pad], axis=0)
        vn = jnp.concatenate([head(vn_ref, h), pad], axis=0)
        acc_scr[h], car_scr[h] = _sb_tile(head(q_ref, h), kn, vn, un, zero, mask)

    uc = uc_ref[...]

    def chunk(kbuf, vbuf):
        worst = None
        for h in range(n_heads):
            pv, tot = _sb_tile(head(q_ref, h), kbuf[h].astype(BF16), vbuf[h].astype(BF16), uc, car_scr[h])
            acc_scr[h] += pv
            car_scr[h] += tot
            worst = car_scr[h] if worst is None else jnp.maximum(worst, car_scr[h])
        return jnp.max(worst)

    for c in first_copies(b, slot):
        c.wait()
    worst = chunk(kfirst.at[slot], vfirst.at[slot])

    def more(s):
        return jnp.logical_and(s[0] >= 0, s[1] >= EXP_UNDERFLOW)

    def older(s):
        cs = copies(b, s[0], kmore, vmore, sem.at[2, 0], sem.at[2, 1])
        for c in cs:
            c.start()
        for c in cs:
            c.wait()
        return s[0] - 1, chunk(kmore, vmore)

    lax.while_loop(more, older, (newest - 1, worst))

    for h in range(n_heads):
        g = head(gt_ref, h)
        o_ref[:, h * HEAD_DIM:(h + 1) * HEAD_DIM] = (acc_scr[h] * (g * _sigmoid(g))).astype(o_ref.dtype)


def _attn_sample(qkv, proj, cache_k, cache_v, *, layer, mp, bs, ts, n_heads):
    p = cache_k.shape[2]
    d_att = n_heads * HEAD_DIM
    pc = _tile(p, 256)
    r0 = mp // ts
    chunk_bytes = n_heads * pc * HEAD_DIM * 4
    vmem = (2 * (3 * ts * d_att * 2 + ts * d_att * 4 + ts * d_att * 2) + 6 * chunk_bytes
            + 2 * n_heads * ts * LANES * 4 + 2 * (LANES * 2 * LANES + pc * (pc + LANES)) * 2)
    new_spec = lambda c: pl.BlockSpec((ts, d_att), lambda b: (r0 + b, c))
    chunk_buf = lambda n: pltpu.VMEM(n + (n_heads, pc, HEAD_DIM), F32)
    head_state = pltpu.VMEM((n_heads, ts, LANES), F32)
    return pl.pallas_call(
        functools.partial(_attn_sample_kernel, layer=layer, pc=pc),
        grid=(bs,),
        in_specs=[new_spec(0), new_spec(1), new_spec(2), new_spec(3),
                  pl.BlockSpec(memory_space=pl.ANY), pl.BlockSpec(memory_space=pl.ANY),
                  pl.BlockSpec((LANES, 2 * LANES), lambda b: (0, 0)),
                  pl.BlockSpec((pc, pc + LANES), lambda b: (0, 0))],
        out_specs=pl.BlockSpec((ts, d_att), lambda b: (b, 0)),
        out_shape=jax.ShapeDtypeStruct((bs * ts, d_att), BF16),
        scratch_shapes=[chunk_buf((2,)), chunk_buf((2,)), chunk_buf(()), chunk_buf(()),
                        head_state, head_state, pltpu.SemaphoreType.DMA((3, 2))],
        compiler_params=_params(("arbitrary",), vmem),
        name="attn_sample",
    )(qkv, qkv, qkv, proj, cache_k, cache_v, _suffix_matrix(LANES), _suffix_matrix(pc))


@jax.jit
def _step(x_prompt, x_sample, cache_k, cache_v, state_h, state_conv, state_pool,
          norm_rec, w_in_rec, conv_w, conv_b, gate_r_w, gate_r_b, gate_i_w, gate_i_b, rg_lambda,
          pool_w, pool_scale, w_out_rec, norm_att, w_in_att, w_out_att, norm_final):
    bp, t, d = x_prompt.shape
    bs, ts, _ = x_sample.shape
    n_rec, n_att = norm_rec.shape[0], norm_att.shape[0]
    d_rnn = state_h.shape[-1]
    d_pool = state_pool.shape[-1]
    n_heads = cache_k.shape[3]
    d_att = n_heads * HEAD_DIM
    mp = bp * t
    assert ts >= POOL_BUF and ts % SUBLANES == 0 and t % ts == 0 and cache_k.shape[4] == HEAD_DIM
    assert gate_r_w.shape[2] == LANES and d_pool // len(POOL_WINDOWS) == 2 * LANES

    x = jnp.concatenate([x_prompt.reshape(mp, d), x_sample.reshape(bs * ts, d)], axis=0)
    sc_pad = jnp.pad(state_conv, ((0, 0), (0, 0), (SUBLANES - (CONV_W - 1), 0), (0, 0)))
    sp_pad = jnp.pad(state_pool, ((0, 0), (0, 0), (POOL_TAIL - POOL_BUF, 0), (0, 0)))

    outs = {k: [] for k in ("kp", "vp", "hp", "cp", "pp", "ks", "vs", "hs", "cs", "ps")}
    for layer in range(n_rec + n_att):
        j = layer // 2
        if layer % 2 == 0:
            (proj,) = _norm_matmul(x, norm_rec[j], w_in_rec[j].astype(BF16), (F32,))
            ya, hp, hs = _rglru(proj, state_h[j], sc_pad[j], conv_w[j], conv_b[j], gate_r_w[j], gate_r_b[j],
                                gate_i_w[j], gate_i_b[j], rg_lambda[j], bp=bp, t=t, bs=bs, ts=ts, d_rnn=d_rnn)
            yb = _pool(proj, sp_pad[j], pool_w[j], pool_scale[j], bp=bp, t=t, bs=bs, ts=ts,
                       d_rnn=d_rnn, d_pool=d_pool)
            w_out = w_out_rec[j].astype(BF16)
            x = _matmul_residual([ya, yb], [w_out[:d_rnn], w_out[d_rnn:]], x)
            pp3 = proj[:mp].reshape(bp, t, -1)
            ps3 = proj[mp:].reshape(bs, ts, -1)
            outs["hp"].append(hp)
            outs["hs"].append(hs)
            outs["cp"].append(pp3[:, t - (CONV_W - 1):, :d_rnn])
            outs["cs"].append(ps3[:, ts - (CONV_W - 1):, :d_rnn])
            outs["pp"].append(pp3[:, t - POOL_BUF:, 2 * d_rnn:2 * d_rnn + d_pool])
            outs["ps"].append(ps3[:, ts - POOL_BUF:, 2 * d_rnn:2 * d_rnn + d_pool])
        else:
            proj, qkv = _norm_matmul(x, norm_att[j], w_in_att[j].astype(BF16), (F32, BF16))
            op = _attn_prompt(qkv, proj, bp=bp, t=t, n_heads=n_heads)
            os_ = _attn_sample(qkv, proj, cache_k, cache_v, layer=j, mp=mp, bs=bs, ts=ts, n_heads=n_heads)
            x = _matmul_residual([jnp.concatenate([op, os_], axis=0)], [w_out_att[j].astype(BF16)], x)
            k_all = proj[:, d_att:2 * d_att]
            v_all = proj[:, 2 * d_att:3 * d_att]
            outs["kp"].append(k_all[:mp].reshape(bp, t, n_heads, HEAD_DIM))
            outs["vp"].append(v_all[:mp].reshape(bp, t, n_heads, HEAD_DIM))
            outs["ks"].append(k_all[mp:].reshape(bs, ts, n_heads, HEAD_DIM))
            outs["vs"].append(v_all[mp:].reshape(bs, ts, n_heads, HEAD_DIM))

    y = _final_norm(x, norm_final)
    st = {k: jnp.stack(v) for k, v in outs.items()}
    return (y[:mp].reshape(bp, t, d), y[mp:].reshape(bs, ts, d),
            st["kp"], st["vp"], st["hp"], st["cp"], st["pp"],
            st["ks"], st["vs"], st["hs"], st["cs"], st["ps"])


def kernel(x_prompt, x_sample, cache_k, cache_v, state_h, state_conv, state_pool, norm_rec, w_in_rec, conv_w, conv_b, gate_r_w, gate_r_b, gate_i_w, gate_i_b, rg_lambda, pool_w, pool_scale, w_out_rec, norm_att, w_in_att, w_out_att, norm_final):
    return _step(x_prompt, x_sample, cache_k, cache_v, state_h, state_conv, state_pool, norm_rec, w_in_rec,
                 conv_w, conv_b, gate_r_w, gate_r_b, gate_i_w, gate_i_b, rg_lambda, pool_w, pool_scale,
                 w_out_rec, norm_att, w_in_att, w_out_att, norm_final)
```

```python
import functools

import jax
import jax.numpy as jnp
from jax import lax
from jax.experimental import pallas as pl
from jax.experimental.pallas import tpu as pltpu

F32 = jnp.float32
BF16 = jnp.bfloat16

EPS = 1e-6
RG_C = 8.0
CONV_W = 4
POOL_WINDOWS = (2, 4, 8, 16)
POOL_BUF = max(POOL_WINDOWS) - 1
HEAD_DIM = 128
ATT_SCALE = HEAD_DIM ** -0.5
EXP_UNDERFLOW = -105.0

LANES = 128
SUBLANES = 8
VMEM_LIMIT_CAP = 60000 * 1024
VMEM_SLACK = 8 * 1024 * 1024


def _params(semantics, buffer_bytes):
    limit = min(VMEM_LIMIT_CAP, buffer_bytes + VMEM_SLACK)
    return pltpu.CompilerParams(dimension_semantics=semantics, vmem_limit_bytes=limit)


def _tile(n, pref):
    t = min(n, pref)
    while n % t:
        t //= 2
    return t


def _sigmoid(x):
    return 1.0 / (1.0 + jnp.exp(-x))


LOG2_E = 1.4426950408889634


def _softplus(x):
    return jnp.maximum(x, 0.0) + jnp.log(1.0 + jnp.exp2(jnp.abs(x) * -LOG2_E))


NORM_ROWS = 128


def _norm_mm_kernel(x_ref, g_ref, w_ref, *refs, n_out):
    out_refs, xn_ref = refs[:n_out], refs[n_out]
    tm = x_ref.shape[0]

    @pl.when(pl.program_id(1) == 0)
    def _():
        g = g_ref[...]

        def body(c, _):
            rows = pl.ds(pl.multiple_of(c * NORM_ROWS, NORM_ROWS), NORM_ROWS)
            x = x_ref[rows, :]
            ms = jnp.mean(x * x, axis=-1, keepdims=True)
            xn_ref[rows, :] = (x * lax.rsqrt(ms + EPS) * g).astype(BF16)
            return 0

        lax.fori_loop(0, tm // NORM_ROWS, body, 0)

    acc = jnp.dot(xn_ref[...], w_ref[...], preferred_element_type=F32)
    for o in out_refs:
        o[...] = acc.astype(o.dtype)


def _norm_matmul(x, g, w, out_dtypes, *, tm_pref=1024, tn_pref=512):
    m, k = x.shape
    n = w.shape[1]
    tm, tn = _tile(m, tm_pref), _tile(n, tn_pref)
    out_bytes = sum(jnp.dtype(d).itemsize for d in out_dtypes)
    vmem = 2 * tm * k * 4 + tm * k * 2 + 2 * k * tn * 2 + 2 * tm * tn * out_bytes
    outs = pl.pallas_call(
        functools.partial(_norm_mm_kernel, n_out=len(out_dtypes)),
        grid=(m // tm, n // tn),
        in_specs=[pl.BlockSpec((tm, k), lambda i, j: (i, 0)),
                  pl.BlockSpec((1, k), lambda i, j: (0, 0)),
                  pl.BlockSpec((k, tn), lambda i, j: (0, j))],
        out_specs=[pl.BlockSpec((tm, tn), lambda i, j: (i, j)) for _ in out_dtypes],
        out_shape=[jax.ShapeDtypeStruct((m, n), d) for d in out_dtypes],
        scratch_shapes=[pltpu.VMEM((tm, k), BF16)],
        compiler_params=_params(("arbitrary", "arbitrary"), vmem),
        name="norm_matmul",
    )(x, g.reshape(1, k), w)
    return outs


def _mm_res_kernel(*refs, n_pairs):
    lhs, ws = refs[:n_pairs], refs[n_pairs:2 * n_pairs]
    res_ref, out_ref = refs[2 * n_pairs], refs[2 * n_pairs + 1]
    acc = res_ref[...]
    for l, w in zip(lhs, ws):
        acc = acc + jnp.dot(l[...], w[...], preferred_element_type=F32)
    out_ref[...] = acc


def _matmul_residual(lhs_list, w_list, res, *, tm_pref=512, tn_pref=1024):
    m, n = res.shape
    tm, tn = _tile(m, tm_pref), _tile(n, tn_pref)
    ks = [l.shape[1] for l in lhs_list]
    vmem = sum(2 * tm * k * 2 + 2 * k * tn * 2 for k in ks) + 4 * tm * tn * 4
    in_specs = ([pl.BlockSpec((tm, k), lambda i, j: (i, 0)) for k in ks]
                + [pl.BlockSpec((k, tn), lambda i, j: (0, j)) for k in ks]
                + [pl.BlockSpec((tm, tn), lambda i, j: (i, j))])
    return pl.pallas_call(
        functools.partial(_mm_res_kernel, n_pairs=len(ks)),
        grid=(m // tm, n // tn),
        in_specs=in_specs,
        out_specs=pl.BlockSpec((tm, tn), lambda i, j: (i, j)),
        out_shape=jax.ShapeDtypeStruct((m, n), F32),
        compiler_params=_params(("arbitrary", "arbitrary"), vmem),
        name="matmul_residual",
    )(*lhs_list, *w_list, res)


def _final_norm_kernel(x_ref, g_ref, o_ref):
    x = x_ref[...]
    ms = jnp.mean(x * x, axis=-1, keepdims=True)
    o_ref[...] = x * lax.rsqrt(ms + EPS) * g_ref[...]


def _final_norm(x, g):
    m, d = x.shape
    tm = _tile(m, 256)
    return pl.pallas_call(
        _final_norm_kernel,
        grid=(m // tm,),
        in_specs=[pl.BlockSpec((tm, d), lambda i: (i, 0)), pl.BlockSpec((1, d), lambda i: (0, 0))],
        out_specs=pl.BlockSpec((tm, d), lambda i: (i, 0)),
        out_shape=jax.ShapeDtypeStruct((m, d), F32),
        compiler_params=_params(("arbitrary",), 4 * tm * d * 4),
        name="final_norm",
    )(x, g.reshape(1, d))


SCAN_ROWS = 256
SCAN_STREAMS = 4


def _scan_tiles(a, u, carry):
    r = a.shape[0]
    nt = r // SUBLANES
    a3 = a.reshape(nt, SUBLANES, LANES)
    u3 = u.reshape(nt, SUBLANES, LANES)
    sub = lax.broadcasted_iota(jnp.int32, a3.shape, 1)
    for d in (1, 2, 4):
        keep = sub >= d
        a_prev = pltpu.roll(a3, d, 1)
        u_prev = pltpu.roll(u3, d, 1)
        u3 = jnp.where(keep, a3 * u_prev + u3, u3)
        a3 = jnp.where(keep, a3 * a_prev, a3)
    hs = []
    for t in range(nt):
        h_t = u3[t] + a3[t] * carry
        carry = h_t[SUBLANES - 1:SUBLANES, :]
        hs.append(h_t)
    return jnp.concatenate(hs, axis=0), carry


def _rglru_kernel(xa_ref, ga_ref, h0_ref, sc_ref, cw_ref, cb_ref, wr_ref, br_ref, wi_ref, bi_ref, lam_ref,
                  out_ref, hp_ref, hs_ref, *, bp, t, bs, ts):
    mp = bp * t
    cw = cw_ref[...]
    cb = cb_ref[...]
    wr = wr_ref[...].astype(BF16)
    wi = wi_ref[...].astype(BF16)
    br = br_ref[...]
    bi = bi_ref[...]
    neg_c_sp = -RG_C * _softplus(-lam_ref[...])

    def conv(ext, n):
        def tap(k):
            back = CONV_W - 1 - k
            rows = ext if back == 0 else pltpu.roll(ext, back, 0)
            return rows[SUBLANES:] * cw[k:k + 1]

        y = cb + tap(0)
        for k in range(1, CONV_W):
            y = y + tap(k)
        return y

    def decay_and_input(xc):
        xb = xc.astype(BF16)
        r = _sigmoid(jnp.dot(xb, wr, preferred_element_type=F32) + br)
        i = _sigmoid(jnp.dot(xb, wi, preferred_element_type=F32) + bi)
        log_a = r * neg_c_sp
        a = jnp.exp(log_a)
        v = -jnp.tanh(log_a) * (1.0 + a * a)
        root = jnp.where(v > 0.0, v * lax.rsqrt(v), 0.0)
        return a, root * (i * xc)

    def emit(rows, a, u, carry):
        h, carry = _scan_tiles(a, u, carry)
        g = ga_ref[rows, :]
        out_ref[rows, :] = (h * (g * _sigmoid(g))).astype(out_ref.dtype)
        return carry

    rp = _tile(t, SCAN_ROWS)

    def prompt_chunk(c, state):
        new = []
        for b in range(bp):
            tail, carry = state[b]
            rows = pl.ds(pl.multiple_of(b * t + c * rp, SUBLANES), rp)
            x = xa_ref[rows, :]
            a, u = decay_and_input(conv(jnp.concatenate([tail, x], axis=0), rp))
            new.append((x[rp - SUBLANES:, :], emit(rows, a, u, carry)))
        return tuple(new)

    start = (jnp.zeros((SUBLANES, LANES), F32), jnp.zeros((1, LANES), F32))
    state = lax.fori_loop(0, t // rp, prompt_chunk, (start,) * bp)
    for b in range(bp):
        hp_ref[b:b + 1, :] = state[b][1]

    ns = _tile(bs, SCAN_STREAMS)

    def sample_group(c, _):
        rows = [pl.ds(pl.multiple_of(mp + (c * ns + k) * ts, SUBLANES), ts) for k in range(ns)]
        xc = [conv(jnp.concatenate([sc_ref[c * ns + k], xa_ref[rows[k], :]], axis=0), ts) for k in range(ns)]
        a, u = decay_and_input(jnp.concatenate(xc, axis=0))
        for k in range(ns):
            seg = slice(k * ts, (k + 1) * ts)
            s = c * ns + k
            hs_ref[pl.ds(s, 1), :] = emit(rows[k], a[seg], u[seg], h0_ref[pl.ds(s, 1), :])
        return 0

    lax.fori_loop(0, bs // ns, sample_group, 0)


def _rglru(proj, h0, sc_pad, cw, cb, wr, br, wi, bi, lam, *, bp, t, bs, ts, d_rnn):
    m = proj.shape[0]
    nb = d_rnn // LANES
    row = lambda v: v.reshape(1, d_rnn)
    vec_spec = pl.BlockSpec((1, LANES), lambda n: (0, n))
    w_spec = pl.BlockSpec((None, LANES, LANES), lambda n: (n, 0, 0))
    vmem = 2 * (2 * m * LANES * 4 + m * LANES * 2)
    return pl.pallas_call(
        functools.partial(_rglru_kernel, bp=bp, t=t, bs=bs, ts=ts),
        grid=(nb,),
        in_specs=[pl.BlockSpec((m, LANES), lambda n: (0, n)),
                  pl.BlockSpec((m, LANES), lambda n: (0, nb + n)),
                  pl.BlockSpec((bs, LANES), lambda n: (0, n)),
                  pl.BlockSpec((bs, SUBLANES, LANES), lambda n: (0, 0, n)),
                  pl.BlockSpec((CONV_W, LANES), lambda n: (0, n)),
                  vec_spec, w_spec, vec_spec, w_spec, vec_spec, vec_spec],
        out_specs=[pl.BlockSpec((m, LANES), lambda n: (0, n)),
                   pl.BlockSpec((bp, LANES), lambda n: (0, n)),
                   pl.BlockSpec((bs, LANES), lambda n: (0, n))],
        out_shape=[jax.ShapeDtypeStruct((m, d_rnn), BF16),
                   jax.ShapeDtypeStruct((bp, d_rnn), F32),
                   jax.ShapeDtypeStruct((bs, d_rnn), F32)],
        compiler_params=_params(("arbitrary",), vmem),
        name="rglru_mixer",
    )(proj, proj, h0, sc_pad, cw, row(cb), wr, row(br), wi, row(bi), row(lam))


POOL_TAIL = 16
POOL_ROWS = 128


def _pool_kernel(xb_ref, gb_ref, sp_ref, pw_ref, ps_ref, out_ref, m_scr, tail_scr, *, npb, bpt, ts):
    g = pl.program_id(0)
    rb = pl.program_id(1)
    rbk, gw = xb_ref.shape
    wf = lax.shift_left(jnp.int32(2), g).astype(F32)

    def window_means(ext, n, pos0):
        s2 = ext[1:] + ext[:-1]
        s4 = s2[2:] + s2[:-2]
        s8 = s4[4:] + s4[:-4]
        s16 = s8[8:] + s8[:-8]
        x = ext[POOL_TAIL:]
        win = jnp.where(g == 0, s2[POOL_TAIL - 1:],
                        jnp.where(g == 1, s4[POOL_TAIL - 3:],
                                  jnp.where(g == 2, s8[POOL_TAIL - 7:], s16[POOL_TAIL - 15:])))
        if pos0 is None:
            cnt = wf
        else:
            pos = pos0 + lax.broadcasted_iota(jnp.int32, (n, gw), 0)
            cnt = jnp.minimum(wf, (pos + 1).astype(F32))
        return win / cnt - x

    @pl.when(rb < npb)
    def _():
        blk = rb % bpt

        @pl.when(blk == 0)
        def _():
            tail_scr[...] = jnp.zeros_like(tail_scr)

        rc = _tile(rbk, POOL_ROWS)

        def body(c, tail):
            rows = pl.ds(pl.multiple_of(c * rc, SUBLANES), rc)
            x = xb_ref[rows, :]
            ext = jnp.concatenate([tail, x], axis=0)
            m_scr[rows, :] = window_means(ext, rc, blk * rbk + c * rc).astype(BF16)
            return x[rc - POOL_TAIL:, :]

        tail_scr[...] = lax.fori_loop(0, rbk // rc, body, tail_scr[...])

    @pl.when(rb >= npb)
    def _():
        s0 = (rb - npb) * (rbk // ts)

        def body(s, _):
            rows = pl.ds(pl.multiple_of(s * ts, SUBLANES), ts)
            ext = jnp.concatenate([sp_ref[s0 + s], xb_ref[rows, :]], axis=0)
            m_scr[rows, :] = window_means(ext, ts, None).astype(BF16)
            return 0

        lax.fori_loop(0, rbk // ts, body, 0)

    y = jnp.dot(m_scr[...], pw_ref[...].astype(BF16), preferred_element_type=F32) * ps_ref[...]
    gt = gb_ref[...]
    out_ref[...] = (y * (gt * _sigmoid(gt))).astype(out_ref.dtype)


def _pool(proj, sp_pad, pw, ps, *, bp, t, bs, ts, d_rnn, d_pool):
    m = proj.shape[0]
    ng = len(POOL_WINDOWS)
    gw = d_pool // ng
    ms = bs * ts
    rbk = min(1024, t, ms)
    while t % rbk or ms % rbk:
        rbk //= 2
    npb = bp * t // rbk
    xcol = 2 * d_rnn // gw
    gcol = (2 * d_rnn + d_pool) // gw
    vmem = 2 * (2 * rbk * gw * 4 + rbk * gw * 2) + rbk * gw * 2 + 2 * bs * POOL_TAIL * gw * 4 + 2 * gw * gw * 4
    return pl.pallas_call(
        functools.partial(_pool_kernel, npb=npb, bpt=t // rbk, ts=ts),
        grid=(ng, m // rbk),
        in_specs=[pl.BlockSpec((rbk, gw), lambda g, r: (r, xcol + g)),
                  pl.BlockSpec((rbk, gw), lambda g, r: (r, gcol + g)),
                  pl.BlockSpec((bs, POOL_TAIL, gw), lambda g, r: (0, 0, g)),
                  pl.BlockSpec((None, gw, gw), lambda g, r: (g, 0, 0)),
                  pl.BlockSpec((1, gw), lambda g, r: (0, g))],
        out_specs=pl.BlockSpec((rbk, gw), lambda g, r: (r, g)),
        out_shape=jax.ShapeDtypeStruct((m, d_pool), BF16),
        scratch_shapes=[pltpu.VMEM((rbk, gw), BF16), pltpu.VMEM((POOL_TAIL, gw), F32)],
        compiler_params=_params(("arbitrary", "arbitrary"), vmem),
        name="pool_mixer",
    )(proj, proj, sp_pad, pw, ps.reshape(1, d_pool))


def _suffix_matrix(n):
    j = lax.broadcasted_iota(jnp.int32, (n, n), 0)
    s = lax.broadcasted_iota(jnp.int32, (n, n), 1)
    u = jnp.where(j > s, -1.0, 0.0).astype(BF16)
    return jnp.concatenate([u, u], axis=0)


def _qk(q, k):
    return lax.dot_general(q, k, (((1,), (1,)), ((), ())), preferred_element_type=F32) * ATT_SCALE


def _strictly_earlier(shape):
    return lax.broadcasted_iota(jnp.int32, shape, 1) < lax.broadcasted_iota(jnp.int32, shape, 0)


def _sb_weights(z, uu, carry=None, mask=None):
    sp = _softplus(z)
    spm = sp if mask is None else jnp.where(mask, sp, 0.0)
    hi = spm.astype(BF16)
    lo = (spm - hi.astype(F32)).astype(BF16)
    e = z - sp + jnp.dot(jnp.concatenate([hi, lo], axis=1), uu, preferred_element_type=F32)
    w = jnp.exp(e if carry is None else e + carry)
    if mask is not None:
        w = jnp.where(mask, w, 0.0)
    return w.astype(BF16), -jnp.sum(spm, axis=1, keepdims=True)


def _sb_tile(q, k, v, uu, carry=None, mask=None):
    w, tot = _sb_weights(_qk(q, k), uu, carry, mask)
    return jnp.dot(w, v, preferred_element_type=F32), tot


def _gated(acc, g):
    return acc * (g * _sigmoid(g))


def _attn_prompt_kernel(q_ref, k_ref, v_ref, gt_ref, uu_ref, o_ref, acc_scr, car_scr, worst_ref, *, tq):
    uu = uu_ref[...]
    mask = _strictly_earlier((tq, tq))
    nq = q_ref.shape[0] // tq

    for qi in range(nq):
        rows = pl.ds(qi * tq, tq)
        q = q_ref[rows, :]
        acc, carry = _sb_tile(q, k_ref[rows, :], v_ref[rows, :], uu, None, mask)
        if qi > 0:
            prev = pl.ds((qi - 1) * tq, tq)
            pv, tot = _sb_tile(q, k_ref[prev, :], v_ref[prev, :], uu, carry)
            acc, carry = acc + pv, carry + tot
        o_ref[rows, :] = _gated(acc, gt_ref[rows, :]).astype(o_ref.dtype)
        if qi > 1:
            acc_scr[qi] = acc
            car_scr[qi] = carry
            worst_ref[qi] = jnp.max(carry)

    def finish(qi, _):
        @pl.when(worst_ref[qi] >= EXP_UNDERFLOW)
        def _():
            rows = pl.ds(pl.multiple_of(qi * tq, tq), tq)
            q = q_ref[rows, :]

            def more(s):
                return jnp.logical_and(s[0] >= 0, jnp.max(s[2]) >= EXP_UNDERFLOW)

            def older(s):
                j, acc, carry = s
                old = pl.ds(pl.multiple_of(j * tq, tq), tq)
                pv, tot = _sb_tile(q, k_ref[old, :], v_ref[old, :], uu, carry)
                return j - 1, acc + pv, carry + tot

            _, acc, _ = lax.while_loop(more, older, (qi - 2, acc_scr[qi], car_scr[qi]))
            o_ref[rows, :] = _gated(acc, gt_ref[rows, :]).astype(o_ref.dtype)

        return 0

    lax.fori_loop(2, nq, finish, 0)


def _attn_prompt(qkv, proj, *, bp, t, n_heads):
    tq = _tile(t, 256)
    d_att = n_heads * HEAD_DIM
    nq = t // tq
    state = pltpu.VMEM((nq, tq, LANES), F32), pltpu.VMEM((nq, tq, 1), F32), pltpu.SMEM((nq,), F32)
    vmem = 2 * (3 * t * LANES * 2 + t * LANES * 4 + 2 * tq * tq * 2 + t * LANES * 2) + 2 * t * LANES * 4
    col = lambda c: pl.BlockSpec((t, LANES), lambda b, h: (b, c * n_heads + h))
    return pl.pallas_call(
        functools.partial(_attn_prompt_kernel, tq=tq),
        grid=(bp, n_heads),
        in_specs=[col(0), col(1), col(2), col(3), pl.BlockSpec((2 * tq, tq), lambda b, h: (0, 0))],
        out_specs=pl.BlockSpec((t, LANES), lambda b, h: (b, h)),
        out_shape=jax.ShapeDtypeStruct((bp * t, d_att), BF16),
        scratch_shapes=list(state),
        compiler_params=_params(("arbitrary", "arbitrary"), vmem),
        name="attn_prompt",
    )(qkv, qkv, qkv, proj, _suffix_matrix(tq))


def _attn_sample_kernel(q_ref, kn_ref, vn_ref, gt_ref, kc_hbm, vc_hbm, un_ref, uc_ref, o_ref,
                        kfirst, vfirst, kmore, vmore, acc_scr, car_scr, sem, *, layer, pc):
    b = pl.program_id(0)
    n_heads, ts = kmore.shape[0], q_ref.shape[0]
    newest = kc_hbm.shape[2] // pc - 1
    slot = b % 2

    def copies(stream, chunk, kdst, vdst, ksem, vsem):
        pos = pl.ds(chunk * pc, pc)
        out = []
        for h in range(n_heads):
            out.append(pltpu.make_async_copy(kc_hbm.at[layer, stream, pos, h, :], kdst.at[h], ksem))
            out.append(pltpu.make_async_copy(vc_hbm.at[layer, stream, pos, h, :], vdst.at[h], vsem))
        return out

    def first_copies(stream, s):
        return copies(stream, newest, kfirst.at[s], vfirst.at[s], sem.at[s, 0], sem.at[s, 1])

    @pl.when(b == 0)
    def _():
        for c in first_copies(0, 0):
            c.start()

    @pl.when(b + 1 < pl.num_programs(0))
    def _():
        for c in first_copies(b + 1, 1 - slot):
            c.start()

    def head(ref, h):
        return ref[:, h * HEAD_DIM:(h + 1) * HEAD_DIM]

    def scores(keys):
        return jnp.concatenate([_qk(head(q_ref, h), keys(h)) for h in range(n_heads)], axis=0)

    def weighted(w, values):
        return jnp.concatenate([jnp.dot(w[h * ts:(h + 1) * ts], values(h), preferred_element_type=F32)
                                for h in range(n_heads)], axis=0)

    pad = jnp.zeros((LANES - ts, HEAD_DIM), BF16)
    query = lax.broadcasted_iota(jnp.int32, (n_heads, ts, LANES), 1).reshape(n_heads * ts, LANES)
    mask = lax.broadcasted_iota(jnp.int32, (n_heads * ts, LANES), 1) < query
    w, carry = _sb_weights(scores(lambda h: jnp.concatenate([head(kn_ref, h), pad], axis=0)),
                           un_ref[...], None, mask)
    acc = weighted(w, lambda h: jnp.concatenate([head(vn_ref, h), pad], axis=0))

    def chunk(kbuf, vbuf, acc, carry):
        w, tot = _sb_weights(scores(lambda h: kbuf[h].astype(BF16)), uc_ref[...], carry)
        return acc + weighted(w, lambda h: vbuf[h].astype(BF16)), carry + tot

    for c in first_copies(b, slot):
        c.wait()
    acc_scr[...], car_scr[...] = chunk(kfirst.at[slot], vfirst.at[slot], acc, carry)

    def more(s):
        return jnp.logical_and(s[0] >= 0, s[1] >= EXP_UNDERFLOW)

    def older(s):
        cs = copies(b, s[0], kmore, vmore, sem.at[2, 0], sem.at[2, 1])
        for c in cs:
            c.start()
        for c in cs:
            c.wait()
        acc_scr[...], car_scr[...] = chunk(kmore, vmore, acc_scr[...], car_scr[...])
        return s[0] - 1, jnp.max(car_scr[...])

    lax.while_loop(more, older, (newest - 1, jnp.max(car_scr[...])))

    for h in range(n_heads):
        o_ref[:, h * HEAD_DIM:(h + 1) * HEAD_DIM] = _gated(acc_scr[pl.ds(h * ts, ts), :],
                                                           head(gt_ref, h)).astype(o_ref.dtype)


def _attn_sample(qkv, proj, cache_k, cache_v, *, layer, mp, bs, ts, n_heads):
    p = cache_k.shape[2]
    d_att = n_heads * HEAD_DIM
    pc = _tile(p, 256)
    r0 = mp // ts
    chunk_bytes = n_heads * pc * HEAD_DIM * 4
    vmem = (2 * (3 * ts * d_att * 2 + ts * d_att * 4 + ts * d_att * 2) + 6 * chunk_bytes
            + 2 * n_heads * ts * LANES * 4 + 2 * 2 * (LANES * LANES + pc * pc) * 2)
    new_spec = lambda c: pl.BlockSpec((ts, d_att), lambda b: (r0 + b, c))
    chunk_buf = lambda n: pltpu.VMEM(n + (n_heads, pc, HEAD_DIM), F32)
    return pl.pallas_call(
        functools.partial(_attn_sample_kernel, layer=layer, pc=pc),
        grid=(bs,),
        in_specs=[new_spec(0), new_spec(1), new_spec(2), new_spec(3),
                  pl.BlockSpec(memory_space=pl.ANY), pl.BlockSpec(memory_space=pl.ANY),
                  pl.BlockSpec((2 * LANES, LANES), lambda b: (0, 0)),
                  pl.BlockSpec((2 * pc, pc), lambda b: (0, 0))],
        out_specs=pl.BlockSpec((ts, d_att), lambda b: (b, 0)),
        out_shape=jax.ShapeDtypeStruct((bs * ts, d_att), BF16),
        scratch_shapes=[chunk_buf((2,)), chunk_buf((2,)), chunk_buf(()), chunk_buf(()),
                        pltpu.VMEM((n_heads * ts, LANES), F32), pltpu.VMEM((n_heads * ts, 1), F32),
                        pltpu.SemaphoreType.DMA((3, 2))],
        compiler_params=_params(("arbitrary",), vmem),
        name="attn_sample",
    )(qkv, qkv, qkv, proj, cache_k, cache_v, _suffix_matrix(LANES), _suffix_matrix(pc))


KV_ROWS = 256


def _kv_layout_kernel(*refs, n_layers, n_heads, npt):
    srcs, out_p, out_s = refs[:n_layers], refs[n_layers], refs[n_layers + 1]
    layer, i = pl.program_id(0), pl.program_id(1)
    tm = srcs[0].shape[0]

    def put(src, dst):
        for h in range(n_heads):
            dst[pl.ds(h, tm, stride=n_heads), :] = src[:, h * HEAD_DIM:(h + 1) * HEAD_DIM]

    for l in range(n_layers):
        pl.when(jnp.logical_and(layer == l, i < npt))(functools.partial(put, srcs[l], out_p))
        pl.when(jnp.logical_and(layer == l, i >= npt))(functools.partial(put, srcs[l], out_s))


def _kv_layout(projs, col, *, mp, n_heads):
    m = projs[0].shape[0]
    n_layers, d_att, ms = len(projs), n_heads * HEAD_DIM, m - mp
    tm = _tile(mp, KV_ROWS)
    while ms % tm:
        tm //= 2
    npt, nst = mp // tm, ms // tm

    def src_spec(l):
        return pl.BlockSpec((tm, d_att), lambda cur, i: (jnp.where(cur < l, 0, jnp.where(cur > l, npt + nst - 1, i)), col))

    blk = (tm * n_heads, HEAD_DIM)
    vmem = 2 * (n_layers * tm * d_att * 4 + 2 * tm * d_att * 4)
    return pl.pallas_call(
        functools.partial(_kv_layout_kernel, n_layers=n_layers, n_heads=n_heads, npt=npt),
        grid=(n_layers, npt + nst),
        in_specs=[src_spec(l) for l in range(n_layers)],
        out_specs=[pl.BlockSpec(blk, lambda cur, i: (cur * npt + jnp.minimum(i, npt - 1), 0)),
                   pl.BlockSpec(blk, lambda cur, i: (cur * nst + jnp.maximum(i - npt, 0), 0))],
        out_shape=[jax.ShapeDtypeStruct((n_layers * mp * n_heads, HEAD_DIM), F32),
                   jax.ShapeDtypeStruct((n_layers * ms * n_heads, HEAD_DIM), F32)],
        compiler_params=_params(("arbitrary", "arbitrary"), vmem),
        name="kv_layout",
    )(*projs)


@jax.jit
def _step(x_prompt, x_sample, cache_k, cache_v, state_h, state_conv, state_pool,
          norm_rec, w_in_rec, conv_w, conv_b, gate_r_w, gate_r_b, gate_i_w, gate_i_b, rg_lambda,
          pool_w, pool_scale, w_out_rec, norm_att, w_in_att, w_out_att, norm_final):
    bp, t, d = x_prompt.shape
    bs, ts, _ = x_sample.shape
    n_rec, n_att = norm_rec.shape[0], norm_att.shape[0]
    d_rnn = state_h.shape[-1]
    d_pool = state_pool.shape[-1]
    n_heads = cache_k.shape[3]
    d_att = n_heads * HEAD_DIM
    mp = bp * t
    assert ts >= POOL_BUF and ts % SUBLANES == 0 and t % ts == 0 and cache_k.shape[4] == HEAD_DIM
    assert gate_r_w.shape[2] == LANES and d_pool // len(POOL_WINDOWS) == 2 * LANES

    x = jnp.concatenate([x_prompt.reshape(mp, d), x_sample.reshape(bs * ts, d)], axis=0)
    sc_pad = jnp.pad(state_conv, ((0, 0), (0, 0), (SUBLANES - (CONV_W - 1), 0), (0, 0)))
    sp_pad = jnp.pad(state_pool, ((0, 0), (0, 0), (POOL_TAIL - POOL_BUF, 0), (0, 0)))

    outs = {k: [] for k in ("hp", "cp", "pp", "hs", "cs", "ps")}
    att_projs = []
    for layer in range(n_rec + n_att):
        j = layer // 2
        if layer % 2 == 0:
            (proj,) = _norm_matmul(x, norm_rec[j], w_in_rec[j].astype(BF16), (F32,))
            ya, hp, hs = _rglru(proj, state_h[j], sc_pad[j], conv_w[j], conv_b[j], gate_r_w[j], gate_r_b[j],
                                gate_i_w[j], gate_i_b[j], rg_lambda[j], bp=bp, t=t, bs=bs, ts=ts, d_rnn=d_rnn)
            yb = _pool(proj, sp_pad[j], pool_w[j], pool_scale[j], bp=bp, t=t, bs=bs, ts=ts,
                       d_rnn=d_rnn, d_pool=d_pool)
            w_out = w_out_rec[j].astype(BF16)
            x = _matmul_residual([ya, yb], [w_out[:d_rnn], w_out[d_rnn:]], x)
            pp3 = proj[:mp].reshape(bp, t, -1)
            ps3 = proj[mp:].reshape(bs, ts, -1)
            outs["hp"].append(hp)
            outs["hs"].append(hs)
            outs["cp"].append(pp3[:, t - (CONV_W - 1):, :d_rnn])
            outs["cs"].append(ps3[:, ts - (CONV_W - 1):, :d_rnn])
            outs["pp"].append(pp3[:, t - POOL_BUF:, 2 * d_rnn:2 * d_rnn + d_pool])
            outs["ps"].append(ps3[:, ts - POOL_BUF:, 2 * d_rnn:2 * d_rnn + d_pool])
        else:
            proj, qkv = _norm_matmul(x, norm_att[j], w_in_att[j].astype(BF16), (F32, BF16))
            op = _attn_prompt(qkv, proj, bp=bp, t=t, n_heads=n_heads)
            os_ = _attn_sample(qkv, proj, cache_k, cache_v, layer=j, mp=mp, bs=bs, ts=ts, n_heads=n_heads)
            x = _matmul_residual([jnp.concatenate([op, os_], axis=0)], [w_out_att[j].astype(BF16)], x)
            att_projs.append(proj)

    y = _final_norm(x, norm_final)
    st = {k: jnp.stack(v) for k, v in outs.items()}
    kp, ks = _kv_layout(att_projs, 1, mp=mp, n_heads=n_heads)
    vp, vs = _kv_layout(att_projs, 2, mp=mp, n_heads=n_heads)
    prompt_shape = (n_att, bp, t, n_heads, HEAD_DIM)
    sample_shape = (n_att, bs, ts, n_heads, HEAD_DIM)
    return (y[:mp].reshape(bp, t, d), y[mp:].reshape(bs, ts, d),
            kp.reshape(prompt_shape), vp.reshape(prompt_shape), st["hp"], st["cp"], st["pp"],
            ks.reshape(sample_shape), vs.reshape(sample_shape), st["hs"], st["cs"], st["ps"])


def kernel(x_prompt, x_sample, cache_k, cache_v, state_h, state_conv, state_pool, norm_rec, w_in_rec, conv_w, conv_b, gate_r_w, gate_r_b, gate_i_w, gate_i_b, rg_lambda, pool_w, pool_scale, w_out_rec, norm_att, w_in_att, w_out_att, norm_final):
    return _step(x_prompt, x_sample, cache_k, cache_v, state_h, state_conv, state_pool, norm_rec, w_in_rec,
                 conv_w, conv_b, gate_r_w, gate_r_b, gate_i_w, gate_i_b, rg_lambda, pool_w, pool_scale,
                 w_out_rec, norm_att, w_in_att, w_out_att, norm_final)
```

```python
import functools

import jax
import jax.numpy as jnp
from jax import lax
from jax.experimental import pallas as pl
from jax.experimental.pallas import tpu as pltpu

F32 = jnp.float32
BF16 = jnp.bfloat16

EPS = 1e-6
RG_C = 8.0
CONV_W = 4
POOL_WINDOWS = (2, 4, 8, 16)
POOL_BUF = max(POOL_WINDOWS) - 1
HEAD_DIM = 128
ATT_SCALE = HEAD_DIM ** -0.5
EXP_UNDERFLOW = -105.0

LANES = 128
SUBLANES = 8
VMEM_LIMIT_CAP = 60000 * 1024
VMEM_SLACK = 8 * 1024 * 1024


def _params(semantics, buffer_bytes):
    limit = min(VMEM_LIMIT_CAP, buffer_bytes + VMEM_SLACK)
    return pltpu.CompilerParams(dimension_semantics=semantics, vmem_limit_bytes=limit)


def _tile(n, pref):
    t = min(n, pref)
    while n % t:
        t //= 2
    return t


def _sigmoid(x):
    return 1.0 / (1.0 + jnp.exp(-x))


LOG2_E = 1.4426950408889634


def _softplus(x):
    return jnp.maximum(x, 0.0) + jnp.log(1.0 + jnp.exp2(jnp.abs(x) * -LOG2_E))


NORM_ROWS = 128


def _normalize_rows(x_ref, g_ref, xn_ref):
    @pl.when(pl.program_id(1) == 0)
    def _():
        g = g_ref[...]

        def body(c, _):
            rows = pl.ds(pl.multiple_of(c * NORM_ROWS, NORM_ROWS), NORM_ROWS)
            x = x_ref[rows, :]
            ms = jnp.mean(x * x, axis=-1, keepdims=True)
            xn_ref[rows, :] = (x * lax.rsqrt(ms + EPS) * g).astype(BF16)
            return 0

        lax.fori_loop(0, x_ref.shape[0] // NORM_ROWS, body, 0)


def _norm_mm_kernel(x_ref, g_ref, w_ref, o_ref, xn_ref):
    _normalize_rows(x_ref, g_ref, xn_ref)
    o_ref[...] = jnp.dot(xn_ref[...], w_ref[...], preferred_element_type=F32)


def _norm_matmul(x, g, w, *, tm_pref=1024, tn_pref=1024):
    m, k = x.shape
    n = w.shape[1]
    tm, tn = _tile(m, tm_pref), _tile(n, tn_pref)
    vmem = 2 * tm * k * 4 + tm * k * 2 + 2 * k * tn * 2 + 2 * tm * tn * 4
    return pl.pallas_call(
        _norm_mm_kernel,
        grid=(m // tm, n // tn),
        in_specs=[pl.BlockSpec((tm, k), lambda i, j: (i, 0)),
                  pl.BlockSpec((1, k), lambda i, j: (0, 0)),
                  pl.BlockSpec((k, tn), lambda i, j: (0, j))],
        out_specs=pl.BlockSpec((tm, tn), lambda i, j: (i, j)),
        out_shape=jax.ShapeDtypeStruct((m, n), F32),
        scratch_shapes=[pltpu.VMEM((tm, k), BF16)],
        compiler_params=_params(("arbitrary", "arbitrary"), vmem),
        name="norm_matmul",
    )(x, g.reshape(1, k), w)


def _norm_mm_att_kernel(x_ref, g_ref, w_ref, kp_in, vp_in, ks_in, vs_in,
                        qkv_ref, gt_ref, kp_hbm, vp_hbm, ks_hbm, vs_hbm, xn_ref, kv_scr, sem,
                        *, layer, n_top, per):
    del kp_in, vp_in, ks_in, vs_in
    _normalize_rows(x_ref, g_ref, xn_ref)
    i, j = pl.program_id(0), pl.program_id(1)
    tm, tn = qkv_ref.shape
    hpt = tn // HEAD_DIM
    top = i < n_top

    def project():
        return jnp.dot(xn_ref[...], w_ref[...], preferred_element_type=F32)

    def copies(jj, prompt_rows):
        slot = (jj - per) % 2
        dst = ((kp_hbm, ks_hbm), (vp_hbm, vs_hbm))[jj // per - 1][0 if prompt_rows else 1]
        rows = pl.ds(i * tm if prompt_rows else (i - n_top) * tm, tm)
        return [pltpu.make_async_copy(kv_scr.at[slot, :, pl.ds(h * HEAD_DIM, HEAD_DIM)],
                                      dst.at[layer, rows, (jj % per) * hpt + h, :], sem.at[slot])
                for h in range(hpt)]

    def for_rows(jj, action):
        @pl.when(top)
        def _():
            for c in copies(jj, True):
                action(c)

        @pl.when(jnp.logical_not(top))
        def _():
            for c in copies(jj, False):
                action(c)

    @pl.when(j < per)
    def _():
        qkv_ref[...] = project().astype(BF16)

    for jj in range(per, 3 * per + 2):
        @pl.when(j == jj)
        def _(jj=jj):
            if jj - 2 >= per:
                for_rows(jj - 2, lambda c: c.wait())
            if jj < 3 * per:
                slot = (jj - per) % 2
                kv_scr[slot] = project()
                qkv_ref[...] = kv_scr[slot].astype(BF16)
                for_rows(jj, lambda c: c.start())

    @pl.when(j >= 3 * per)
    def _():
        gt_ref[...] = project()


def _norm_matmul_att(x, g, w, new_kv, *, layer, mp, n_heads, tm_pref=1024):
    m, k = x.shape
    d_att = n_heads * HEAD_DIM
    tn = d_att // 2
    per = d_att // tn
    assert per >= 2 and w.shape[1] == 4 * d_att
    tm = _tile(mp, tm_pref)
    while (m - mp) % tm:
        tm //= 2
    any_spec = pl.BlockSpec(memory_space=pl.ANY)
    vmem = 2 * tm * k * 4 + tm * k * 2 + 2 * k * tn * 2 + 2 * tm * tn * (2 + 4) + 3 * tm * tn * 4
    outs = pl.pallas_call(
        functools.partial(_norm_mm_att_kernel, layer=layer, n_top=mp // tm, per=per),
        grid=(m // tm, 4 * per),
        in_specs=[pl.BlockSpec((tm, k), lambda i, j: (i, 0)),
                  pl.BlockSpec((1, k), lambda i, j: (0, 0)),
                  pl.BlockSpec((k, tn), lambda i, j: (0, j)),
                  any_spec, any_spec, any_spec, any_spec],
        out_specs=[pl.BlockSpec((tm, tn), lambda i, j: (i, jnp.minimum(j, 3 * per - 1))),
                   pl.BlockSpec((tm, tn), lambda i, j: (i, jnp.maximum(j - 3 * per, 0))),
                   any_spec, any_spec, any_spec, any_spec],
        out_shape=[jax.ShapeDtypeStruct((m, 3 * d_att), BF16), jax.ShapeDtypeStruct((m, d_att), F32)]
                  + [jax.ShapeDtypeStruct(a.shape, a.dtype) for a in new_kv],
        input_output_aliases={3: 2, 4: 3, 5: 4, 6: 5},
        scratch_shapes=[pltpu.VMEM((tm, k), BF16), pltpu.VMEM((2, tm, tn), F32),
                        pltpu.SemaphoreType.DMA((2,))],
        compiler_params=_params(("arbitrary", "arbitrary"), vmem),
        name="norm_matmul_att",
    )(x, g.reshape(1, k), w, *new_kv)
    return outs[0], outs[1], tuple(outs[2:])


def _mm_res_kernel(*refs, n_pairs):
    lhs, ws = refs[:n_pairs], refs[n_pairs:2 * n_pairs]
    res_ref, out_ref = refs[2 * n_pairs], refs[2 * n_pairs + 1]
    acc = res_ref[...]
    for l, w in zip(lhs, ws):
        acc = acc + jnp.dot(l[...], w[...], preferred_element_type=F32)
    out_ref[...] = acc


def _matmul_residual(lhs_list, w_list, res, *, tm_pref=1024, tn_pref=1024):
    m, n = res.shape
    tm, tn = _tile(m, tm_pref), _tile(n, tn_pref)
    ks = [l.shape[1] for l in lhs_list]
    vmem = sum(2 * tm * k * 2 + 2 * k * tn * 2 for k in ks) + 4 * tm * tn * 4
    in_specs = ([pl.BlockSpec((tm, k), lambda i, j: (i, 0)) for k in ks]
                + [pl.BlockSpec((k, tn), lambda i, j: (0, j)) for k in ks]
                + [pl.BlockSpec((tm, tn), lambda i, j: (i, j))])
    return pl.pallas_call(
        functools.partial(_mm_res_kernel, n_pairs=len(ks)),
        grid=(m // tm, n // tn),
        in_specs=in_specs,
        out_specs=pl.BlockSpec((tm, tn), lambda i, j: (i, j)),
        out_shape=jax.ShapeDtypeStruct((m, n), F32),
        compiler_params=_params(("arbitrary", "arbitrary"), vmem),
        name="matmul_residual",
    )(*lhs_list, *w_list, res)


def _mm_res_rows_kernel(top_ref, bottom_ref, w_ref, res_ref, out_ref, *, n_top):
    i = pl.program_id(0)

    def emit(lhs_ref):
        out_ref[...] = res_ref[...] + jnp.dot(lhs_ref[...], w_ref[...], preferred_element_type=F32)

    pl.when(i < n_top)(functools.partial(emit, top_ref))
    pl.when(i >= n_top)(functools.partial(emit, bottom_ref))


def _matmul_residual_rows(top, bottom, w, res, *, tm_pref=1024, tn_pref=1024):
    m, n = res.shape
    k = w.shape[0]
    tm = _tile(top.shape[0], tm_pref)
    while bottom.shape[0] % tm:
        tm //= 2
    tn = _tile(n, tn_pref)
    n_top = top.shape[0] // tm
    vmem = 4 * tm * k * 2 + 2 * k * tn * 2 + 4 * tm * tn * 4
    return pl.pallas_call(
        functools.partial(_mm_res_rows_kernel, n_top=n_top),
        grid=(m // tm, n // tn),
        in_specs=[pl.BlockSpec((tm, k), lambda i, j: (jnp.minimum(i, n_top - 1), 0)),
                  pl.BlockSpec((tm, k), lambda i, j: (jnp.maximum(i - n_top, 0), 0)),
                  pl.BlockSpec((k, tn), lambda i, j: (0, j)),
                  pl.BlockSpec((tm, tn), lambda i, j: (i, j))],
        out_specs=pl.BlockSpec((tm, tn), lambda i, j: (i, j)),
        out_shape=jax.ShapeDtypeStruct((m, n), F32),
        compiler_params=_params(("arbitrary", "arbitrary"), vmem),
        name="matmul_residual_rows",
    )(top, bottom, w, res)


def _final_norm_kernel(x_ref, g_ref, o_ref):
    x = x_ref[...]
    ms = jnp.mean(x * x, axis=-1, keepdims=True)
    o_ref[...] = x * lax.rsqrt(ms + EPS) * g_ref[...]


def _final_norm(x, g):
    m, d = x.shape
    tm = _tile(m, 256)
    return pl.pallas_call(
        _final_norm_kernel,
        grid=(m // tm,),
        in_specs=[pl.BlockSpec((tm, d), lambda i: (i, 0)), pl.BlockSpec((1, d), lambda i: (0, 0))],
        out_specs=pl.BlockSpec((tm, d), lambda i: (i, 0)),
        out_shape=jax.ShapeDtypeStruct((m, d), F32),
        compiler_params=_params(("arbitrary",), 4 * tm * d * 4),
        name="final_norm",
    )(x, g.reshape(1, d))


SCAN_ROWS = 256
SCAN_STREAMS = 4


def _scan_tiles(a, u, carry):
    r = a.shape[0]
    nt = r // SUBLANES
    a3 = a.reshape(nt, SUBLANES, LANES)
    u3 = u.reshape(nt, SUBLANES, LANES)
    sub = lax.broadcasted_iota(jnp.int32, a3.shape, 1)
    for d in (1, 2, 4):
        keep = sub >= d
        a_prev = pltpu.roll(a3, d, 1)
        u_prev = pltpu.roll(u3, d, 1)
        u3 = jnp.where(keep, a3 * u_prev + u3, u3)
        a3 = jnp.where(keep, a3 * a_prev, a3)
    hs = []
    for t in range(nt):
        h_t = u3[t] + a3[t] * carry
        carry = h_t[SUBLANES - 1:SUBLANES, :]
        hs.append(h_t)
    return jnp.concatenate(hs, axis=0), carry


def _rglru_kernel(xa_ref, ga_ref, h0_ref, sc_ref, cw_ref, cb_ref, wr_ref, br_ref, wi_ref, bi_ref, lam_ref,
                  out_ref, hp_ref, hs_ref, *, bp, t, bs, ts):
    mp = bp * t
    cw = cw_ref[...]
    cb = cb_ref[...]
    wr = wr_ref[...].astype(BF16)
    wi = wi_ref[...].astype(BF16)
    br = br_ref[...]
    bi = bi_ref[...]
    neg_c_sp = -RG_C * _softplus(-lam_ref[...])

    def conv(ext, n):
        def tap(k):
            back = CONV_W - 1 - k
            rows = ext if back == 0 else pltpu.roll(ext, back, 0)
            return rows[SUBLANES:] * cw[k:k + 1]

        y = cb + tap(0)
        for k in range(1, CONV_W):
            y = y + tap(k)
        return y

    def decay_and_input(xc):
        xb = xc.astype(BF16)
        r = _sigmoid(jnp.dot(xb, wr, preferred_element_type=F32) + br)
        i = _sigmoid(jnp.dot(xb, wi, preferred_element_type=F32) + bi)
        log_a = r * neg_c_sp
        a = jnp.exp(log_a)
        v = -jnp.tanh(log_a) * (1.0 + a * a)
        root = jnp.where(v > 0.0, v * lax.rsqrt(v), 0.0)
        return a, root * (i * xc)

    def emit(rows, a, u, carry):
        h, carry = _scan_tiles(a, u, carry)
        g = ga_ref[rows, :]
        out_ref[rows, :] = (h * (g * _sigmoid(g))).astype(out_ref.dtype)
        return carry

    rp = _tile(t, SCAN_ROWS)

    def prompt_chunk(c, state):
        new = []
        for b in range(bp):
            tail, carry = state[b]
            rows = pl.ds(pl.multiple_of(b * t + c * rp, SUBLANES), rp)
            x = xa_ref[rows, :]
            a, u = decay_and_input(conv(jnp.concatenate([tail, x], axis=0), rp))
            new.append((x[rp - SUBLANES:, :], emit(rows, a, u, carry)))
        return tuple(new)

    start = (jnp.zeros((SUBLANES, LANES), F32), jnp.zeros((1, LANES), F32))
    state = lax.fori_loop(0, t // rp, prompt_chunk, (start,) * bp)
    for b in range(bp):
        hp_ref[b:b + 1, :] = state[b][1]

    ns = _tile(bs, SCAN_STREAMS)

    def sample_group(c, _):
        rows = [pl.ds(pl.multiple_of(mp + (c * ns + k) * ts, SUBLANES), ts) for k in range(ns)]
        xc = [conv(jnp.concatenate([sc_ref[c * ns + k], xa_ref[rows[k], :]], axis=0), ts) for k in range(ns)]
        a, u = decay_and_input(jnp.concatenate(xc, axis=0))
        for k in range(ns):
            seg = slice(k * ts, (k + 1) * ts)
            s = c * ns + k
            hs_ref[pl.ds(s, 1), :] = emit(rows[k], a[seg], u[seg], h0_ref[pl.ds(s, 1), :])
        return 0

    lax.fori_loop(0, bs // ns, sample_group, 0)


def _rglru(proj, h0, sc_pad, cw, cb, wr, br, wi, bi, lam, *, bp, t, bs, ts, d_rnn):
    m = proj.shape[0]
    nb = d_rnn // LANES
    row = lambda v: v.reshape(1, d_rnn)
    vec_spec = pl.BlockSpec((1, LANES), lambda n: (0, n))
    w_spec = pl.BlockSpec((None, LANES, LANES), lambda n: (n, 0, 0))
    vmem = 2 * (2 * m * LANES * 4 + m * LANES * 2)
    return pl.pallas_call(
        functools.partial(_rglru_kernel, bp=bp, t=t, bs=bs, ts=ts),
        grid=(nb,),
        in_specs=[pl.BlockSpec((m, LANES), lambda n: (0, n)),
                  pl.BlockSpec((m, LANES), lambda n: (0, nb + n)),
                  pl.BlockSpec((bs, LANES), lambda n: (0, n)),
                  pl.BlockSpec((bs, SUBLANES, LANES), lambda n: (0, 0, n)),
                  pl.BlockSpec((CONV_W, LANES), lambda n: (0, n)),
                  vec_spec, w_spec, vec_spec, w_spec, vec_spec, vec_spec],
        out_specs=[pl.BlockSpec((m, LANES), lambda n: (0, n)),
                   pl.BlockSpec((bp, LANES), lambda n: (0, n)),
                   pl.BlockSpec((bs, LANES), lambda n: (0, n))],
        out_shape=[jax.ShapeDtypeStruct((m, d_rnn), BF16),
                   jax.ShapeDtypeStruct((bp, d_rnn), F32),
                   jax.ShapeDtypeStruct((bs, d_rnn), F32)],
        compiler_params=_params(("arbitrary",), vmem),
        name="rglru_mixer",
    )(proj, proj, h0, sc_pad, cw, row(cb), wr, row(br), wi, row(bi), row(lam))


POOL_TAIL = 16
POOL_ROWS = 128


def _pool_kernel(xb_ref, gb_ref, sp_ref, pw_ref, ps_ref, out_ref, m_scr, tail_scr, *, npb, bpt, ts):
    g = pl.program_id(0)
    rb = pl.program_id(1)
    rbk, gw = xb_ref.shape
    wf = lax.shift_left(jnp.int32(2), g).astype(F32)

    def window_means(ext, n, pos0):
        s2 = ext[1:] + ext[:-1]
        s4 = s2[2:] + s2[:-2]
        s8 = s4[4:] + s4[:-4]
        s16 = s8[8:] + s8[:-8]
        x = ext[POOL_TAIL:]
        win = jnp.where(g == 0, s2[POOL_TAIL - 1:],
                        jnp.where(g == 1, s4[POOL_TAIL - 3:],
                                  jnp.where(g == 2, s8[POOL_TAIL - 7:], s16[POOL_TAIL - 15:])))
        if pos0 is None:
            cnt = wf
        else:
            pos = pos0 + lax.broadcasted_iota(jnp.int32, (n, gw), 0)
            cnt = jnp.minimum(wf, (pos + 1).astype(F32))
        return win / cnt - x

    @pl.when(rb < npb)
    def _():
        blk = rb % bpt

        @pl.when(blk == 0)
        def _():
            tail_scr[...] = jnp.zeros_like(tail_scr)

        rc = _tile(rbk, POOL_ROWS)

        def body(c, tail):
            rows = pl.ds(pl.multiple_of(c * rc, SUBLANES), rc)
            x = xb_ref[rows, :]
            ext = jnp.concatenate([tail, x], axis=0)
            m_scr[rows, :] = window_means(ext, rc, blk * rbk + c * rc).astype(BF16)
            return x[rc - POOL_TAIL:, :]

        tail_scr[...] = lax.fori_loop(0, rbk // rc, body, tail_scr[...])

    @pl.when(rb >= npb)
    def _():
        s0 = (rb - npb) * (rbk // ts)

        def body(s, _):
            rows = pl.ds(pl.multiple_of(s * ts, SUBLANES), ts)
            ext = jnp.concatenate([sp_ref[s0 + s], xb_ref[rows, :]], axis=0)
            m_scr[rows, :] = window_means(ext, ts, None).astype(BF16)
            return 0

        lax.fori_loop(0, rbk // ts, body, 0)

    y = jnp.dot(m_scr[...], pw_ref[...].astype(BF16), preferred_element_type=F32) * ps_ref[...]
    gt = gb_ref[...]
    out_ref[...] = (y * (gt * _sigmoid(gt))).astype(out_ref.dtype)


def _pool(proj, sp_pad, pw, ps, *, bp, t, bs, ts, d_rnn, d_pool):
    m = proj.shape[0]
    ng = len(POOL_WINDOWS)
    gw = d_pool // ng
    ms = bs * ts
    rbk = min(1024, t, ms)
    while t % rbk or ms % rbk:
        rbk //= 2
    npb = bp * t // rbk
    xcol = 2 * d_rnn // gw
    gcol = (2 * d_rnn + d_pool) // gw
    vmem = 2 * (2 * rbk * gw * 4 + rbk * gw * 2) + rbk * gw * 2 + 2 * bs * POOL_TAIL * gw * 4 + 2 * gw * gw * 4
    return pl.pallas_call(
        functools.partial(_pool_kernel, npb=npb, bpt=t // rbk, ts=ts),
        grid=(ng, m // rbk),
        in_specs=[pl.BlockSpec((rbk, gw), lambda g, r: (r, xcol + g)),
                  pl.BlockSpec((rbk, gw), lambda g, r: (r, gcol + g)),
                  pl.BlockSpec((bs, POOL_TAIL, gw), lambda g, r: (0, 0, g)),
                  pl.BlockSpec((None, gw, gw), lambda g, r: (g, 0, 0)),
                  pl.BlockSpec((1, gw), lambda g, r: (0, g))],
        out_specs=pl.BlockSpec((rbk, gw), lambda g, r: (r, g)),
        out_shape=jax.ShapeDtypeStruct((m, d_pool), BF16),
        scratch_shapes=[pltpu.VMEM((rbk, gw), BF16), pltpu.VMEM((POOL_TAIL, gw), F32)],
        compiler_params=_params(("arbitrary", "arbitrary"), vmem),
        name="pool_mixer",
    )(proj, proj, sp_pad, pw, ps.reshape(1, d_pool))


def _suffix_matrix(n):
    j = lax.broadcasted_iota(jnp.int32, (n, n), 0)
    s = lax.broadcasted_iota(jnp.int32, (n, n), 1)
    u = jnp.where(j > s, -1.0, 0.0).astype(BF16)
    return jnp.concatenate([u, u], axis=0)


def _qk(q, k):
    return lax.dot_general(q, k, (((1,), (1,)), ((), ())), preferred_element_type=F32) * ATT_SCALE


def _strictly_earlier(shape):
    return lax.broadcasted_iota(jnp.int32, shape, 1) < lax.broadcasted_iota(jnp.int32, shape, 0)


def _sb_weights(z, uu, carry=None, mask=None):
    sp = _softplus(z)
    spm = sp if mask is None else jnp.where(mask, sp, 0.0)
    hi = spm.astype(BF16)
    lo = (spm - hi.astype(F32)).astype(BF16)
    e = z - sp + jnp.dot(jnp.concatenate([hi, lo], axis=1), uu, preferred_element_type=F32)
    w = jnp.exp(e if carry is None else e + carry)
    if mask is not None:
        w = jnp.where(mask, w, 0.0)
    return w.astype(BF16), -jnp.sum(spm, axis=1, keepdims=True)


def _sb_tile(q, k, v, uu, carry=None, mask=None):
    w, tot = _sb_weights(_qk(q, k), uu, carry, mask)
    return jnp.dot(w, v, preferred_element_type=F32), tot


def _gated(acc, g):
    return acc * (g * _sigmoid(g))


def _attn_prompt_kernel(q_ref, k_ref, v_ref, gt_ref, uu_ref, o_ref, acc_scr, car_scr, worst_ref, *, tq):
    uu = uu_ref[...]
    mask = _strictly_earlier((tq, tq))
    nq = q_ref.shape[0] // tq

    for qi in range(nq):
        rows = pl.ds(qi * tq, tq)
        q = q_ref[rows, :]
        acc, carry = _sb_tile(q, k_ref[rows, :], v_ref[rows, :], uu, None, mask)
        if qi > 0:
            prev = pl.ds((qi - 1) * tq, tq)
            pv, tot = _sb_tile(q, k_ref[prev, :], v_ref[prev, :], uu, carry)
            acc, carry = acc + pv, carry + tot
        o_ref[rows, :] = _gated(acc, gt_ref[rows, :]).astype(o_ref.dtype)
        if qi > 1:
            acc_scr[qi] = acc
            car_scr[qi] = carry
            worst_ref[qi] = jnp.max(carry)

    def finish(qi, _):
        @pl.when(worst_ref[qi] >= EXP_UNDERFLOW)
        def _():
            rows = pl.ds(pl.multiple_of(qi * tq, tq), tq)
            q = q_ref[rows, :]

            def more(s):
                return jnp.logical_and(s[0] >= 0, jnp.max(s[2]) >= EXP_UNDERFLOW)

            def older(s):
                j, acc, carry = s
                old = pl.ds(pl.multiple_of(j * tq, tq), tq)
                pv, tot = _sb_tile(q, k_ref[old, :], v_ref[old, :], uu, carry)
                return j - 1, acc + pv, carry + tot

            _, acc, _ = lax.while_loop(more, older, (qi - 2, acc_scr[qi], car_scr[qi]))
            o_ref[rows, :] = _gated(acc, gt_ref[rows, :]).astype(o_ref.dtype)

        return 0

    lax.fori_loop(2, nq, finish, 0)


def _attn_prompt(qkv, gate, *, bp, t, n_heads):
    tq = _tile(t, 256)
    d_att = n_heads * HEAD_DIM
    nq = t // tq
    state = pltpu.VMEM((nq, tq, LANES), F32), pltpu.VMEM((nq, tq, 1), F32), pltpu.SMEM((nq,), F32)
    vmem = 2 * (3 * t * LANES * 2 + t * LANES * 4 + 2 * tq * tq * 2 + t * LANES * 2) + 2 * t * LANES * 4
    col = lambda c: pl.BlockSpec((t, LANES), lambda b, h: (b, c * n_heads + h))
    return pl.pallas_call(
        functools.partial(_attn_prompt_kernel, tq=tq),
        grid=(bp, n_heads),
        in_specs=[col(0), col(1), col(2), col(0), pl.BlockSpec((2 * tq, tq), lambda b, h: (0, 0))],
        out_specs=pl.BlockSpec((t, LANES), lambda b, h: (b, h)),
        out_shape=jax.ShapeDtypeStruct((bp * t, d_att), BF16),
        scratch_shapes=list(state),
        compiler_params=_params(("arbitrary", "arbitrary"), vmem),
        name="attn_prompt",
    )(qkv, qkv, qkv, gate, _suffix_matrix(tq))


def _attn_sample_kernel(q_ref, kn_ref, vn_ref, gt_ref, kc_hbm, vc_hbm, un_ref, uc_ref, o_ref,
                        kfirst, vfirst, kmore, vmore, acc_scr, car_scr, sem, *, layer, pc):
    b = pl.program_id(0)
    n_heads, ts = kmore.shape[0], q_ref.shape[0]
    newest = kc_hbm.shape[2] // pc - 1
    slot = b % 2

    def copies(stream, chunk, kdst, vdst, ksem, vsem):
        pos = pl.ds(chunk * pc, pc)
        out = []
        for h in range(n_heads):
            out.append(pltpu.make_async_copy(kc_hbm.at[layer, stream, pos, h, :], kdst.at[h], ksem))
            out.append(pltpu.make_async_copy(vc_hbm.at[layer, stream, pos, h, :], vdst.at[h], vsem))
        return out

    def first_copies(stream, s):
        return copies(stream, newest, kfirst.at[s], vfirst.at[s], sem.at[s, 0], sem.at[s, 1])

    @pl.when(b == 0)
    def _():
        for c in first_copies(0, 0):
            c.start()

    @pl.when(b + 1 < pl.num_programs(0))
    def _():
        for c in first_copies(b + 1, 1 - slot):
            c.start()

    def head(ref, h):
        return ref[:, h * HEAD_DIM:(h + 1) * HEAD_DIM]

    def scores(keys):
        return jnp.concatenate([_qk(head(q_ref, h), keys(h)) for h in range(n_heads)], axis=0)

    def weighted(w, values):
        return jnp.concatenate([jnp.dot(w[h * ts:(h + 1) * ts], values(h), preferred_element_type=F32)
                                for h in range(n_heads)], axis=0)

    pad = jnp.zeros((LANES - ts, HEAD_DIM), BF16)
    query = lax.broadcasted_iota(jnp.int32, (n_heads, ts, LANES), 1).reshape(n_heads * ts, LANES)
    mask = lax.broadcasted_iota(jnp.int32, (n_heads * ts, LANES), 1) < query
    w, carry = _sb_weights(scores(lambda h: jnp.concatenate([head(kn_ref, h), pad], axis=0)),
                           un_ref[...], None, mask)
    acc = weighted(w, lambda h: jnp.concatenate([head(vn_ref, h), pad], axis=0))

    def chunk(kbuf, vbuf, acc, carry):
        w, tot = _sb_weights(scores(lambda h: kbuf[h].astype(BF16)), uc_ref[...], carry)
        return acc + weighted(w, lambda h: vbuf[h].astype(BF16)), carry + tot

    for c in first_copies(b, slot):
        c.wait()
    acc_scr[...], car_scr[...] = chunk(kfirst.at[slot], vfirst.at[slot], acc, carry)

    def more(s):
        return jnp.logical_and(s[0] >= 0, s[1] >= EXP_UNDERFLOW)

    def older(s):
        cs = copies(b, s[0], kmore, vmore, sem.at[2, 0], sem.at[2, 1])
        for c in cs:
            c.start()
        for c in cs:
            c.wait()
        acc_scr[...], car_scr[...] = chunk(kmore, vmore, acc_scr[...], car_scr[...])
        return s[0] - 1, jnp.max(car_scr[...])

    lax.while_loop(more, older, (newest - 1, jnp.max(car_scr[...])))

    for h in range(n_heads):
        o_ref[:, h * HEAD_DIM:(h + 1) * HEAD_DIM] = _gated(acc_scr[pl.ds(h * ts, ts), :],
                                                           head(gt_ref, h)).astype(o_ref.dtype)


def _attn_sample(qkv, gate, cache_k, cache_v, *, layer, mp, bs, ts, n_heads):
    p = cache_k.shape[2]
    d_att = n_heads * HEAD_DIM
    pc = _tile(p, 256)
    r0 = mp // ts
    chunk_bytes = n_heads * pc * HEAD_DIM * 4
    vmem = (2 * (3 * ts * d_att * 2 + ts * d_att * 4 + ts * d_att * 2) + 6 * chunk_bytes
            + 2 * n_heads * ts * LANES * 4 + 2 * 2 * (LANES * LANES + pc * pc) * 2)
    new_spec = lambda c: pl.BlockSpec((ts, d_att), lambda b: (r0 + b, c))
    chunk_buf = lambda n: pltpu.VMEM(n + (n_heads, pc, HEAD_DIM), F32)
    return pl.pallas_call(
        functools.partial(_attn_sample_kernel, layer=layer, pc=pc),
        grid=(bs,),
        in_specs=[new_spec(0), new_spec(1), new_spec(2), new_spec(0),
                  pl.BlockSpec(memory_space=pl.ANY), pl.BlockSpec(memory_space=pl.ANY),
                  pl.BlockSpec((2 * LANES, LANES), lambda b: (0, 0)),
                  pl.BlockSpec((2 * pc, pc), lambda b: (0, 0))],
        out_specs=pl.BlockSpec((ts, d_att), lambda b: (b, 0)),
        out_shape=jax.ShapeDtypeStruct((bs * ts, d_att), BF16),
        scratch_shapes=[chunk_buf((2,)), chunk_buf((2,)), chunk_buf(()), chunk_buf(()),
                        pltpu.VMEM((n_heads * ts, LANES), F32), pltpu.VMEM((n_heads * ts, 1), F32),
                        pltpu.SemaphoreType.DMA((3, 2))],
        compiler_params=_params(("arbitrary",), vmem),
        name="attn_sample",
    )(qkv, qkv, qkv, gate, cache_k, cache_v, _suffix_matrix(LANES), _suffix_matrix(pc))


@jax.jit
def _step(x_prompt, x_sample, cache_k, cache_v, state_h, state_conv, state_pool,
          norm_rec, w_in_rec, conv_w, conv_b, gate_r_w, gate_r_b, gate_i_w, gate_i_b, rg_lambda,
          pool_w, pool_scale, w_out_rec, norm_att, w_in_att, w_out_att, norm_final):
    bp, t, d = x_prompt.shape
    bs, ts, _ = x_sample.shape
    n_rec, n_att = norm_rec.shape[0], norm_att.shape[0]
    d_rnn = state_h.shape[-1]
    d_pool = state_pool.shape[-1]
    n_heads = cache_k.shape[3]
    d_att = n_heads * HEAD_DIM
    mp = bp * t
    assert ts >= POOL_BUF and ts % SUBLANES == 0 and t % ts == 0 and cache_k.shape[4] == HEAD_DIM
    assert gate_r_w.shape[2] == LANES and d_pool // len(POOL_WINDOWS) == 2 * LANES

    x = jnp.concatenate([x_prompt.reshape(mp, d), x_sample.reshape(bs * ts, d)], axis=0)
    sc_pad = jnp.pad(state_conv, ((0, 0), (0, 0), (SUBLANES - (CONV_W - 1), 0), (0, 0)))
    sp_pad = jnp.pad(state_pool, ((0, 0), (0, 0), (POOL_TAIL - POOL_BUF, 0), (0, 0)))

    outs = {k: [] for k in ("hp", "cp", "pp", "hs", "cs", "ps")}
    new_kv = tuple(jnp.zeros((n_att, rows, n_heads, HEAD_DIM), F32) for rows in (mp, mp, bs * ts, bs * ts))
    for layer in range(n_rec + n_att):
        j = layer // 2
        if layer % 2 == 0:
            proj = _norm_matmul(x, norm_rec[j], w_in_rec[j].astype(BF16))
            ya, hp, hs = _rglru(proj, state_h[j], sc_pad[j], conv_w[j], conv_b[j], gate_r_w[j], gate_r_b[j],
                                gate_i_w[j], gate_i_b[j], rg_lambda[j], bp=bp, t=t, bs=bs, ts=ts, d_rnn=d_rnn)
            yb = _pool(proj, sp_pad[j], pool_w[j], pool_scale[j], bp=bp, t=t, bs=bs, ts=ts,
                       d_rnn=d_rnn, d_pool=d_pool)
            w_out = w_out_rec[j].astype(BF16)
            x = _matmul_residual([ya, yb], [w_out[:d_rnn], w_out[d_rnn:]], x)
            frames = proj.reshape(-1, ts, proj.shape[1])

            def last_rows(n, c0, c1):
                prompt = lax.slice(frames, (t // ts - 1, ts - n, c0), (mp // ts, ts, c1), (t // ts, 1, 1))
                sample = lax.slice(frames, (mp // ts, ts - n, c0), (frames.shape[0], ts, c1))
                return prompt, sample

            cp, cs = last_rows(CONV_W - 1, 0, d_rnn)
            pp, ps = last_rows(POOL_BUF, 2 * d_rnn, 2 * d_rnn + d_pool)
            for key, val in (("hp", hp), ("hs", hs), ("cp", cp), ("cs", cs), ("pp", pp), ("ps", ps)):
                outs[key].append(val)
        else:
            qkv, gate, new_kv = _norm_matmul_att(x, norm_att[j], w_in_att[j].astype(BF16), new_kv,
                                                 layer=j, mp=mp, n_heads=n_heads)
            op = _attn_prompt(qkv, gate, bp=bp, t=t, n_heads=n_heads)
            os_ = _attn_sample(qkv, gate, cache_k, cache_v, layer=j, mp=mp, bs=bs, ts=ts, n_heads=n_heads)
            x = _matmul_residual_rows(op, os_, w_out_att[j].astype(BF16), x)

    y = _final_norm(x, norm_final)
    st = {k: jnp.stack(v) for k, v in outs.items()}
    kp, vp, ks, vs = new_kv
    prompt_shape = (n_att, bp, t, n_heads, HEAD_DIM)
    sample_shape = (n_att, bs, ts, n_heads, HEAD_DIM)
    return (y[:mp].reshape(bp, t, d), y[mp:].reshape(bs, ts, d),
            kp.reshape(prompt_shape), vp.reshape(prompt_shape), st["hp"], st["cp"], st["pp"],
            ks.reshape(sample_shape), vs.reshape(sample_shape), st["hs"], st["cs"], st["ps"])


def kernel(x_prompt, x_sample, cache_k, cache_v, state_h, state_conv, state_pool, norm_rec, w_in_rec, conv_w, conv_b, gate_r_w, gate_r_b, gate_i_w, gate_i_b, rg_lambda, pool_w, pool_scale, w_out_rec, norm_att, w_in_att, w_out_att, norm_final):
    return _step(x_prompt, x_sample, cache_k, cache_v, state_h, state_conv, state_pool, norm_rec, w_in_rec,
                 conv_w, conv_b, gate_r_w, gate_r_b, gate_i_w, gate_i_b, rg_lambda, pool_w, pool_scale,
                 w_out_rec, norm_att, w_in_att, w_out_att, norm_final)
```

```python
import functools

import jax
import jax.numpy as jnp
from jax import lax
from jax.experimental import pallas as pl
from jax.experimental.pallas import tpu as pltpu

F32 = jnp.float32
BF16 = jnp.bfloat16

EPS = 1e-6
RG_C = 8.0
CONV_W = 4
POOL_WINDOWS = (2, 4, 8, 16)
POOL_BUF = max(POOL_WINDOWS) - 1
HEAD_DIM = 128
ATT_SCALE = HEAD_DIM ** -0.5
EXP_UNDERFLOW = -105.0

LANES = 128
SUBLANES = 8
VMEM_LIMIT_CAP = 60000 * 1024
VMEM_SLACK = 8 * 1024 * 1024


def _params(semantics, buffer_bytes):
    limit = min(VMEM_LIMIT_CAP, buffer_bytes + VMEM_SLACK)
    return pltpu.CompilerParams(dimension_semantics=semantics, vmem_limit_bytes=limit)


def _tile(n, pref):
    t = min(n, pref)
    while n % t:
        t //= 2
    return t


def _sigmoid(x):
    return 1.0 / (1.0 + jnp.exp(-x))


LOG2_E = 1.4426950408889634


def _softplus(x):
    return jnp.maximum(x, 0.0) + jnp.log(1.0 + jnp.exp2(jnp.abs(x) * -LOG2_E))


NORM_ROWS = 128


def _normalize_rows(x_ref, g_ref, xn_ref):
    @pl.when(pl.program_id(1) == 0)
    def _():
        g = g_ref[...]

        def body(c, _):
            rows = pl.ds(pl.multiple_of(c * NORM_ROWS, NORM_ROWS), NORM_ROWS)
            x = x_ref[rows, :]
            ms = jnp.mean(x * x, axis=-1, keepdims=True)
            xn_ref[rows, :] = (x * lax.rsqrt(ms + EPS) * g).astype(BF16)
            return 0

        lax.fori_loop(0, x_ref.shape[0] // NORM_ROWS, body, 0)


def _norm_mm_kernel(x_ref, g_ref, w_ref, o_ref, xn_ref):
    _normalize_rows(x_ref, g_ref, xn_ref)
    o_ref[...] = jnp.dot(xn_ref[...], w_ref[...], preferred_element_type=F32)


def _norm_matmul(x, g, w_all, layer, *, tm_pref=1024, tn_pref=1024):
    m, k = x.shape
    n = w_all.shape[2]
    tm, tn = _tile(m, tm_pref), _tile(n, tn_pref)
    vmem = 2 * tm * k * 4 + tm * k * 2 + 2 * k * tn * 2 + 2 * tm * tn * 4
    return pl.pallas_call(
        _norm_mm_kernel,
        grid=(m // tm, n // tn),
        in_specs=[pl.BlockSpec((tm, k), lambda i, j: (i, 0)),
                  pl.BlockSpec((1, k), lambda i, j: (0, 0)),
                  pl.BlockSpec((None, k, tn), lambda i, j: (layer, 0, j))],
        out_specs=pl.BlockSpec((tm, tn), lambda i, j: (i, j)),
        out_shape=jax.ShapeDtypeStruct((m, n), F32),
        scratch_shapes=[pltpu.VMEM((tm, k), BF16)],
        compiler_params=_params(("arbitrary", "arbitrary"), vmem),
        name="norm_matmul",
    )(x, g.reshape(1, k), w_all)


def _norm_mm_att_kernel(x_ref, g_ref, w_ref, *refs, layer, n_top, per):
    qkv_ref, gt_ref, kp_hbm, vp_hbm, ks_hbm, vs_hbm, xn_ref, kv_scr, sem = refs[-9:]
    layers = (layer,) if len(refs) > 9 else range(kp_hbm.shape[0])
    _normalize_rows(x_ref, g_ref, xn_ref)
    i, j = pl.program_id(0), pl.program_id(1)
    tm, tn = qkv_ref.shape
    hpt = tn // HEAD_DIM
    top = i < n_top

    def project():
        return jnp.dot(xn_ref[...], w_ref[...], preferred_element_type=F32)

    def copies(jj, prompt_rows):
        slot = (jj - per) % 2
        dst = ((kp_hbm, ks_hbm), (vp_hbm, vs_hbm))[jj // per - 1][0 if prompt_rows else 1]
        rows = pl.ds(i * tm if prompt_rows else (i - n_top) * tm, tm)
        return [pltpu.make_async_copy(kv_scr.at[slot, :, pl.ds(h * HEAD_DIM, HEAD_DIM)],
                                      dst.at[l, rows, (jj % per) * hpt + h, :], sem.at[slot])
                for l in layers for h in range(hpt)]

    def for_rows(jj, action):
        @pl.when(top)
        def _():
            for c in copies(jj, True):
                action(c)

        @pl.when(jnp.logical_not(top))
        def _():
            for c in copies(jj, False):
                action(c)

    @pl.when(j < per)
    def _():
        qkv_ref[...] = (project() * ATT_SCALE).astype(BF16)

    for jj in range(per, 3 * per + 2):
        @pl.when(j == jj)
        def _(jj=jj):
            if jj - 2 >= per:
                for_rows(jj - 2, lambda c: c.wait())
            if jj < 3 * per:
                slot = (jj - per) % 2
                kv_scr[slot] = project()
                qkv_ref[...] = kv_scr[slot].astype(BF16)
                for_rows(jj, lambda c: c.start())

    @pl.when(j >= 3 * per)
    def _():
        gt_ref[...] = project()


def _norm_matmul_att(x, g, w_all, new_kv, *, layer, mp, n_heads, tm_pref=1024):
    m, k = x.shape
    n_layers = w_all.shape[0]
    d_att = n_heads * HEAD_DIM
    tn = d_att // 2
    per = d_att // tn
    assert per >= 2 and w_all.shape[2] == 4 * d_att
    tm = _tile(mp, tm_pref)
    while (m - mp) % tm:
        tm //= 2
    any_spec = pl.BlockSpec(memory_space=pl.ANY)
    carried = () if new_kv is None else tuple(new_kv)
    kv_shapes = [jax.ShapeDtypeStruct((n_layers, rows, n_heads, HEAD_DIM), F32) for rows in (mp, mp, m - mp, m - mp)]
    vmem = 2 * tm * k * 4 + tm * k * 2 + 2 * k * tn * 2 + 2 * tm * tn * (2 + 4) + 3 * tm * tn * 4
    outs = pl.pallas_call(
        functools.partial(_norm_mm_att_kernel, layer=layer, n_top=mp // tm, per=per),
        grid=(m // tm, 4 * per),
        in_specs=[pl.BlockSpec((tm, k), lambda i, j: (i, 0)),
                  pl.BlockSpec((1, k), lambda i, j: (0, 0)),
                  pl.BlockSpec((None, k, tn), lambda i, j: (layer, 0, j))] + [any_spec] * len(carried),
        out_specs=[pl.BlockSpec((tm, tn), lambda i, j: (i, jnp.minimum(j, 3 * per - 1))),
                   pl.BlockSpec((tm, tn), lambda i, j: (i, jnp.maximum(j - 3 * per, 0))),
                   any_spec, any_spec, any_spec, any_spec],
        out_shape=[jax.ShapeDtypeStruct((m, 3 * d_att), BF16), jax.ShapeDtypeStruct((m, d_att), F32)] + kv_shapes,
        input_output_aliases={3 + n: 2 + n for n in range(len(carried))},
        scratch_shapes=[pltpu.VMEM((tm, k), BF16), pltpu.VMEM((2, tm, tn), F32),
                        pltpu.SemaphoreType.DMA((2,))],
        compiler_params=_params(("arbitrary", "arbitrary"), vmem),
        name="norm_matmul_att",
    )(x, g.reshape(1, k), w_all, *carried)
    return outs[0], outs[1], tuple(outs[2:])


def _mm_res_kernel(*refs, n_pairs):
    lhs, ws = refs[:n_pairs], refs[n_pairs:2 * n_pairs]
    res_ref, out_ref = refs[2 * n_pairs], refs[2 * n_pairs + 1]
    acc = res_ref[...]
    for l, w in zip(lhs, ws):
        acc = acc + jnp.dot(l[...], w[...], preferred_element_type=F32)
    out_ref[...] = acc


def _matmul_residual(lhs_list, w_all, layer, res, *, tm_pref=1024, tn_pref=1024):
    m, n = res.shape
    tm, tn = _tile(m, tm_pref), _tile(n, tn_pref)
    ks = [l.shape[1] for l in lhs_list]
    starts = [sum(ks[:p]) for p in range(len(ks))]
    assert all(r % k == 0 for r, k in zip(starts, ks)) and sum(ks) == w_all.shape[1]
    vmem = sum(2 * tm * k * 2 + 2 * k * tn * 2 for k in ks) + 4 * tm * tn * 4
    in_specs = ([pl.BlockSpec((tm, k), lambda i, j: (i, 0)) for k in ks]
                + [pl.BlockSpec((None, k, tn), lambda i, j, blk=r // k: (layer, blk, j)) for r, k in zip(starts, ks)]
                + [pl.BlockSpec((tm, tn), lambda i, j: (i, j))])
    return pl.pallas_call(
        functools.partial(_mm_res_kernel, n_pairs=len(ks)),
        grid=(m // tm, n // tn),
        in_specs=in_specs,
        out_specs=pl.BlockSpec((tm, tn), lambda i, j: (i, j)),
        out_shape=jax.ShapeDtypeStruct((m, n), F32),
        compiler_params=_params(("arbitrary", "arbitrary"), vmem),
        name="matmul_residual",
    )(*lhs_list, *([w_all] * len(ks)), res)


def _mm_res_rows_kernel(top_ref, bottom_ref, w_ref, res_ref, out_ref, *, n_top):
    i = pl.program_id(0)

    def emit(lhs_ref):
        out_ref[...] = res_ref[...] + jnp.dot(lhs_ref[...], w_ref[...], preferred_element_type=F32)

    pl.when(i < n_top)(functools.partial(emit, top_ref))
    pl.when(i >= n_top)(functools.partial(emit, bottom_ref))


def _matmul_residual_rows(top, bottom, w_all, layer, res, *, tm_pref=1024, tn_pref=1024):
    m, n = res.shape
    k = w_all.shape[1]
    tm = _tile(top.shape[0], tm_pref)
    while bottom.shape[0] % tm:
        tm //= 2
    tn = _tile(n, tn_pref)
    n_top = top.shape[0] // tm
    vmem = 4 * tm * k * 2 + 2 * k * tn * 2 + 4 * tm * tn * 4
    return pl.pallas_call(
        functools.partial(_mm_res_rows_kernel, n_top=n_top),
        grid=(m // tm, n // tn),
        in_specs=[pl.BlockSpec((tm, k), lambda i, j: (jnp.minimum(i, n_top - 1), 0)),
                  pl.BlockSpec((tm, k), lambda i, j: (jnp.maximum(i - n_top, 0), 0)),
                  pl.BlockSpec((None, k, tn), lambda i, j: (layer, 0, j)),
                  pl.BlockSpec((tm, tn), lambda i, j: (i, j))],
        out_specs=pl.BlockSpec((tm, tn), lambda i, j: (i, j)),
        out_shape=jax.ShapeDtypeStruct((m, n), F32),
        compiler_params=_params(("arbitrary", "arbitrary"), vmem),
        name="matmul_residual_rows",
    )(top, bottom, w_all, res)


def _final_norm_kernel(x_ref, g_ref, top_ref, bottom_ref, *, n_top):
    x = x_ref[...]
    ms = jnp.mean(x * x, axis=-1, keepdims=True)
    y = x * lax.rsqrt(ms + EPS) * g_ref[...]
    i = pl.program_id(0)

    @pl.when(i < n_top)
    def _():
        top_ref[...] = y

    @pl.when(i >= n_top)
    def _():
        bottom_ref[...] = y


def _final_norm(x, g, mp):
    m, d = x.shape
    tm = _tile(mp, 256)
    while (m - mp) % tm:
        tm //= 2
    n_top = mp // tm
    return pl.pallas_call(
        functools.partial(_final_norm_kernel, n_top=n_top),
        grid=(m // tm,),
        in_specs=[pl.BlockSpec((tm, d), lambda i: (i, 0)), pl.BlockSpec((1, d), lambda i: (0, 0))],
        out_specs=[pl.BlockSpec((tm, d), lambda i: (jnp.minimum(i, n_top - 1), 0)),
                   pl.BlockSpec((tm, d), lambda i: (jnp.maximum(i - n_top, 0), 0))],
        out_shape=[jax.ShapeDtypeStruct((mp, d), F32), jax.ShapeDtypeStruct((m - mp, d), F32)],
        compiler_params=_params(("arbitrary",), 6 * tm * d * 4),
        name="final_norm",
    )(x, g.reshape(1, d))


SCAN_ROWS = 256
SCAN_STREAMS = 4


def _scan_tiles(a, u, carry):
    r = a.shape[0]
    nt = r // SUBLANES
    a3 = a.reshape(nt, SUBLANES, LANES)
    u3 = u.reshape(nt, SUBLANES, LANES)
    sub = lax.broadcasted_iota(jnp.int32, a3.shape, 1)
    for d in (1, 2, 4):
        keep = sub >= d
        a_prev = pltpu.roll(a3, d, 1)
        u_prev = pltpu.roll(u3, d, 1)
        u3 = jnp.where(keep, a3 * u_prev + u3, u3)
        a3 = jnp.where(keep, a3 * a_prev, a3)
    hs = []
    for t in range(nt):
        h_t = u3[t] + a3[t] * carry
        carry = h_t[SUBLANES - 1:SUBLANES, :]
        hs.append(h_t)
    return jnp.concatenate(hs, axis=0), carry


def _rglru_kernel(xa_ref, ga_ref, h0_ref, sc_ref, cw_ref, cb_ref, wr_ref, br_ref, wi_ref, bi_ref, lam_ref,
                  out_ref, hp_ref, hs_ref, *, bp, t, bs, ts):
    mp = bp * t
    cw = cw_ref[...]
    cb = cb_ref[...]
    wr = wr_ref[...].astype(BF16)
    wi = wi_ref[...].astype(BF16)
    br = br_ref[...]
    bi = bi_ref[...]
    neg_c_sp = -RG_C * _softplus(-lam_ref[...])

    def conv(ext, n):
        def tap(k):
            back = CONV_W - 1 - k
            rows = ext if back == 0 else pltpu.roll(ext, back, 0)
            return rows[SUBLANES:] * cw[k:k + 1]

        y = cb + tap(0)
        for k in range(1, CONV_W):
            y = y + tap(k)
        return y

    def decay_and_input(xc):
        xb = xc.astype(BF16)
        r = _sigmoid(jnp.dot(xb, wr, preferred_element_type=F32) + br)
        i = _sigmoid(jnp.dot(xb, wi, preferred_element_type=F32) + bi)
        log_a = r * neg_c_sp
        a = jnp.exp(log_a)
        v = -jnp.tanh(log_a) * (1.0 + a * a)
        root = jnp.where(v > 0.0, v * lax.rsqrt(v), 0.0)
        return a, root * (i * xc)

    def emit(rows, a, u, carry):
        h, carry = _scan_tiles(a, u, carry)
        g = ga_ref[rows, :]
        out_ref[rows, :] = (h * (g * _sigmoid(g))).astype(out_ref.dtype)
        return carry

    rp = _tile(t, SCAN_ROWS)

    def prompt_chunk(c, state):
        new = []
        for b in range(bp):
            tail, carry = state[b]
            rows = pl.ds(pl.multiple_of(b * t + c * rp, SUBLANES), rp)
            x = xa_ref[rows, :]
            a, u = decay_and_input(conv(jnp.concatenate([tail, x], axis=0), rp))
            new.append((x[rp - SUBLANES:, :], emit(rows, a, u, carry)))
        return tuple(new)

    start = (jnp.zeros((SUBLANES, LANES), F32), jnp.zeros((1, LANES), F32))
    state = lax.fori_loop(0, t // rp, prompt_chunk, (start,) * bp)
    for b in range(bp):
        hp_ref[b:b + 1, :] = state[b][1]

    ns = _tile(bs, SCAN_STREAMS)

    def sample_group(c, _):
        rows = [pl.ds(pl.multiple_of(mp + (c * ns + k) * ts, SUBLANES), ts) for k in range(ns)]
        xc = [conv(jnp.concatenate([sc_ref[c * ns + k], xa_ref[rows[k], :]], axis=0), ts) for k in range(ns)]
        a, u = decay_and_input(jnp.concatenate(xc, axis=0))
        for k in range(ns):
            seg = slice(k * ts, (k + 1) * ts)
            s = c * ns + k
            hs_ref[pl.ds(s, 1), :] = emit(rows[k], a[seg], u[seg], h0_ref[pl.ds(s, 1), :])
        return 0

    lax.fori_loop(0, bs // ns, sample_group, 0)


def _rglru(proj, h0, sc_pad, cw, cb, wr, br, wi, bi, lam, *, bp, t, bs, ts, d_rnn):
    m = proj.shape[0]
    nb = d_rnn // LANES
    row = lambda v: v.reshape(1, d_rnn)
    vec_spec = pl.BlockSpec((1, LANES), lambda n: (0, n))
    w_spec = pl.BlockSpec((None, LANES, LANES), lambda n: (n, 0, 0))
    vmem = 2 * (2 * m * LANES * 4 + m * LANES * 2)
    return pl.pallas_call(
        functools.partial(_rglru_kernel, bp=bp, t=t, bs=bs, ts=ts),
        grid=(nb,),
        in_specs=[pl.BlockSpec((m, LANES), lambda n: (0, n)),
                  pl.BlockSpec((m, LANES), lambda n: (0, nb + n)),
                  pl.BlockSpec((bs, LANES), lambda n: (0, n)),
                  pl.BlockSpec((bs, SUBLANES, LANES), lambda n: (0, 0, n)),
                  pl.BlockSpec((CONV_W, LANES), lambda n: (0, n)),
                  vec_spec, w_spec, vec_spec, w_spec, vec_spec, vec_spec],
        out_specs=[pl.BlockSpec((m, LANES), lambda n: (0, n)),
                   pl.BlockSpec((bp, LANES), lambda n: (0, n)),
                   pl.BlockSpec((bs, LANES), lambda n: (0, n))],
        out_shape=[jax.ShapeDtypeStruct((m, d_rnn), BF16),
                   jax.ShapeDtypeStruct((bp, d_rnn), F32),
                   jax.ShapeDtypeStruct((bs, d_rnn), F32)],
        compiler_params=_params(("arbitrary",), vmem),
        name="rglru_mixer",
    )(proj, proj, h0, sc_pad, cw, row(cb), wr, row(br), wi, row(bi), row(lam))


POOL_TAIL = 16
POOL_ROWS = 128


def _pool_kernel(xb_ref, gb_ref, sp_ref, pw_ref, ps_ref, out_ref, m_scr, tail_scr, *, npb, bpt, ts):
    g = pl.program_id(0)
    rb = pl.program_id(1)
    rbk, gw = xb_ref.shape
    wf = lax.shift_left(jnp.int32(2), g).astype(F32)

    def window_means(ext, n, pos0):
        s2 = ext[1:] + ext[:-1]
        s4 = s2[2:] + s2[:-2]
        s8 = s4[4:] + s4[:-4]
        s16 = s8[8:] + s8[:-8]
        x = ext[POOL_TAIL:]
        win = jnp.where(g == 0, s2[POOL_TAIL - 1:],
                        jnp.where(g == 1, s4[POOL_TAIL - 3:],
                                  jnp.where(g == 2, s8[POOL_TAIL - 7:], s16[POOL_TAIL - 15:])))
        if pos0 is None:
            cnt = wf
        else:
            pos = pos0 + lax.broadcasted_iota(jnp.int32, (n, gw), 0)
            cnt = jnp.minimum(wf, (pos + 1).astype(F32))
        return win / cnt - x

    @pl.when(rb < npb)
    def _():
        blk = rb % bpt

        @pl.when(blk == 0)
        def _():
            tail_scr[...] = jnp.zeros_like(tail_scr)

        rc = _tile(rbk, POOL_ROWS)

        def body(c, tail):
            rows = pl.ds(pl.multiple_of(c * rc, SUBLANES), rc)
            x = xb_ref[rows, :]
            ext = jnp.concatenate([tail, x], axis=0)
            m_scr[rows, :] = window_means(ext, rc, blk * rbk + c * rc).astype(BF16)
            return x[rc - POOL_TAIL:, :]

        tail_scr[...] = lax.fori_loop(0, rbk // rc, body, tail_scr[...])

    @pl.when(rb >= npb)
    def _():
        s0 = (rb - npb) * (rbk // ts)

        def body(s, _):
            rows = pl.ds(pl.multiple_of(s * ts, SUBLANES), ts)
            ext = jnp.concatenate([sp_ref[s0 + s], xb_ref[rows, :]], axis=0)
            m_scr[rows, :] = window_means(ext, ts, None).astype(BF16)
            return 0

        lax.fori_loop(0, rbk // ts, body, 0)

    y = jnp.dot(m_scr[...], pw_ref[...].astype(BF16), preferred_element_type=F32) * ps_ref[...]
    gt = gb_ref[...]
    out_ref[...] = (y * (gt * _sigmoid(gt))).astype(out_ref.dtype)


def _pool(proj, sp_pad, pw, ps, *, bp, t, bs, ts, d_rnn, d_pool):
    m = proj.shape[0]
    ng = len(POOL_WINDOWS)
    gw = d_pool // ng
    ms = bs * ts
    rbk = min(1024, t, ms)
    while t % rbk or ms % rbk:
        rbk //= 2
    npb = bp * t // rbk
    xcol = 2 * d_rnn // gw
    gcol = (2 * d_rnn + d_pool) // gw
    vmem = 2 * (2 * rbk * gw * 4 + rbk * gw * 2) + rbk * gw * 2 + 2 * bs * POOL_TAIL * gw * 4 + 2 * gw * gw * 4
    return pl.pallas_call(
        functools.partial(_pool_kernel, npb=npb, bpt=t // rbk, ts=ts),
        grid=(ng, m // rbk),
        in_specs=[pl.BlockSpec((rbk, gw), lambda g, r: (r, xcol + g)),
                  pl.BlockSpec((rbk, gw), lambda g, r: (r, gcol + g)),
                  pl.BlockSpec((bs, POOL_TAIL, gw), lambda g, r: (0, 0, g)),
                  pl.BlockSpec((None, gw, gw), lambda g, r: (g, 0, 0)),
                  pl.BlockSpec((1, gw), lambda g, r: (0, g))],
        out_specs=pl.BlockSpec((rbk, gw), lambda g, r: (r, g)),
        out_shape=jax.ShapeDtypeStruct((m, d_pool), BF16),
        scratch_shapes=[pltpu.VMEM((rbk, gw), BF16), pltpu.VMEM((POOL_TAIL, gw), F32)],
        compiler_params=_params(("arbitrary", "arbitrary"), vmem),
        name="pool_mixer",
    )(proj, proj, sp_pad, pw, ps.reshape(1, d_pool))


def _suffix_matrix(n):
    j = lax.broadcasted_iota(jnp.int32, (n, n), 0)
    s = lax.broadcasted_iota(jnp.int32, (n, n), 1)
    u = jnp.where(j > s, -1.0, 0.0).astype(BF16)
    return jnp.concatenate([u, u], axis=0)


def _qk(q, k):
    return lax.dot_general(q, k, (((1,), (1,)), ((), ())), preferred_element_type=F32)


def _strictly_earlier(shape):
    return lax.broadcasted_iota(jnp.int32, shape, 1) < lax.broadcasted_iota(jnp.int32, shape, 0)


def _sb_weights(z, uu, carry=None, mask=None):
    sp = _softplus(z)
    spm = sp if mask is None else jnp.where(mask, sp, 0.0)
    hi = spm.astype(BF16)
    lo = (spm - hi.astype(F32)).astype(BF16)
    e = z - sp + jnp.dot(jnp.concatenate([hi, lo], axis=1), uu, preferred_element_type=F32)
    w = jnp.exp(e if carry is None else e + carry)
    if mask is not None:
        w = jnp.where(mask, w, 0.0)
    return w.astype(BF16), -jnp.sum(spm, axis=1, keepdims=True)


def _sb_tile(q, k, v, uu, carry=None, mask=None):
    w, tot = _sb_weights(_qk(q, k), uu, carry, mask)
    return jnp.dot(w, v, preferred_element_type=F32), tot


def _gated(acc, g):
    return acc * (g * _sigmoid(g))


def _attn_prompt_kernel(q_ref, k_ref, v_ref, gt_ref, uu_ref, o_ref, acc_scr, car_scr, worst_ref, *, tq):
    uu = uu_ref[...]
    mask = _strictly_earlier((tq, tq))
    nq = q_ref.shape[0] // tq

    for qi in range(nq):
        rows = pl.ds(qi * tq, tq)
        q = q_ref[rows, :]
        acc, carry = _sb_tile(q, k_ref[rows, :], v_ref[rows, :], uu, None, mask)
        if qi > 0:
            prev = pl.ds((qi - 1) * tq, tq)
            pv, tot = _sb_tile(q, k_ref[prev, :], v_ref[prev, :], uu, carry)
            acc, carry = acc + pv, carry + tot
        o_ref[rows, :] = _gated(acc, gt_ref[rows, :]).astype(o_ref.dtype)
        if qi > 1:
            acc_scr[qi] = acc
            car_scr[qi] = carry
            worst_ref[qi] = jnp.max(carry)

    def finish(qi, _):
        @pl.when(worst_ref[qi] >= EXP_UNDERFLOW)
        def _():
            rows = pl.ds(pl.multiple_of(qi * tq, tq), tq)
            q = q_ref[rows, :]

            def more(s):
                return jnp.logical_and(s[0] >= 0, jnp.max(s[2]) >= EXP_UNDERFLOW)

            def older(s):
                j, acc, carry = s
                old = pl.ds(pl.multiple_of(j * tq, tq), tq)
                pv, tot = _sb_tile(q, k_ref[old, :], v_ref[old, :], uu, carry)
                return j - 1, acc + pv, carry + tot

            _, acc, _ = lax.while_loop(more, older, (qi - 2, acc_scr[qi], car_scr[qi]))
            o_ref[rows, :] = _gated(acc, gt_ref[rows, :]).astype(o_ref.dtype)

        return 0

    lax.fori_loop(2, nq, finish, 0)


def _attn_prompt(qkv, gate, *, bp, t, n_heads):
    tq = _tile(t, 256)
    d_att = n_heads * HEAD_DIM
    nq = t // tq
    state = pltpu.VMEM((nq, tq, LANES), F32), pltpu.VMEM((nq, tq, 1), F32), pltpu.SMEM((nq,), F32)
    vmem = 2 * (3 * t * LANES * 2 + t * LANES * 4 + 2 * tq * tq * 2 + t * LANES * 2) + 2 * t * LANES * 4
    col = lambda c: pl.BlockSpec((t, LANES), lambda b, h: (b, c * n_heads + h))
    return pl.pallas_call(
        functools.partial(_attn_prompt_kernel, tq=tq),
        grid=(bp, n_heads),
        in_specs=[col(0), col(1), col(2), col(0), pl.BlockSpec((2 * tq, tq), lambda b, h: (0, 0))],
        out_specs=pl.BlockSpec((t, LANES), lambda b, h: (b, h)),
        out_shape=jax.ShapeDtypeStruct((bp * t, d_att), BF16),
        scratch_shapes=list(state),
        compiler_params=_params(("arbitrary", "arbitrary"), vmem),
        name="attn_prompt",
    )(qkv, qkv, qkv, gate, _suffix_matrix(tq))


def _attn_sample_kernel(q_ref, kn_ref, vn_ref, gt_ref, kc_hbm, vc_hbm, un_ref, uc_ref, o_ref,
                        kfirst, vfirst, kmore, vmore, acc_scr, car_scr, sem, *, layer, pc):
    b = pl.program_id(0)
    n_heads, ts = kmore.shape[0], q_ref.shape[0]
    newest = kc_hbm.shape[2] // pc - 1
    slot = b % 2

    def copies(stream, chunk, kdst, vdst, ksem, vsem):
        pos = pl.ds(chunk * pc, pc)
        out = []
        for h in range(n_heads):
            out.append(pltpu.make_async_copy(kc_hbm.at[layer, stream, pos, h, :], kdst.at[h], ksem))
            out.append(pltpu.make_async_copy(vc_hbm.at[layer, stream, pos, h, :], vdst.at[h], vsem))
        return out

    def first_copies(stream, s):
        return copies(stream, newest, kfirst.at[s], vfirst.at[s], sem.at[s, 0], sem.at[s, 1])

    @pl.when(b == 0)
    def _():
        for c in first_copies(0, 0):
            c.start()

    @pl.when(b + 1 < pl.num_programs(0))
    def _():
        for c in first_copies(b + 1, 1 - slot):
            c.start()

    def head(ref, h):
        return ref[:, h * HEAD_DIM:(h + 1) * HEAD_DIM]

    def scores(keys):
        return jnp.concatenate([_qk(head(q_ref, h), keys(h)) for h in range(n_heads)], axis=0)

    def weighted(w, values):
        return jnp.concatenate([jnp.dot(w[h * ts:(h + 1) * ts], values(h), preferred_element_type=F32)
                                for h in range(n_heads)], axis=0)

    pad = jnp.zeros((LANES - ts, HEAD_DIM), BF16)
    query = lax.broadcasted_iota(jnp.int32, (n_heads, ts, LANES), 1).reshape(n_heads * ts, LANES)
    mask = lax.broadcasted_iota(jnp.int32, (n_heads * ts, LANES), 1) < query
    w, carry = _sb_weights(scores(lambda h: jnp.concatenate([head(kn_ref, h), pad], axis=0)),
                           un_ref[...], None, mask)
    acc = weighted(w, lambda h: jnp.concatenate([head(vn_ref, h), pad], axis=0))

    def chunk(kbuf, vbuf, acc, carry):
        w, tot = _sb_weights(scores(lambda h: kbuf[h].astype(BF16)), uc_ref[...], carry)
        return acc + weighted(w, lambda h: vbuf[h].astype(BF16)), carry + tot

    for c in first_copies(b, slot):
        c.wait()
    acc_scr[...], car_scr[...] = chunk(kfirst.at[slot], vfirst.at[slot], acc, carry)

    def more(s):
        return jnp.logical_and(s[0] >= 0, s[1] >= EXP_UNDERFLOW)

    def older(s):
        cs = copies(b, s[0], kmore, vmore, sem.at[2, 0], sem.at[2, 1])
        for c in cs:
            c.start()
        for c in cs:
            c.wait()
        acc_scr[...], car_scr[...] = chunk(kmore, vmore, acc_scr[...], car_scr[...])
        return s[0] - 1, jnp.max(car_scr[...])

    lax.while_loop(more, older, (newest - 1, jnp.max(car_scr[...])))

    for h in range(n_heads):
        o_ref[:, h * HEAD_DIM:(h + 1) * HEAD_DIM] = _gated(acc_scr[pl.ds(h * ts, ts), :],
                                                           head(gt_ref, h)).astype(o_ref.dtype)


def _attn_sample(qkv, gate, cache_k, cache_v, *, layer, mp, bs, ts, n_heads):
    p = cache_k.shape[2]
    d_att = n_heads * HEAD_DIM
    pc = _tile(p, 256)
    r0 = mp // ts
    chunk_bytes = n_heads * pc * HEAD_DIM * 4
    vmem = (2 * (3 * ts * d_att * 2 + ts * d_att * 4 + ts * d_att * 2) + 6 * chunk_bytes
            + 2 * n_heads * ts * LANES * 4 + 2 * 2 * (LANES * LANES + pc * pc) * 2)
    new_spec = lambda c: pl.BlockSpec((ts, d_att), lambda b: (r0 + b, c))
    chunk_buf = lambda n: pltpu.VMEM(n + (n_heads, pc, HEAD_DIM), F32)
    return pl.pallas_call(
        functools.partial(_attn_sample_kernel, layer=layer, pc=pc),
        grid=(bs,),
        in_specs=[new_spec(0), new_spec(1), new_spec(2), new_spec(0),
                  pl.BlockSpec(memory_space=pl.ANY), pl.BlockSpec(memory_space=pl.ANY),
                  pl.BlockSpec((2 * LANES, LANES), lambda b: (0, 0)),
                  pl.BlockSpec((2 * pc, pc), lambda b: (0, 0))],
        out_specs=pl.BlockSpec((ts, d_att), lambda b: (b, 0)),
        out_shape=jax.ShapeDtypeStruct((bs * ts, d_att), BF16),
        scratch_shapes=[chunk_buf((2,)), chunk_buf((2,)), chunk_buf(()), chunk_buf(()),
                        pltpu.VMEM((n_heads * ts, LANES), F32), pltpu.VMEM((n_heads * ts, 1), F32),
                        pltpu.SemaphoreType.DMA((3, 2))],
        compiler_params=_params(("arbitrary",), vmem),
        name="attn_sample",
    )(qkv, qkv, qkv, gate, cache_k, cache_v, _suffix_matrix(LANES), _suffix_matrix(pc))


@jax.jit
def _step(x_prompt, x_sample, cache_k, cache_v, state_h, state_conv, state_pool,
          norm_rec, w_in_rec, conv_w, conv_b, gate_r_w, gate_r_b, gate_i_w, gate_i_b, rg_lambda,
          pool_w, pool_scale, w_out_rec, norm_att, w_in_att, w_out_att, norm_final):
    bp, t, d = x_prompt.shape
    bs, ts, _ = x_sample.shape
    n_rec, n_att = norm_rec.shape[0], norm_att.shape[0]
    d_rnn = state_h.shape[-1]
    d_pool = state_pool.shape[-1]
    n_heads = cache_k.shape[3]
    mp = bp * t
    assert ts >= POOL_BUF and ts % SUBLANES == 0 and t % ts == 0 and cache_k.shape[4] == HEAD_DIM
    assert gate_r_w.shape[2] == LANES and d_pool // len(POOL_WINDOWS) == 2 * LANES

    x = jnp.concatenate([x_prompt.reshape(mp, d), x_sample.reshape(bs * ts, d)], axis=0)
    sc_pad = jnp.pad(state_conv, ((0, 0), (0, 0), (SUBLANES - (CONV_W - 1), 0), (0, 0)))
    sp_pad = jnp.pad(state_pool, ((0, 0), (0, 0), (POOL_TAIL - POOL_BUF, 0), (0, 0)))
    w_in_rec, w_out_rec, w_in_att, w_out_att = (w.astype(BF16) for w in (w_in_rec, w_out_rec, w_in_att, w_out_att))

    outs = {k: [] for k in ("hp", "cp", "pp", "hs", "cs", "ps")}
    new_kv = None
    for layer in range(n_rec + n_att):
        j = layer // 2
        if layer % 2 == 0:
            proj = _norm_matmul(x, norm_rec[j], w_in_rec, j)
            ya, hp, hs = _rglru(proj, state_h[j], sc_pad[j], conv_w[j], conv_b[j], gate_r_w[j], gate_r_b[j],
                                gate_i_w[j], gate_i_b[j], rg_lambda[j], bp=bp, t=t, bs=bs, ts=ts, d_rnn=d_rnn)
            yb = _pool(proj, sp_pad[j], pool_w[j], pool_scale[j], bp=bp, t=t, bs=bs, ts=ts,
                       d_rnn=d_rnn, d_pool=d_pool)
            x = _matmul_residual([ya, yb], w_out_rec, j, x)
            frames = proj.reshape(-1, ts, proj.shape[1])

            def last_rows(n, c0, c1):
                prompt = lax.slice(frames, (t // ts - 1, ts - n, c0), (mp // ts, ts, c1), (t // ts, 1, 1))
                sample = lax.slice(frames, (mp // ts, ts - n, c0), (frames.shape[0], ts, c1))
                return prompt, sample

            cp, cs = last_rows(CONV_W - 1, 0, d_rnn)
            pp, ps = last_rows(POOL_BUF, 2 * d_rnn, 2 * d_rnn + d_pool)
            for key, val in (("hp", hp), ("hs", hs), ("cp", cp), ("cs", cs), ("pp", pp), ("ps", ps)):
                outs[key].append(val)
        else:
            qkv, gate, new_kv = _norm_matmul_att(x, norm_att[j], w_in_att, new_kv, layer=j, mp=mp, n_heads=n_heads)
            op = _attn_prompt(qkv, gate, bp=bp, t=t, n_heads=n_heads)
            os_ = _attn_sample(qkv, gate, cache_k, cache_v, layer=j, mp=mp, bs=bs, ts=ts, n_heads=n_heads)
            x = _matmul_residual_rows(op, os_, w_out_att, j, x)

    y_prompt, y_sample = _final_norm(x, norm_final, mp)
    st = {k: jnp.stack(v) for k, v in outs.items()}
    kp, vp, ks, vs = new_kv
    prompt_shape = (n_att, bp, t, n_heads, HEAD_DIM)
    sample_shape = (n_att, bs, ts, n_heads, HEAD_DIM)
    return (y_prompt.reshape(bp, t, d), y_sample.reshape(bs, ts, d),
            kp.reshape(prompt_shape), vp.reshape(prompt_shape), st["hp"], st["cp"], st["pp"],
            ks.reshape(sample_shape), vs.reshape(sample_shape), st["hs"], st["cs"], st["ps"])


def kernel(x_prompt, x_sample, cache_k, cache_v, state_h, state_conv, state_pool, norm_rec, w_in_rec, conv_w, conv_b, gate_r_w, gate_r_b, gate_i_w, gate_i_b, rg_lambda, pool_w, pool_scale, w_out_rec, norm_att, w_in_att, w_out_att, norm_final):
    return _step(x_prompt, x_sample, cache_k, cache_v, state_h, state_conv, state_pool, norm_rec, w_in_rec,
                 conv_w, conv_b, gate_r_w, gate_r_b, gate_i_w, gate_i_b, rg_lambda, pool_w, pool_scale,
                 w_out_rec, norm_att, w_in_att, w_out_att, norm_final)
```

```python
import functools

import jax
import jax.numpy as jnp
from jax import lax
from jax.experimental import pallas as pl
from jax.experimental.pallas import tpu as pltpu

F32 = jnp.float32
BF16 = jnp.bfloat16

EPS = 1e-6
RG_C = 8.0
CONV_W = 4
POOL_WINDOWS = (2, 4, 8, 16)
POOL_BUF = max(POOL_WINDOWS) - 1
HEAD_DIM = 128
ATT_SCALE = HEAD_DIM ** -0.5
EXP_UNDERFLOW = -105.0

LANES = 128
SUBLANES = 8
VMEM_LIMIT_CAP = 60000 * 1024
VMEM_SLACK = 8 * 1024 * 1024


def _params(semantics, buffer_bytes):
    limit = min(VMEM_LIMIT_CAP, buffer_bytes + VMEM_SLACK)
    return pltpu.CompilerParams(dimension_semantics=semantics, vmem_limit_bytes=limit)


def _tile(n, pref):
    t = min(n, pref)
    while n % t:
        t //= 2
    return t


def _silu(x):
    half = 0.5 * x
    return half + half * jnp.tanh(half)


LOG2_E = 1.4426950408889634


def _softplus(x):
    return jnp.maximum(x, 0.0) + jnp.log(1.0 + jnp.exp2(jnp.abs(x) * -LOG2_E))


NORM_ROWS = 128


def _normalize_rows(x_ref, g_ref, xn_ref):
    @pl.when(pl.program_id(1) == 0)
    def _():
        g = g_ref[...]

        def body(c, _):
            rows = pl.ds(pl.multiple_of(c * NORM_ROWS, NORM_ROWS), NORM_ROWS)
            x = x_ref[rows, :]
            ms = jnp.mean(x * x, axis=-1, keepdims=True)
            xn_ref[rows, :] = (x * lax.rsqrt(ms + EPS) * g).astype(BF16)
            return 0

        lax.fori_loop(0, x_ref.shape[0] // NORM_ROWS, body, 0)


def _norm_mm_kernel(x_ref, g_ref, w_ref, o_ref, xn_ref):
    _normalize_rows(x_ref, g_ref, xn_ref)
    o_ref[...] = jnp.dot(xn_ref[...], w_ref[...], preferred_element_type=F32)


def _norm_mm_rows_kernel(top_ref, bottom_ref, g_ref, w_ref, o_ref, xn_ref, *, n_top):
    i = pl.program_id(0)
    pl.when(i < n_top)(functools.partial(_normalize_rows, top_ref, g_ref, xn_ref))
    pl.when(i >= n_top)(functools.partial(_normalize_rows, bottom_ref, g_ref, xn_ref))
    o_ref[...] = jnp.dot(xn_ref[...], w_ref[...], preferred_element_type=F32)


def _norm_matmul(x, g, w_all, layer, *, tm_pref=1024, tn_pref=1024):
    m, k = x.shape
    n = w_all.shape[2]
    tm, tn = _tile(m, tm_pref), _tile(n, tn_pref)
    vmem = 2 * tm * k * 4 + tm * k * 2 + 2 * k * tn * 2 + 2 * tm * tn * 4
    return pl.pallas_call(
        _norm_mm_kernel,
        grid=(m // tm, n // tn),
        in_specs=[pl.BlockSpec((tm, k), lambda i, j: (i, 0)),
                  pl.BlockSpec((1, k), lambda i, j: (0, 0)),
                  pl.BlockSpec((None, k, tn), lambda i, j: (layer, 0, j))],
        out_specs=pl.BlockSpec((tm, tn), lambda i, j: (i, j)),
        out_shape=jax.ShapeDtypeStruct((m, n), F32),
        scratch_shapes=[pltpu.VMEM((tm, k), BF16)],
        compiler_params=_params(("arbitrary", "arbitrary"), vmem),
        name="norm_matmul",
    )(x, g.reshape(1, k), w_all)


def _norm_matmul_rows(top, bottom, g, w_all, layer, *, tm_pref=512, tn_pref=1024):
    k = top.shape[1]
    m, n = top.shape[0] + bottom.shape[0], w_all.shape[2]
    tm = _tile(top.shape[0], tm_pref)
    while bottom.shape[0] % tm:
        tm //= 2
    tn = _tile(n, tn_pref)
    n_top = top.shape[0] // tm
    vmem = 4 * tm * k * 4 + tm * k * 2 + 2 * k * tn * 2 + 2 * tm * tn * 4
    return pl.pallas_call(
        functools.partial(_norm_mm_rows_kernel, n_top=n_top),
        grid=(m // tm, n // tn),
        in_specs=[pl.BlockSpec((tm, k), lambda i, j: (jnp.minimum(i, n_top - 1), 0)),
                  pl.BlockSpec((tm, k), lambda i, j: (jnp.maximum(i - n_top, 0), 0)),
                  pl.BlockSpec((1, k), lambda i, j: (0, 0)),
                  pl.BlockSpec((None, k, tn), lambda i, j: (layer, 0, j))],
        out_specs=pl.BlockSpec((tm, tn), lambda i, j: (i, j)),
        out_shape=jax.ShapeDtypeStruct((m, n), F32),
        scratch_shapes=[pltpu.VMEM((tm, k), BF16)],
        compiler_params=_params(("arbitrary", "arbitrary"), vmem),
        name="norm_matmul_rows",
    )(top, bottom, g.reshape(1, k), w_all)


def _norm_mm_att_kernel(x_ref, g_ref, w_ref, *refs, layer, n_top, per):
    qkv_ref, gt_ref, kp_hbm, vp_hbm, ks_hbm, vs_hbm, xn_ref, kv_scr, sem = refs[-9:]
    layers = (layer,) if len(refs) > 9 else range(kp_hbm.shape[0])
    _normalize_rows(x_ref, g_ref, xn_ref)
    i, j = pl.program_id(0), pl.program_id(1)
    tm, tn = qkv_ref.shape
    hpt = tn // HEAD_DIM
    top = i < n_top

    def project():
        return jnp.dot(xn_ref[...], w_ref[...], preferred_element_type=F32)

    def copies(jj, prompt_rows):
        slot = (jj - per) % 2
        dst = ((kp_hbm, ks_hbm), (vp_hbm, vs_hbm))[jj // per - 1][0 if prompt_rows else 1]
        rows = pl.ds(i * tm if prompt_rows else (i - n_top) * tm, tm)
        return [pltpu.make_async_copy(kv_scr.at[slot, :, pl.ds(h * HEAD_DIM, HEAD_DIM)],
                                      dst.at[l, rows, (jj % per) * hpt + h, :], sem.at[slot])
                for l in layers for h in range(hpt)]

    def for_rows(jj, action):
        @pl.when(top)
        def _():
            for c in copies(jj, True):
                action(c)

        @pl.when(jnp.logical_not(top))
        def _():
            for c in copies(jj, False):
                action(c)

    @pl.when(j < per)
    def _():
        qkv_ref[...] = (project() * ATT_SCALE).astype(BF16)

    for jj in range(per, 3 * per + 2):
        @pl.when(j == jj)
        def _(jj=jj):
            if jj - 2 >= per:
                for_rows(jj - 2, lambda c: c.wait())
            if jj < 3 * per:
                slot = (jj - per) % 2
                kv_scr[slot] = project()
                qkv_ref[...] = kv_scr[slot].astype(BF16)
                for_rows(jj, lambda c: c.start())

    @pl.when(j >= 3 * per)
    def _():
        gt_ref[...] = project()


def _norm_matmul_att(x, g, w_all, new_kv, *, layer, mp, n_heads, tm_pref=1024):
    m, k = x.shape
    n_layers = w_all.shape[0]
    d_att = n_heads * HEAD_DIM
    tn = d_att // 2
    per = d_att // tn
    assert per >= 2 and w_all.shape[2] == 4 * d_att
    tm = _tile(mp, tm_pref)
    while (m - mp) % tm:
        tm //= 2
    any_spec = pl.BlockSpec(memory_space=pl.ANY)
    carried = () if new_kv is None else tuple(new_kv)
    kv_shapes = [jax.ShapeDtypeStruct((n_layers, rows, n_heads, HEAD_DIM), F32) for rows in (mp, mp, m - mp, m - mp)]
    vmem = 2 * tm * k * 4 + tm * k * 2 + 2 * k * tn * 2 + 2 * tm * tn * (2 + 4) + 3 * tm * tn * 4
    outs = pl.pallas_call(
        functools.partial(_norm_mm_att_kernel, layer=layer, n_top=mp // tm, per=per),
        grid=(m // tm, 4 * per),
        in_specs=[pl.BlockSpec((tm, k), lambda i, j: (i, 0)),
                  pl.BlockSpec((1, k), lambda i, j: (0, 0)),
                  pl.BlockSpec((None, k, tn), lambda i, j: (layer, 0, j))] + [any_spec] * len(carried),
        out_specs=[pl.BlockSpec((tm, tn), lambda i, j: (i, jnp.minimum(j, 3 * per - 1))),
                   pl.BlockSpec((tm, tn), lambda i, j: (i, jnp.maximum(j - 3 * per, 0))),
                   any_spec, any_spec, any_spec, any_spec],
        out_shape=[jax.ShapeDtypeStruct((m, 3 * d_att), BF16), jax.ShapeDtypeStruct((m, d_att), F32)] + kv_shapes,
        input_output_aliases={3 + n: 2 + n for n in range(len(carried))},
        scratch_shapes=[pltpu.VMEM((tm, k), BF16), pltpu.VMEM((2, tm, tn), F32),
                        pltpu.SemaphoreType.DMA((2,))],
        compiler_params=_params(("arbitrary", "arbitrary"), vmem),
        name="norm_matmul_att",
    )(x, g.reshape(1, k), w_all, *carried)
    return outs[0], outs[1], tuple(outs[2:])


def _mm_res_kernel(*refs, n_pairs, n_top):
    lhs, ws = refs[:n_pairs], refs[n_pairs:2 * n_pairs]
    res_refs, out_ref = refs[2 * n_pairs:-1], refs[-1]

    def emit(res_ref):
        acc = res_ref[...]
        for l, w in zip(lhs, ws):
            acc = acc + jnp.dot(l[...], w[...], preferred_element_type=F32)
        out_ref[...] = acc

    if len(res_refs) == 1:
        emit(res_refs[0])
    else:
        i = pl.program_id(0)
        pl.when(i < n_top)(functools.partial(emit, res_refs[0]))
        pl.when(i >= n_top)(functools.partial(emit, res_refs[1]))


def _matmul_residual(lhs_list, w_all, layer, res, *, tm_pref=1024, tn_pref=1024):
    parts = res if isinstance(res, tuple) else (res,)
    m, n = sum(p.shape[0] for p in parts), parts[0].shape[1]
    tm = _tile(parts[0].shape[0], tm_pref)
    while any(p.shape[0] % tm for p in parts):
        tm //= 2
    tn = _tile(n, tn_pref)
    n_top = parts[0].shape[0] // tm
    ks = [l.shape[1] for l in lhs_list]
    starts = [sum(ks[:p]) for p in range(len(ks))]
    assert all(r % k == 0 for r, k in zip(starts, ks)) and sum(ks) == w_all.shape[1]
    vmem = sum(2 * tm * k * 2 + 2 * k * tn * 2 for k in ks) + (2 + 2 * len(parts)) * tm * tn * 4
    res_specs = ([pl.BlockSpec((tm, tn), lambda i, j: (i, j))] if len(parts) == 1 else
                 [pl.BlockSpec((tm, tn), lambda i, j: (jnp.minimum(i, n_top - 1), j)),
                  pl.BlockSpec((tm, tn), lambda i, j: (jnp.maximum(i - n_top, 0), j))])
    in_specs = ([pl.BlockSpec((tm, k), lambda i, j: (i, 0)) for k in ks]
                + [pl.BlockSpec((None, k, tn), lambda i, j, blk=r // k: (layer, blk, j)) for r, k in zip(starts, ks)]
                + res_specs)
    return pl.pallas_call(
        functools.partial(_mm_res_kernel, n_pairs=len(ks), n_top=n_top),
        grid=(m // tm, n // tn),
        in_specs=in_specs,
        out_specs=pl.BlockSpec((tm, tn), lambda i, j: (i, j)),
        out_shape=jax.ShapeDtypeStruct((m, n), F32),
        compiler_params=_params(("arbitrary", "arbitrary"), vmem),
        name="matmul_residual",
    )(*lhs_list, *([w_all] * len(ks)), *parts)


def _mm_res_rows_kernel(top_ref, bottom_ref, w_ref, res_ref, out_ref, *, n_top):
    i = pl.program_id(0)

    def emit(lhs_ref):
        out_ref[...] = res_ref[...] + jnp.dot(lhs_ref[...], w_ref[...], preferred_element_type=F32)

    pl.when(i < n_top)(functools.partial(emit, top_ref))
    pl.when(i >= n_top)(functools.partial(emit, bottom_ref))


def _mm_res_rows_norm_kernel(top_ref, bottom_ref, w_ref, res_ref, g_ref, top_out, bottom_out, *, n_top):
    i = pl.program_id(0)

    def emit(lhs_ref, out_ref):
        y = res_ref[...] + jnp.dot(lhs_ref[...], w_ref[...], preferred_element_type=F32)
        ms = jnp.mean(y * y, axis=-1, keepdims=True)
        out_ref[...] = y * lax.rsqrt(ms + EPS) * g_ref[...]

    pl.when(i < n_top)(functools.partial(emit, top_ref, top_out))
    pl.when(i >= n_top)(functools.partial(emit, bottom_ref, bottom_out))


def _matmul_residual_rows_norm(top, bottom, w_all, layer, res, g, *, tm_pref=512):
    m, n = res.shape
    k = w_all.shape[1]
    tm = _tile(top.shape[0], tm_pref)
    while bottom.shape[0] % tm:
        tm //= 2
    n_top = top.shape[0] // tm
    vmem = 4 * tm * k * 2 + 2 * k * n * 2 + 6 * tm * n * 4
    return pl.pallas_call(
        functools.partial(_mm_res_rows_norm_kernel, n_top=n_top),
        grid=(m // tm,),
        in_specs=[pl.BlockSpec((tm, k), lambda i: (jnp.minimum(i, n_top - 1), 0)),
                  pl.BlockSpec((tm, k), lambda i: (jnp.maximum(i - n_top, 0), 0)),
                  pl.BlockSpec((None, k, n), lambda i: (layer, 0, 0)),
                  pl.BlockSpec((tm, n), lambda i: (i, 0)),
                  pl.BlockSpec((1, n), lambda i: (0, 0))],
        out_specs=[pl.BlockSpec((tm, n), lambda i: (jnp.minimum(i, n_top - 1), 0)),
                   pl.BlockSpec((tm, n), lambda i: (jnp.maximum(i - n_top, 0), 0))],
        out_shape=[jax.ShapeDtypeStruct((top.shape[0], n), F32), jax.ShapeDtypeStruct((bottom.shape[0], n), F32)],
        compiler_params=_params(("arbitrary",), vmem),
        name="matmul_residual_rows_norm",
    )(top, bottom, w_all, res, g.reshape(1, n))


def _matmul_residual_rows(top, bottom, w_all, layer, res, *, tm_pref=1024, tn_pref=1024):
    m, n = res.shape
    k = w_all.shape[1]
    tm = _tile(top.shape[0], tm_pref)
    while bottom.shape[0] % tm:
        tm //= 2
    tn = _tile(n, tn_pref)
    n_top = top.shape[0] // tm
    vmem = 4 * tm * k * 2 + 2 * k * tn * 2 + 4 * tm * tn * 4
    return pl.pallas_call(
        functools.partial(_mm_res_rows_kernel, n_top=n_top),
        grid=(m // tm, n // tn),
        in_specs=[pl.BlockSpec((tm, k), lambda i, j: (jnp.minimum(i, n_top - 1), 0)),
                  pl.BlockSpec((tm, k), lambda i, j: (jnp.maximum(i - n_top, 0), 0)),
                  pl.BlockSpec((None, k, tn), lambda i, j: (layer, 0, j)),
                  pl.BlockSpec((tm, tn), lambda i, j: (i, j))],
        out_specs=pl.BlockSpec((tm, tn), lambda i, j: (i, j)),
        out_shape=jax.ShapeDtypeStruct((m, n), F32),
        compiler_params=_params(("arbitrary", "arbitrary"), vmem),
        name="matmul_residual_rows",
    )(top, bottom, w_all, res)


def _final_norm_kernel(x_ref, g_ref, top_ref, bottom_ref, *, n_top):
    x = x_ref[...]
    ms = jnp.mean(x * x, axis=-1, keepdims=True)
    y = x * lax.rsqrt(ms + EPS) * g_ref[...]
    i = pl.program_id(0)

    @pl.when(i < n_top)
    def _():
        top_ref[...] = y

    @pl.when(i >= n_top)
    def _():
        bottom_ref[...] = y


def _final_norm(x, g, mp):
    m, d = x.shape
    tm = _tile(mp, 256)
    while (m - mp) % tm:
        tm //= 2
    n_top = mp // tm
    return pl.pallas_call(
        functools.partial(_final_norm_kernel, n_top=n_top),
        grid=(m // tm,),
        in_specs=[pl.BlockSpec((tm, d), lambda i: (i, 0)), pl.BlockSpec((1, d), lambda i: (0, 0))],
        out_specs=[pl.BlockSpec((tm, d), lambda i: (jnp.minimum(i, n_top - 1), 0)),
                   pl.BlockSpec((tm, d), lambda i: (jnp.maximum(i - n_top, 0), 0))],
        out_shape=[jax.ShapeDtypeStruct((mp, d), F32), jax.ShapeDtypeStruct((m - mp, d), F32)],
        compiler_params=_params(("arbitrary",), 6 * tm * d * 4),
        name="final_norm",
    )(x, g.reshape(1, d))


SCAN_ROWS = 256
SCAN_STREAMS = 4


def _scan_tiles(a, u, carry):
    r = a.shape[0]
    nt = r // SUBLANES
    a3 = a.reshape(nt, SUBLANES, LANES)
    u3 = u.reshape(nt, SUBLANES, LANES)
    sub = lax.broadcasted_iota(jnp.int32, a3.shape, 1)
    for d in (1, 2, 4):
        keep = sub >= d
        a_prev = pltpu.roll(a3, d, 1)
        u_prev = pltpu.roll(u3, d, 1)
        u3 = jnp.where(keep, a3 * u_prev + u3, u3)
        a3 = jnp.where(keep, a3 * a_prev, a3)
    hs = []
    for t in range(nt):
        h_t = u3[t] + a3[t] * carry
        carry = h_t[SUBLANES - 1:SUBLANES, :]
        hs.append(h_t)
    return jnp.concatenate(hs, axis=0), carry


def _rglru_kernel(xa_ref, ga_ref, h0_ref, sc_ref, cw_ref, cb_ref, wr_ref, br_ref, wi_ref, bi_ref, lam_ref,
                  out_ref, hp_ref, hs_ref, *, bp, t, bs, ts):
    mp = bp * t
    cw = cw_ref[...]
    cb = cb_ref[...]
    half_wr = (0.5 * wr_ref[...]).astype(BF16)
    half_wi = (0.5 * wi_ref[...]).astype(BF16)
    half_br = 0.5 * br_ref[...]
    half_bi = 0.5 * bi_ref[...]
    half_c = (-0.5 * RG_C) * _softplus(-lam_ref[...])

    def conv(ext, n):
        def tap(k):
            back = CONV_W - 1 - k
            rows = ext if back == 0 else pltpu.roll(ext, back, 0)
            return rows[SUBLANES:] * cw[k:k + 1]

        y = cb + tap(0)
        for k in range(1, CONV_W):
            y = y + tap(k)
        return y

    def decay_and_input(xc):
        xb = xc.astype(BF16)
        tr = jnp.tanh(jnp.dot(xb, half_wr, preferred_element_type=F32) + half_br)
        ti = jnp.tanh(jnp.dot(xb, half_wi, preferred_element_type=F32) + half_bi)
        log_a = half_c + half_c * tr
        half_x = 0.5 * xc
        gated_x = half_x + half_x * ti
        a = jnp.exp(log_a)
        v = -jnp.tanh(log_a) * (1.0 + a * a)
        root = jnp.where(v > 0.0, v * lax.rsqrt(v), 0.0)
        return a, root * gated_x

    def emit(rows, a, u, carry):
        h, carry = _scan_tiles(a, u, carry)
        out_ref[rows, :] = (h * _silu(ga_ref[rows, :])).astype(out_ref.dtype)
        return carry

    rp = _tile(t, SCAN_ROWS)

    def prompt_chunk(c, state):
        new = []
        for b in range(bp):
            tail, carry = state[b]
            rows = pl.ds(pl.multiple_of(b * t + c * rp, SUBLANES), rp)
            x = xa_ref[rows, :]
            a, u = decay_and_input(conv(jnp.concatenate([tail, x], axis=0), rp))
            new.append((x[rp - SUBLANES:, :], emit(rows, a, u, carry)))
        return tuple(new)

    start = (jnp.zeros((SUBLANES, LANES), F32), jnp.zeros((1, LANES), F32))
    state = lax.fori_loop(0, t // rp, prompt_chunk, (start,) * bp)
    for b in range(bp):
        hp_ref[b:b + 1, :] = state[b][1]

    ns = _tile(bs, SCAN_STREAMS)

    def sample_group(c, _):
        rows = [pl.ds(pl.multiple_of(mp + (c * ns + k) * ts, SUBLANES), ts) for k in range(ns)]
        xc = [conv(jnp.concatenate([sc_ref[c * ns + k], xa_ref[rows[k], :]], axis=0), ts) for k in range(ns)]
        a, u = decay_and_input(jnp.concatenate(xc, axis=0))
        for k in range(ns):
            seg = slice(k * ts, (k + 1) * ts)
            s = c * ns + k
            hs_ref[pl.ds(s, 1), :] = emit(rows[k], a[seg], u[seg], h0_ref[pl.ds(s, 1), :])
        return 0

    lax.fori_loop(0, bs // ns, sample_group, 0)


def _rglru(proj, h0, sc_pad, cw, cb, wr, br, wi, bi, lam, *, bp, t, bs, ts, d_rnn):
    m = proj.shape[0]
    nb = d_rnn // LANES
    row = lambda v: v.reshape(1, d_rnn)
    vec_spec = pl.BlockSpec((1, LANES), lambda n: (0, n))
    w_spec = pl.BlockSpec((None, LANES, LANES), lambda n: (n, 0, 0))
    vmem = 2 * (2 * m * LANES * 4 + m * LANES * 2)
    return pl.pallas_call(
        functools.partial(_rglru_kernel, bp=bp, t=t, bs=bs, ts=ts),
        grid=(nb,),
        in_specs=[pl.BlockSpec((m, LANES), lambda n: (0, n)),
                  pl.BlockSpec((m, LANES), lambda n: (0, nb + n)),
                  pl.BlockSpec((bs, LANES), lambda n: (0, n)),
                  pl.BlockSpec((bs, SUBLANES, LANES), lambda n: (0, 0, n)),
                  pl.BlockSpec((CONV_W, LANES), lambda n: (0, n)),
                  vec_spec, w_spec, vec_spec, w_spec, vec_spec, vec_spec],
        out_specs=[pl.BlockSpec((m, LANES), lambda n: (0, n)),
                   pl.BlockSpec((bp, LANES), lambda n: (0, n)),
                   pl.BlockSpec((bs, LANES), lambda n: (0, n))],
        out_shape=[jax.ShapeDtypeStruct((m, d_rnn), BF16),
                   jax.ShapeDtypeStruct((bp, d_rnn), F32),
                   jax.ShapeDtypeStruct((bs, d_rnn), F32)],
        compiler_params=_params(("arbitrary",), vmem),
        name="rglru_mixer",
    )(proj, proj, h0, sc_pad, cw, row(cb), wr, row(br), wi, row(bi), row(lam))


POOL_TAIL = 16
POOL_ROWS = 128


def _pool_kernel(xb_ref, gb_ref, sp_ref, pw_ref, ps_ref, out_ref, m_scr, tail_scr, *, npb, bpt, ts):
    g = pl.program_id(0)
    rb = pl.program_id(1)
    rbk, gw = xb_ref.shape
    wf = lax.shift_left(jnp.int32(2), g).astype(F32)

    def window_means(ext, n, pos0):
        s2 = ext[1:] + ext[:-1]
        s4 = s2[2:] + s2[:-2]
        s8 = s4[4:] + s4[:-4]
        s16 = s8[8:] + s8[:-8]
        x = ext[POOL_TAIL:]
        win = jnp.where(g == 0, s2[POOL_TAIL - 1:],
                        jnp.where(g == 1, s4[POOL_TAIL - 3:],
                                  jnp.where(g == 2, s8[POOL_TAIL - 7:], s16[POOL_TAIL - 15:])))
        if pos0 is None:
            cnt = wf
        else:
            pos = pos0 + lax.broadcasted_iota(jnp.int32, (n, gw), 0)
            cnt = jnp.minimum(wf, (pos + 1).astype(F32))
        return win / cnt - x

    @pl.when(rb < npb)
    def _():
        blk = rb % bpt

        @pl.when(blk == 0)
        def _():
            tail_scr[...] = jnp.zeros_like(tail_scr)

        rc = _tile(rbk, POOL_ROWS)

        def body(c, tail):
            rows = pl.ds(pl.multiple_of(c * rc, SUBLANES), rc)
            x = xb_ref[rows, :]
            ext = jnp.concatenate([tail, x], axis=0)
            m_scr[rows, :] = window_means(ext, rc, blk * rbk + c * rc).astype(BF16)
            return x[rc - POOL_TAIL:, :]

        tail_scr[...] = lax.fori_loop(0, rbk // rc, body, tail_scr[...])

    @pl.when(rb >= npb)
    def _():
        s0 = (rb - npb) * (rbk // ts)

        def body(s, _):
            rows = pl.ds(pl.multiple_of(s * ts, SUBLANES), ts)
            ext = jnp.concatenate([sp_ref[s0 + s], xb_ref[rows, :]], axis=0)
            m_scr[rows, :] = window_means(ext, ts, None).astype(BF16)
            return 0

        lax.fori_loop(0, rbk // ts, body, 0)

    y = jnp.dot(m_scr[...], pw_ref[...].astype(BF16), preferred_element_type=F32) * ps_ref[...]
    out_ref[...] = (y * _silu(gb_ref[...])).astype(out_ref.dtype)


def _pool(proj, sp_pad, pw, ps, *, bp, t, bs, ts, d_rnn, d_pool):
    m = proj.shape[0]
    ng = len(POOL_WINDOWS)
    gw = d_pool // ng
    ms = bs * ts
    rbk = min(1024, t, ms)
    while t % rbk or ms % rbk:
        rbk //= 2
    npb = bp * t // rbk
    xcol = 2 * d_rnn // gw
    gcol = (2 * d_rnn + d_pool) // gw
    vmem = 2 * (2 * rbk * gw * 4 + rbk * gw * 2) + rbk * gw * 2 + 2 * bs * POOL_TAIL * gw * 4 + 2 * gw * gw * 4
    return pl.pallas_call(
        functools.partial(_pool_kernel, npb=npb, bpt=t // rbk, ts=ts),
        grid=(ng, m // rbk),
        in_specs=[pl.BlockSpec((rbk, gw), lambda g, r: (r, xcol + g)),
                  pl.BlockSpec((rbk, gw), lambda g, r: (r, gcol + g)),
                  pl.BlockSpec((bs, POOL_TAIL, gw), lambda g, r: (0, 0, g)),
                  pl.BlockSpec((None, gw, gw), lambda g, r: (g, 0, 0)),
                  pl.BlockSpec((1, gw), lambda g, r: (0, g))],
        out_specs=pl.BlockSpec((rbk, gw), lambda g, r: (r, g)),
        out_shape=jax.ShapeDtypeStruct((m, d_pool), BF16),
        scratch_shapes=[pltpu.VMEM((rbk, gw), BF16), pltpu.VMEM((POOL_TAIL, gw), F32)],
        compiler_params=_params(("arbitrary", "arbitrary"), vmem),
        name="pool_mixer",
    )(proj, proj, sp_pad, pw, ps.reshape(1, d_pool))


def _suffix_matrix(n):
    j = lax.broadcasted_iota(jnp.int32, (n, n), 0)
    s = lax.broadcasted_iota(jnp.int32, (n, n), 1)
    u = jnp.where(j > s, -1.0, 0.0).astype(BF16)
    return jnp.concatenate([u, u], axis=0)


def _qk(q, k):
    return lax.dot_general(q, k, (((1,), (1,)), ((), ())), preferred_element_type=F32)


def _strictly_earlier(shape):
    return lax.broadcasted_iota(jnp.int32, shape, 1) < lax.broadcasted_iota(jnp.int32, shape, 0)


def _sb_weights(z, uu, carry=None, mask=None):
    sp = _softplus(z)
    spm = sp if mask is None else jnp.where(mask, sp, 0.0)
    hi = spm.astype(BF16)
    lo = (spm - hi.astype(F32)).astype(BF16)
    e = z - sp + jnp.dot(jnp.concatenate([hi, lo], axis=1), uu, preferred_element_type=F32)
    w = jnp.exp(e if carry is None else e + carry)
    if mask is not None:
        w = jnp.where(mask, w, 0.0)
    return w.astype(BF16), -jnp.sum(spm, axis=1, keepdims=True)


def _sb_tile(q, k, v, uu, carry=None, mask=None):
    w, tot = _sb_weights(_qk(q, k), uu, carry, mask)
    return jnp.dot(w, v, preferred_element_type=F32), tot


PROMPT_TILE = 256
PROMPT_ALWAYS = 2


def _attn_prompt_kernel(q_ref, k_ref, v_ref, gt_ref, uu_ref, o_ref, acc_scr, car_scr, worst_ref, *, tq):
    uu = uu_ref[...]
    mask = _strictly_earlier((tq, tq))
    nq = q_ref.shape[0] // tq

    for qi in range(nq):
        rows = pl.ds(qi * tq, tq)
        q = q_ref[rows, :]
        acc, carry = _sb_tile(q, k_ref[rows, :], v_ref[rows, :], uu, None, mask)
        for back in range(1, min(qi, PROMPT_ALWAYS - 1) + 1):
            prev = pl.ds((qi - back) * tq, tq)
            pv, tot = _sb_tile(q, k_ref[prev, :], v_ref[prev, :], uu, carry)
            acc, carry = acc + pv, carry + tot
        o_ref[rows, :] = (acc * _silu(gt_ref[rows, :])).astype(o_ref.dtype)
        if qi >= PROMPT_ALWAYS:
            acc_scr[qi] = acc
            car_scr[qi] = carry
            worst_ref[qi] = jnp.max(carry)

    def finish(qi, _):
        @pl.when(worst_ref[qi] >= EXP_UNDERFLOW)
        def _():
            rows = pl.ds(pl.multiple_of(qi * tq, tq), tq)
            q = q_ref[rows, :]

            def more(s):
                return jnp.logical_and(s[0] >= 0, jnp.max(s[2]) >= EXP_UNDERFLOW)

            def older(s):
                j, acc, carry = s
                old = pl.ds(pl.multiple_of(j * tq, tq), tq)
                pv, tot = _sb_tile(q, k_ref[old, :], v_ref[old, :], uu, carry)
                return j - 1, acc + pv, carry + tot

            _, acc, _ = lax.while_loop(more, older, (qi - PROMPT_ALWAYS, acc_scr[qi], car_scr[qi]))
            o_ref[rows, :] = (acc * _silu(gt_ref[rows, :])).astype(o_ref.dtype)

        return 0

    lax.fori_loop(PROMPT_ALWAYS, nq, finish, 0)


def _attn_prompt(qkv, gate, *, bp, t, n_heads):
    tq = _tile(t, PROMPT_TILE)
    d_att = n_heads * HEAD_DIM
    nq = t // tq
    state = pltpu.VMEM((nq, tq, LANES), F32), pltpu.VMEM((nq, tq, 1), F32), pltpu.SMEM((nq,), F32)
    vmem = 2 * (3 * t * LANES * 2 + t * LANES * 4 + 2 * tq * tq * 2 + t * LANES * 2) + 2 * t * LANES * 4
    col = lambda c: pl.BlockSpec((t, LANES), lambda b, h: (b, c * n_heads + h))
    return pl.pallas_call(
        functools.partial(_attn_prompt_kernel, tq=tq),
        grid=(bp, n_heads),
        in_specs=[col(0), col(1), col(2), col(0), pl.BlockSpec((2 * tq, tq), lambda b, h: (0, 0))],
        out_specs=pl.BlockSpec((t, LANES), lambda b, h: (b, h)),
        out_shape=jax.ShapeDtypeStruct((bp * t, d_att), BF16),
        scratch_shapes=list(state),
        compiler_params=_params(("arbitrary", "arbitrary"), vmem),
        name="attn_prompt",
    )(qkv, qkv, qkv, gate, _suffix_matrix(tq))


def _attn_sample_kernel(q_ref, kn_ref, vn_ref, gt_ref, kc_hbm, vc_hbm, un_ref, uc_ref, o_ref,
                        kfirst, vfirst, kmore, vmore, acc_scr, car_scr, sem, *, layer, pc):
    b = pl.program_id(0)
    n_heads, ts = kmore.shape[0], q_ref.shape[0]
    newest = kc_hbm.shape[2] // pc - 1
    slot = b % 2

    def copies(stream, chunk, kdst, vdst, ksem, vsem):
        pos = pl.ds(chunk * pc, pc)
        out = []
        for h in range(n_heads):
            out.append(pltpu.make_async_copy(kc_hbm.at[layer, stream, pos, h, :], kdst.at[h], ksem))
            out.append(pltpu.make_async_copy(vc_hbm.at[layer, stream, pos, h, :], vdst.at[h], vsem))
        return out

    def first_copies(stream, s):
        return copies(stream, newest, kfirst.at[s], vfirst.at[s], sem.at[s, 0], sem.at[s, 1])

    @pl.when(b == 0)
    def _():
        for c in first_copies(0, 0):
            c.start()

    @pl.when(b + 1 < pl.num_programs(0))
    def _():
        for c in first_copies(b + 1, 1 - slot):
            c.start()

    def head(ref, h):
        return ref[:, h * HEAD_DIM:(h + 1) * HEAD_DIM]

    def scores(keys):
        return jnp.concatenate([_qk(head(q_ref, h), keys(h)) for h in range(n_heads)], axis=0)

    def weighted(w, values):
        return jnp.concatenate([jnp.dot(w[h * ts:(h + 1) * ts], values(h), preferred_element_type=F32)
                                for h in range(n_heads)], axis=0)

    pad = jnp.zeros((LANES - ts, HEAD_DIM), BF16)
    query = lax.broadcasted_iota(jnp.int32, (n_heads, ts, LANES), 1).reshape(n_heads * ts, LANES)
    mask = lax.broadcasted_iota(jnp.int32, (n_heads * ts, LANES), 1) < query
    w, carry = _sb_weights(scores(lambda h: jnp.concatenate([head(kn_ref, h), pad], axis=0)),
                           un_ref[...], None, mask)
    acc = weighted(w, lambda h: jnp.concatenate([head(vn_ref, h), pad], axis=0))

    def chunk(kbuf, vbuf, acc, carry):
        w, tot = _sb_weights(scores(lambda h: kbuf[h].astype(BF16)), uc_ref[...], carry)
        return acc + weighted(w, lambda h: vbuf[h].astype(BF16)), carry + tot

    for c in first_copies(b, slot):
        c.wait()
    acc_scr[...], car_scr[...] = chunk(kfirst.at[slot], vfirst.at[slot], acc, carry)

    def more(s):
        return jnp.logical_and(s[0] >= 0, s[1] >= EXP_UNDERFLOW)

    def older(s):
        cs = copies(b, s[0], kmore, vmore, sem.at[2, 0], sem.at[2, 1])
        for c in cs:
            c.start()
        for c in cs:
            c.wait()
        acc_scr[...], car_scr[...] = chunk(kmore, vmore, acc_scr[...], car_scr[...])
        return s[0] - 1, jnp.max(car_scr[...])

    lax.while_loop(more, older, (newest - 1, jnp.max(car_scr[...])))

    for h in range(n_heads):
        o_ref[:, h * HEAD_DIM:(h + 1) * HEAD_DIM] = (acc_scr[pl.ds(h * ts, ts), :]
                                                     * _silu(head(gt_ref, h))).astype(o_ref.dtype)


def _attn_sample(qkv, gate, cache_k, cache_v, *, layer, mp, bs, ts, n_heads):
    p = cache_k.shape[2]
    d_att = n_heads * HEAD_DIM
    pc = _tile(p, 256)
    r0 = mp // ts
    chunk_bytes = n_heads * pc * HEAD_DIM * 4
    vmem = (2 * (3 * ts * d_att * 2 + ts * d_att * 4 + ts * d_att * 2) + 6 * chunk_bytes
            + 2 * n_heads * ts * LANES * 4 + 2 * 2 * (LANES * LANES + pc * pc) * 2)
    new_spec = lambda c: pl.BlockSpec((ts, d_att), lambda b: (r0 + b, c))
    chunk_buf = lambda n: pltpu.VMEM(n + (n_heads, pc, HEAD_DIM), F32)
    return pl.pallas_call(
        functools.partial(_attn_sample_kernel, layer=layer, pc=pc),
        grid=(bs,),
        in_specs=[new_spec(0), new_spec(1), new_spec(2), new_spec(0),
                  pl.BlockSpec(memory_space=pl.ANY), pl.BlockSpec(memory_space=pl.ANY),
                  pl.BlockSpec((2 * LANES, LANES), lambda b: (0, 0)),
                  pl.BlockSpec((2 * pc, pc), lambda b: (0, 0))],
        out_specs=pl.BlockSpec((ts, d_att), lambda b: (b, 0)),
        out_shape=jax.ShapeDtypeStruct((bs * ts, d_att), BF16),
        scratch_shapes=[chunk_buf((2,)), chunk_buf((2,)), chunk_buf(()), chunk_buf(()),
                        pltpu.VMEM((n_heads * ts, LANES), F32), pltpu.VMEM((n_heads * ts, 1), F32),
                        pltpu.SemaphoreType.DMA((3, 2))],
        compiler_params=_params(("arbitrary",), vmem),
        name="attn_sample",
    )(qkv, qkv, qkv, gate, cache_k, cache_v, _suffix_matrix(LANES), _suffix_matrix(pc))


@jax.jit
def _step(x_prompt, x_sample, cache_k, cache_v, state_h, state_conv, state_pool,
          norm_rec, w_in_rec, conv_w, conv_b, gate_r_w, gate_r_b, gate_i_w, gate_i_b, rg_lambda,
          pool_w, pool_scale, w_out_rec, norm_att, w_in_att, w_out_att, norm_final):
    bp, t, d = x_prompt.shape
    bs, ts, _ = x_sample.shape
    n_rec, n_att = norm_rec.shape[0], norm_att.shape[0]
    d_rnn = state_h.shape[-1]
    d_pool = state_pool.shape[-1]
    n_heads = cache_k.shape[3]
    mp = bp * t
    assert ts >= POOL_BUF and ts % SUBLANES == 0 and t % ts == 0 and cache_k.shape[4] == HEAD_DIM
    assert gate_r_w.shape[2] == LANES and d_pool // len(POOL_WINDOWS) == 2 * LANES

    x = (x_prompt.reshape(mp, d), x_sample.reshape(bs * ts, d))
    sc_pad = jnp.pad(state_conv, ((0, 0), (0, 0), (SUBLANES - (CONV_W - 1), 0), (0, 0)))
    sp_pad = jnp.pad(state_pool, ((0, 0), (0, 0), (POOL_TAIL - POOL_BUF, 0), (0, 0)))
    w_in_rec, w_out_rec, w_in_att, w_out_att = (w.astype(BF16) for w in (w_in_rec, w_out_rec, w_in_att, w_out_att))

    outs = {k: [] for k in ("hp", "cp", "pp", "hs", "cs", "ps")}
    new_kv = None
    for layer in range(n_rec + n_att):
        j = layer // 2
        if layer % 2 == 0:
            proj = (_norm_matmul_rows(*x, norm_rec[j], w_in_rec, j) if isinstance(x, tuple)
                    else _norm_matmul(x, norm_rec[j], w_in_rec, j))
            ya, hp, hs = _rglru(proj, state_h[j], sc_pad[j], conv_w[j], conv_b[j], gate_r_w[j], gate_r_b[j],
                                gate_i_w[j], gate_i_b[j], rg_lambda[j], bp=bp, t=t, bs=bs, ts=ts, d_rnn=d_rnn)
            yb = _pool(proj, sp_pad[j], pool_w[j], pool_scale[j], bp=bp, t=t, bs=bs, ts=ts,
                       d_rnn=d_rnn, d_pool=d_pool)
            x = _matmul_residual([ya, yb], w_out_rec, j, x)
            frames = proj.reshape(-1, ts, proj.shape[1])

            def last_rows(n, c0, c1):
                prompt = lax.slice(frames, (t // ts - 1, ts - n, c0), (mp // ts, ts, c1), (t // ts, 1, 1))
                sample = lax.slice(frames, (mp // ts, ts - n, c0), (frames.shape[0], ts, c1))
                return prompt, sample

            cp, cs = last_rows(CONV_W - 1, 0, d_rnn)
            pp, ps = last_rows(POOL_BUF, 2 * d_rnn, 2 * d_rnn + d_pool)
            for key, val in (("hp", hp), ("hs", hs), ("cp", cp), ("cs", cs), ("pp", pp), ("ps", ps)):
                outs[key].append(val)
        else:
            qkv, gate, new_kv = _norm_matmul_att(x, norm_att[j], w_in_att, new_kv, layer=j, mp=mp, n_heads=n_heads)
            op = _attn_prompt(qkv, gate, bp=bp, t=t, n_heads=n_heads)
            os_ = _attn_sample(qkv, gate, cache_k, cache_v, layer=j, mp=mp, bs=bs, ts=ts, n_heads=n_heads)
            if layer + 1 < n_rec + n_att:
                x = _matmul_residual_rows(op, os_, w_out_att, j, x)
            else:
                y_prompt, y_sample = _matmul_residual_rows_norm(op, os_, w_out_att, j, x, norm_final)

    if (n_rec + n_att) % 2:
        y_prompt, y_sample = _final_norm(x, norm_final, mp)
    st = {k: jnp.stack(v) for k, v in outs.items()}
    kp, vp, ks, vs = new_kv
    prompt_shape = (n_att, bp, t, n_heads, HEAD_DIM)
    sample_shape = (n_att, bs, ts, n_heads, HEAD_DIM)
    return (y_prompt.reshape(bp, t, d), y_sample.reshape(bs, ts, d),
            kp.reshape(prompt_shape), vp.reshape(prompt_shape), st["hp"], st["cp"], st["pp"],
            ks.reshape(sample_shape), vs.reshape(sample_shape), st["hs"], st["cs"], st["ps"])


def kernel(x_prompt, x_sample, cache_k, cache_v, state_h, state_conv, state_pool, norm_rec, w_in_rec, conv_w, conv_b, gate_r_w, gate_r_b, gate_i_w, gate_i_b, rg_lambda, pool_w, pool_scale, w_out_rec, norm_att, w_in_att, w_out_att, norm_final):
    return _step(x_prompt, x_sample, cache_k, cache_v, state_h, state_conv, state_pool, norm_rec, w_in_rec,
                 conv_w, conv_b, gate_r_w, gate_r_b, gate_i_w, gate_i_b, rg_lambda, pool_w, pool_scale,
                 w_out_rec, norm_att, w_in_att, w_out_att, norm_final)
```

```python
import functools

import jax
import jax.numpy as jnp
from jax import lax
from jax.experimental import pallas as pl
from jax.experimental.pallas import tpu as pltpu

F32 = jnp.float32
BF16 = jnp.bfloat16

EPS = 1e-6
RG_C = 8.0
CONV_W = 4
POOL_WINDOWS = (2, 4, 8, 16)
POOL_BUF = max(POOL_WINDOWS) - 1
HEAD_DIM = 128
ATT_SCALE = HEAD_DIM ** -0.5
EXP_UNDERFLOW = -105.0

LANES = 128
SUBLANES = 8
VMEM_LIMIT_CAP = 60000 * 1024
VMEM_SLACK = 8 * 1024 * 1024


def _params(semantics, buffer_bytes):
    limit = min(VMEM_LIMIT_CAP, buffer_bytes + VMEM_SLACK)
    return pltpu.CompilerParams(dimension_semantics=semantics, vmem_limit_bytes=limit)


def _tile(n, pref):
    t = min(n, pref)
    while n % t:
        t //= 2
    return t


def _silu(x):
    half = 0.5 * x
    return half + half * jnp.tanh(half)


LOG2_E = 1.4426950408889634


def _softplus(x):
    return jnp.maximum(x, 0.0) + jnp.log(1.0 + jnp.exp2(jnp.abs(x) * -LOG2_E))


NORM_ROWS = 128


def _normalize_rows(x_ref, g_ref, xn_ref):
    @pl.when(pl.program_id(1) == 0)
    def _():
        g = g_ref[...]

        def body(c, _):
            rows = pl.ds(pl.multiple_of(c * NORM_ROWS, NORM_ROWS), NORM_ROWS)
            x = x_ref[rows, :]
            ms = jnp.mean(x * x, axis=-1, keepdims=True)
            xn_ref[rows, :] = (x * lax.rsqrt(ms + EPS) * g).astype(BF16)
            return 0

        lax.fori_loop(0, x_ref.shape[0] // NORM_ROWS, body, 0)


def _norm_mm_kernel(x_ref, g_ref, w_ref, o_ref, xn_ref):
    _normalize_rows(x_ref, g_ref, xn_ref)
    o_ref[...] = jnp.dot(xn_ref[...], w_ref[...], preferred_element_type=F32)


def _norm_mm_rows_kernel(top_ref, bottom_ref, g_ref, w_ref, o_ref, xn_ref, *, n_top):
    i = pl.program_id(0)
    pl.when(i < n_top)(functools.partial(_normalize_rows, top_ref, g_ref, xn_ref))
    pl.when(i >= n_top)(functools.partial(_normalize_rows, bottom_ref, g_ref, xn_ref))
    o_ref[...] = jnp.dot(xn_ref[...], w_ref[...], preferred_element_type=F32)


def _norm_matmul(x, g, w_all, layer, *, tm_pref=1024, tn_pref=1024):
    m, k = x.shape
    n = w_all.shape[2]
    tm, tn = _tile(m, tm_pref), _tile(n, tn_pref)
    vmem = 2 * tm * k * 4 + tm * k * 2 + 2 * k * tn * 2 + 2 * tm * tn * 4
    return pl.pallas_call(
        _norm_mm_kernel,
        grid=(m // tm, n // tn),
        in_specs=[pl.BlockSpec((tm, k), lambda i, j: (i, 0)),
                  pl.BlockSpec((1, k), lambda i, j: (0, 0)),
                  pl.BlockSpec((None, k, tn), lambda i, j: (layer, 0, j))],
        out_specs=pl.BlockSpec((tm, tn), lambda i, j: (i, j)),
        out_shape=jax.ShapeDtypeStruct((m, n), F32),
        scratch_shapes=[pltpu.VMEM((tm, k), BF16)],
        compiler_params=_params(("arbitrary", "arbitrary"), vmem),
        name="norm_matmul",
    )(x, g.reshape(1, k), w_all)


def _norm_matmul_rows(top, bottom, g, w_all, layer, *, tm_pref=1024, tn_pref=1024):
    k = top.shape[1]
    m, n = top.shape[0] + bottom.shape[0], w_all.shape[2]
    tm = _tile(top.shape[0], tm_pref)
    while bottom.shape[0] % tm:
        tm //= 2
    tn = _tile(n, tn_pref)
    n_top = top.shape[0] // tm
    vmem = 3 * tm * k * 4 + tm * k * 2 + 2 * k * tn * 2 + 2 * tm * tn * 4
    return pl.pallas_call(
        functools.partial(_norm_mm_rows_kernel, n_top=n_top),
        grid=(m // tm, n // tn),
        in_specs=[pl.BlockSpec((tm, k), lambda i, j: (jnp.minimum(i, n_top - 1), 0)),
                  pl.BlockSpec((tm, k), lambda i, j: (jnp.maximum(i - n_top, 0), 0),
                               pipeline_mode=pl.Buffered(1)),
                  pl.BlockSpec((1, k), lambda i, j: (0, 0)),
                  pl.BlockSpec((None, k, tn), lambda i, j: (layer, 0, j))],
        out_specs=pl.BlockSpec((tm, tn), lambda i, j: (i, j)),
        out_shape=jax.ShapeDtypeStruct((m, n), F32),
        scratch_shapes=[pltpu.VMEM((tm, k), BF16)],
        compiler_params=_params(("arbitrary", "arbitrary"), vmem),
        name="norm_matmul_rows",
    )(top, bottom, g.reshape(1, k), w_all)


def _norm_mm_att_kernel(x_ref, g_ref, w_ref, *refs, layer, n_top, per):
    qkv_ref, gt_ref, kp_hbm, vp_hbm, ks_hbm, vs_hbm, xn_ref, kv_scr, sem = refs[-9:]
    layers = (layer,) if len(refs) > 9 else range(kp_hbm.shape[0])
    _normalize_rows(x_ref, g_ref, xn_ref)
    i, j = pl.program_id(0), pl.program_id(1)
    tm, tn = qkv_ref.shape
    hpt = tn // HEAD_DIM
    top = i < n_top

    def project():
        return jnp.dot(xn_ref[...], w_ref[...], preferred_element_type=F32)

    def copies(jj, prompt_rows):
        slot = (jj - per) % 2
        dst = ((kp_hbm, ks_hbm), (vp_hbm, vs_hbm))[jj // per - 1][0 if prompt_rows else 1]
        rows = pl.ds(i * tm if prompt_rows else (i - n_top) * tm, tm)
        return [pltpu.make_async_copy(kv_scr.at[slot, :, pl.ds(h * HEAD_DIM, HEAD_DIM)],
                                      dst.at[l, rows, (jj % per) * hpt + h, :], sem.at[slot])
                for l in layers for h in range(hpt)]

    def for_rows(jj, action):
        @pl.when(top)
        def _():
            for c in copies(jj, True):
                action(c)

        @pl.when(jnp.logical_not(top))
        def _():
            for c in copies(jj, False):
                action(c)

    @pl.when(j < per)
    def _():
        qkv_ref[...] = (project() * ATT_SCALE).astype(BF16)

    for jj in range(per, 3 * per + 2):
        @pl.when(j == jj)
        def _(jj=jj):
            if jj - 2 >= per:
                for_rows(jj - 2, lambda c: c.wait())
            if jj < 3 * per:
                slot = (jj - per) % 2
                kv_scr[slot] = project()
                qkv_ref[...] = kv_scr[slot].astype(BF16)
                for_rows(jj, lambda c: c.start())

    @pl.when(j >= 3 * per)
    def _():
        gt_ref[...] = project()


def _norm_matmul_att(x, g, w_all, new_kv, *, layer, mp, n_heads, tm_pref=1024):
    m, k = x.shape
    n_layers = w_all.shape[0]
    d_att = n_heads * HEAD_DIM
    tn = d_att // 2
    per = d_att // tn
    assert per >= 2 and w_all.shape[2] == 4 * d_att
    tm = _tile(mp, tm_pref)
    while (m - mp) % tm:
        tm //= 2
    any_spec = pl.BlockSpec(memory_space=pl.ANY)
    carried = () if new_kv is None else tuple(new_kv)
    kv_shapes = [jax.ShapeDtypeStruct((n_layers, rows, n_heads, HEAD_DIM), F32) for rows in (mp, mp, m - mp, m - mp)]
    vmem = 2 * tm * k * 4 + tm * k * 2 + 2 * k * tn * 2 + 2 * tm * tn * (2 + 4) + 3 * tm * tn * 4
    outs = pl.pallas_call(
        functools.partial(_norm_mm_att_kernel, layer=layer, n_top=mp // tm, per=per),
        grid=(m // tm, 4 * per),
        in_specs=[pl.BlockSpec((tm, k), lambda i, j: (i, 0)),
                  pl.BlockSpec((1, k), lambda i, j: (0, 0)),
                  pl.BlockSpec((None, k, tn), lambda i, j: (layer, 0, j))] + [any_spec] * len(carried),
        out_specs=[pl.BlockSpec((tm, tn), lambda i, j: (i, jnp.minimum(j, 3 * per - 1))),
                   pl.BlockSpec((tm, tn), lambda i, j: (i, jnp.maximum(j - 3 * per, 0))),
                   any_spec, any_spec, any_spec, any_spec],
        out_shape=[jax.ShapeDtypeStruct((m, 3 * d_att), BF16), jax.ShapeDtypeStruct((m, d_att), F32)] + kv_shapes,
        input_output_aliases={3 + n: 2 + n for n in range(len(carried))},
        scratch_shapes=[pltpu.VMEM((tm, k), BF16), pltpu.VMEM((2, tm, tn), F32),
                        pltpu.SemaphoreType.DMA((2,))],
        compiler_params=_params(("arbitrary", "arbitrary"), vmem),
        name="norm_matmul_att",
    )(x, g.reshape(1, k), w_all, *carried)
    return outs[0], outs[1], tuple(outs[2:])


def _mm_res_kernel(*refs, n_pairs, n_top):
    lhs, ws = refs[:n_pairs], refs[n_pairs:2 * n_pairs]
    res_refs, out_ref = refs[2 * n_pairs:-1], refs[-1]

    def emit(res_ref):
        acc = res_ref[...]
        for l, w in zip(lhs, ws):
            acc = acc + jnp.dot(l[...], w[...], preferred_element_type=F32)
        out_ref[...] = acc

    if len(res_refs) == 1:
        emit(res_refs[0])
    else:
        i = pl.program_id(0)
        pl.when(i < n_top)(functools.partial(emit, res_refs[0]))
        pl.when(i >= n_top)(functools.partial(emit, res_refs[1]))


def _matmul_residual(lhs_list, w_all, layer, res, *, tm_pref=1024, tn_pref=1024):
    parts = res if isinstance(res, tuple) else (res,)
    m, n = sum(p.shape[0] for p in parts), parts[0].shape[1]
    tm = _tile(parts[0].shape[0], tm_pref)
    while any(p.shape[0] % tm for p in parts):
        tm //= 2
    tn = _tile(n, tn_pref)
    n_top = parts[0].shape[0] // tm
    ks = [l.shape[1] for l in lhs_list]
    starts = [sum(ks[:p]) for p in range(len(ks))]
    assert all(r % k == 0 for r, k in zip(starts, ks)) and sum(ks) == w_all.shape[1]
    vmem = sum(2 * tm * k * 2 + 2 * k * tn * 2 for k in ks) + (2 + 2 * len(parts)) * tm * tn * 4
    res_specs = ([pl.BlockSpec((tm, tn), lambda i, j: (i, j))] if len(parts) == 1 else
                 [pl.BlockSpec((tm, tn), lambda i, j: (jnp.minimum(i, n_top - 1), j)),
                  pl.BlockSpec((tm, tn), lambda i, j: (jnp.maximum(i - n_top, 0), j))])
    in_specs = ([pl.BlockSpec((tm, k), lambda i, j: (i, 0)) for k in ks]
                + [pl.BlockSpec((None, k, tn), lambda i, j, blk=r // k: (layer, blk, j)) for r, k in zip(starts, ks)]
                + res_specs)
    return pl.pallas_call(
        functools.partial(_mm_res_kernel, n_pairs=len(ks), n_top=n_top),
        grid=(m // tm, n // tn),
        in_specs=in_specs,
        out_specs=pl.BlockSpec((tm, tn), lambda i, j: (i, j)),
        out_shape=jax.ShapeDtypeStruct((m, n), F32),
        compiler_params=_params(("arbitrary", "arbitrary"), vmem),
        name="matmul_residual",
    )(*lhs_list, *([w_all] * len(ks)), *parts)


def _mm_res_rows_kernel(top_ref, bottom_ref, w_ref, res_ref, out_ref, *, n_top):
    i = pl.program_id(0)

    def emit(lhs_ref):
        out_ref[...] = res_ref[...] + jnp.dot(lhs_ref[...], w_ref[...], preferred_element_type=F32)

    pl.when(i < n_top)(functools.partial(emit, top_ref))
    pl.when(i >= n_top)(functools.partial(emit, bottom_ref))


def _mm_res_rows_norm_kernel(top_ref, bottom_ref, w_ref, res_ref, g_ref, top_out, bottom_out, *, n_top):
    i = pl.program_id(0)

    def emit(lhs_ref, out_ref):
        y = res_ref[...] + jnp.dot(lhs_ref[...], w_ref[...], preferred_element_type=F32)
        ms = jnp.mean(y * y, axis=-1, keepdims=True)
        out_ref[...] = y * lax.rsqrt(ms + EPS) * g_ref[...]

    pl.when(i < n_top)(functools.partial(emit, top_ref, top_out))
    pl.when(i >= n_top)(functools.partial(emit, bottom_ref, bottom_out))


def _matmul_residual_rows_norm(top, bottom, w_all, layer, res, g, *, tm_pref=512):
    m, n = res.shape
    k = w_all.shape[1]
    tm = _tile(top.shape[0], tm_pref)
    while bottom.shape[0] % tm:
        tm //= 2
    n_top = top.shape[0] // tm
    vmem = 4 * tm * k * 2 + 2 * k * n * 2 + 6 * tm * n * 4
    return pl.pallas_call(
        functools.partial(_mm_res_rows_norm_kernel, n_top=n_top),
        grid=(m // tm,),
        in_specs=[pl.BlockSpec((tm, k), lambda i: (jnp.minimum(i, n_top - 1), 0)),
                  pl.BlockSpec((tm, k), lambda i: (jnp.maximum(i - n_top, 0), 0)),
                  pl.BlockSpec((None, k, n), lambda i: (layer, 0, 0)),
                  pl.BlockSpec((tm, n), lambda i: (i, 0)),
                  pl.BlockSpec((1, n), lambda i: (0, 0))],
        out_specs=[pl.BlockSpec((tm, n), lambda i: (jnp.minimum(i, n_top - 1), 0)),
                   pl.BlockSpec((tm, n), lambda i: (jnp.maximum(i - n_top, 0), 0))],
        out_shape=[jax.ShapeDtypeStruct((top.shape[0], n), F32), jax.ShapeDtypeStruct((bottom.shape[0], n), F32)],
        compiler_params=_params(("arbitrary",), vmem),
        name="matmul_residual_rows_norm",
    )(top, bottom, w_all, res, g.reshape(1, n))


def _matmul_residual_rows(top, bottom, w_all, layer, res, *, tm_pref=1024, tn_pref=1024):
    m, n = res.shape
    k = w_all.shape[1]
    tm = _tile(top.shape[0], tm_pref)
    while bottom.shape[0] % tm:
        tm //= 2
    tn = _tile(n, tn_pref)
    n_top = top.shape[0] // tm
    vmem = 4 * tm * k * 2 + 2 * k * tn * 2 + 4 * tm * tn * 4
    return pl.pallas_call(
        functools.partial(_mm_res_rows_kernel, n_top=n_top),
        grid=(m // tm, n // tn),
        in_specs=[pl.BlockSpec((tm, k), lambda i, j: (jnp.minimum(i, n_top - 1), 0)),
                  pl.BlockSpec((tm, k), lambda i, j: (jnp.maximum(i - n_top, 0), 0)),
                  pl.BlockSpec((None, k, tn), lambda i, j: (layer, 0, j)),
                  pl.BlockSpec((tm, tn), lambda i, j: (i, j))],
        out_specs=pl.BlockSpec((tm, tn), lambda i, j: (i, j)),
        out_shape=jax.ShapeDtypeStruct((m, n), F32),
        compiler_params=_params(("arbitrary", "arbitrary"), vmem),
        name="matmul_residual_rows",
    )(top, bottom, w_all, res)


def _final_norm_kernel(x_ref, g_ref, top_ref, bottom_ref, *, n_top):
    x = x_ref[...]
    ms = jnp.mean(x * x, axis=-1, keepdims=True)
    y = x * lax.rsqrt(ms + EPS) * g_ref[...]
    i = pl.program_id(0)

    @pl.when(i < n_top)
    def _():
        top_ref[...] = y

    @pl.when(i >= n_top)
    def _():
        bottom_ref[...] = y


def _final_norm(x, g, mp):
    m, d = x.shape
    tm = _tile(mp, 256)
    while (m - mp) % tm:
        tm //= 2
    n_top = mp // tm
    return pl.pallas_call(
        functools.partial(_final_norm_kernel, n_top=n_top),
        grid=(m // tm,),
        in_specs=[pl.BlockSpec((tm, d), lambda i: (i, 0)), pl.BlockSpec((1, d), lambda i: (0, 0))],
        out_specs=[pl.BlockSpec((tm, d), lambda i: (jnp.minimum(i, n_top - 1), 0)),
                   pl.BlockSpec((tm, d), lambda i: (jnp.maximum(i - n_top, 0), 0))],
        out_shape=[jax.ShapeDtypeStruct((mp, d), F32), jax.ShapeDtypeStruct((m - mp, d), F32)],
        compiler_params=_params(("arbitrary",), 6 * tm * d * 4),
        name="final_norm",
    )(x, g.reshape(1, d))


SCAN_ROWS = 256
SCAN_STREAMS = 4


def _scan_tiles(a, u, carry):
    r = a.shape[0]
    nt = r // SUBLANES
    a3 = a.reshape(nt, SUBLANES, LANES)
    u3 = u.reshape(nt, SUBLANES, LANES)
    sub = lax.broadcasted_iota(jnp.int32, a3.shape, 1)
    for d in (1, 2, 4):
        keep = sub >= d
        a_prev = pltpu.roll(a3, d, 1)
        u_prev = pltpu.roll(u3, d, 1)
        u3 = jnp.where(keep, a3 * u_prev + u3, u3)
        a3 = jnp.where(keep, a3 * a_prev, a3)
    hs = []
    for t in range(nt):
        h_t = u3[t] + a3[t] * carry
        carry = h_t[SUBLANES - 1:SUBLANES, :]
        hs.append(h_t)
    return jnp.concatenate(hs, axis=0), carry


def _rglru_kernel(xa_ref, ga_ref, h0_ref, sc_ref, cw_ref, cb_ref, wr_ref, br_ref, wi_ref, bi_ref, lam_ref,
                  out_ref, hp_ref, hs_ref, *, bp, t, bs, ts):
    mp = bp * t
    cw = cw_ref[...]
    cb = cb_ref[...]
    half_wr = (0.5 * wr_ref[...]).astype(BF16)
    half_wi = (0.5 * wi_ref[...]).astype(BF16)
    half_br = 0.5 * br_ref[...]
    half_bi = 0.5 * bi_ref[...]
    half_c = (-0.5 * RG_C) * _softplus(-lam_ref[...])

    def conv_taps(tap):
        y = cb + tap(0) * cw[0:1]
        for k in range(1, CONV_W):
            y = y + tap(k) * cw[k:k + 1]
        return y

    def conv(ext, n):
        def tap(k):
            back = CONV_W - 1 - k
            return (ext if back == 0 else pltpu.roll(ext, back, 0))[SUBLANES:]

        return conv_taps(tap)

    def decay_and_input(xc):
        xb = xc.astype(BF16)
        tr = jnp.tanh(jnp.dot(xb, half_wr, preferred_element_type=F32) + half_br)
        ti = jnp.tanh(jnp.dot(xb, half_wi, preferred_element_type=F32) + half_bi)
        log_a = half_c + half_c * tr
        half_x = 0.5 * xc
        gated_x = half_x + half_x * ti
        a = jnp.exp(log_a)
        v = -jnp.tanh(log_a) * (1.0 + a * a)
        root = jnp.where(v > 0.0, v * lax.rsqrt(v), 0.0)
        return a, root * gated_x

    def emit(rows, a, u, carry):
        h, carry = _scan_tiles(a, u, carry)
        out_ref[rows, :] = (h * _silu(ga_ref[rows, :])).astype(out_ref.dtype)
        return carry

    rp = _tile(t, SCAN_ROWS)

    def first_chunk(b):
        rows = pl.ds(b * t, rp)
        ext = jnp.concatenate([jnp.zeros((SUBLANES, LANES), F32), xa_ref[rows, :]], axis=0)
        a, u = decay_and_input(conv(ext, rp))
        return emit(rows, a, u, jnp.zeros((1, LANES), F32))

    def later_chunk(c, carries):
        new = []
        for b in range(bp):
            r0 = b * t + c * rp
            rows = pl.ds(pl.multiple_of(r0, SUBLANES), rp)
            xc = conv_taps(lambda k: xa_ref[pl.ds(r0 - (CONV_W - 1 - k), rp), :])
            a, u = decay_and_input(xc)
            new.append(emit(rows, a, u, carries[b]))
        return tuple(new)

    carries = lax.fori_loop(1, t // rp, later_chunk, tuple(first_chunk(b) for b in range(bp)))
    for b in range(bp):
        hp_ref[b:b + 1, :] = carries[b]

    ns = _tile(bs, SCAN_STREAMS)

    def sample_group(c, _):
        rows = [pl.ds(pl.multiple_of(mp + (c * ns + k) * ts, SUBLANES), ts) for k in range(ns)]
        xc = [conv(jnp.concatenate([sc_ref[c * ns + k], xa_ref[rows[k], :]], axis=0), ts) for k in range(ns)]
        a, u = decay_and_input(jnp.concatenate(xc, axis=0))
        for k in range(ns):
            seg = slice(k * ts, (k + 1) * ts)
            s = c * ns + k
            hs_ref[pl.ds(s, 1), :] = emit(rows[k], a[seg], u[seg], h0_ref[pl.ds(s, 1), :])
        return 0

    lax.fori_loop(0, bs // ns, sample_group, 0)


def _rglru(proj, h0, sc_pad, cw, cb, wr, br, wi, bi, lam, *, bp, t, bs, ts, d_rnn):
    m = proj.shape[0]
    nb = d_rnn // LANES
    row = lambda v: v.reshape(1, d_rnn)
    vec_spec = pl.BlockSpec((1, LANES), lambda n: (0, n))
    w_spec = pl.BlockSpec((None, LANES, LANES), lambda n: (n, 0, 0))
    vmem = 2 * (2 * m * LANES * 4 + m * LANES * 2)
    return pl.pallas_call(
        functools.partial(_rglru_kernel, bp=bp, t=t, bs=bs, ts=ts),
        grid=(nb,),
        in_specs=[pl.BlockSpec((m, LANES), lambda n: (0, n)),
                  pl.BlockSpec((m, LANES), lambda n: (0, nb + n)),
                  pl.BlockSpec((bs, LANES), lambda n: (0, n)),
                  pl.BlockSpec((bs, SUBLANES, LANES), lambda n: (0, 0, n)),
                  pl.BlockSpec((CONV_W, LANES), lambda n: (0, n)),
                  vec_spec, w_spec, vec_spec, w_spec, vec_spec, vec_spec],
        out_specs=[pl.BlockSpec((m, LANES), lambda n: (0, n)),
                   pl.BlockSpec((bp, LANES), lambda n: (0, n)),
                   pl.BlockSpec((bs, LANES), lambda n: (0, n))],
        out_shape=[jax.ShapeDtypeStruct((m, d_rnn), BF16),
                   jax.ShapeDtypeStruct((bp, d_rnn), F32),
                   jax.ShapeDtypeStruct((bs, d_rnn), F32)],
        compiler_params=_params(("arbitrary",), vmem),
        name="rglru_mixer",
    )(proj, proj, h0, sc_pad, cw, row(cb), wr, row(br), wi, row(bi), row(lam))


POOL_TAIL = 16
POOL_ROWS = 128


def _pool_kernel(xb_ref, gb_ref, sp_ref, pw_ref, ps_ref, out_ref, m_scr, tail_scr, *, npb, bpt, ts):
    g = pl.program_id(0)
    rb = pl.program_id(1)
    rbk, gw = xb_ref.shape
    wf = lax.shift_left(jnp.int32(2), g).astype(F32)

    def window_means(ext, n, pos0):
        s2 = ext[1:] + ext[:-1]
        s4 = s2[2:] + s2[:-2]
        s8 = s4[4:] + s4[:-4]
        s16 = s8[8:] + s8[:-8]
        x = ext[POOL_TAIL:]
        win = jnp.where(g == 0, s2[POOL_TAIL - 1:],
                        jnp.where(g == 1, s4[POOL_TAIL - 3:],
                                  jnp.where(g == 2, s8[POOL_TAIL - 7:], s16[POOL_TAIL - 15:])))
        if pos0 is None:
            cnt = wf
        else:
            pos = pos0 + lax.broadcasted_iota(jnp.int32, (n, gw), 0)
            cnt = jnp.minimum(wf, (pos + 1).astype(F32))
        return win / cnt - x

    @pl.when(rb < npb)
    def _():
        blk = rb % bpt

        @pl.when(blk == 0)
        def _():
            tail_scr[...] = jnp.zeros_like(tail_scr)

        rc = _tile(rbk, POOL_ROWS)

        def body(c, tail):
            rows = pl.ds(pl.multiple_of(c * rc, SUBLANES), rc)
            x = xb_ref[rows, :]
            ext = jnp.concatenate([tail, x], axis=0)
            m_scr[rows, :] = window_means(ext, rc, blk * rbk + c * rc).astype(BF16)
            return x[rc - POOL_TAIL:, :]

        tail_scr[...] = lax.fori_loop(0, rbk // rc, body, tail_scr[...])

    @pl.when(rb >= npb)
    def _():
        s0 = (rb - npb) * (rbk // ts)

        def body(s, _):
            rows = pl.ds(pl.multiple_of(s * ts, SUBLANES), ts)
            ext = jnp.concatenate([sp_ref[s0 + s], xb_ref[rows, :]], axis=0)
            m_scr[rows, :] = window_means(ext, ts, None).astype(BF16)
            return 0

        lax.fori_loop(0, rbk // ts, body, 0)

    y = jnp.dot(m_scr[...], pw_ref[...].astype(BF16), preferred_element_type=F32) * ps_ref[...]
    out_ref[...] = (y * _silu(gb_ref[...])).astype(out_ref.dtype)


def _pool(proj, sp_pad, pw, ps, *, bp, t, bs, ts, d_rnn, d_pool):
    m = proj.shape[0]
    ng = len(POOL_WINDOWS)
    gw = d_pool // ng
    ms = bs * ts
    rbk = min(1024, t, ms)
    while t % rbk or ms % rbk:
        rbk //= 2
    npb = bp * t // rbk
    xcol = 2 * d_rnn // gw
    gcol = (2 * d_rnn + d_pool) // gw
    vmem = 2 * (2 * rbk * gw * 4 + rbk * gw * 2) + rbk * gw * 2 + 2 * bs * POOL_TAIL * gw * 4 + 2 * gw * gw * 4
    return pl.pallas_call(
        functools.partial(_pool_kernel, npb=npb, bpt=t // rbk, ts=ts),
        grid=(ng, m // rbk),
        in_specs=[pl.BlockSpec((rbk, gw), lambda g, r: (r, xcol + g)),
                  pl.BlockSpec((rbk, gw), lambda g, r: (r, gcol + g)),
                  pl.BlockSpec((bs, POOL_TAIL, gw), lambda g, r: (0, 0, g)),
                  pl.BlockSpec((None, gw, gw), lambda g, r: (g, 0, 0)),
                  pl.BlockSpec((1, gw), lambda g, r: (0, g))],
        out_specs=pl.BlockSpec((rbk, gw), lambda g, r: (r, g)),
        out_shape=jax.ShapeDtypeStruct((m, d_pool), BF16),
        scratch_shapes=[pltpu.VMEM((rbk, gw), BF16), pltpu.VMEM((POOL_TAIL, gw), F32)],
        compiler_params=_params(("arbitrary", "arbitrary"), vmem),
        name="pool_mixer",
    )(proj, proj, sp_pad, pw, ps.reshape(1, d_pool))


def _suffix_matrix(n):
    j = lax.broadcasted_iota(jnp.int32, (n, n), 0)
    s = lax.broadcasted_iota(jnp.int32, (n, n), 1)
    u = jnp.where(j > s, -1.0, 0.0).astype(BF16)
    return jnp.concatenate([u, u], axis=0)


def _qk(q, k):
    return lax.dot_general(q, k, (((1,), (1,)), ((), ())), preferred_element_type=F32)


def _strictly_earlier(shape):
    return lax.broadcasted_iota(jnp.int32, shape, 1) < lax.broadcasted_iota(jnp.int32, shape, 0)


def _sb_weights(z, uu, carry=None, mask=None):
    sp = _softplus(z)
    spm = sp if mask is None else jnp.where(mask, sp, 0.0)
    hi = spm.astype(BF16)
    lo = (spm - hi.astype(F32)).astype(BF16)
    e = z - sp + jnp.dot(jnp.concatenate([hi, lo], axis=1), uu, preferred_element_type=F32)
    w = jnp.exp(e if carry is None else e + carry)
    if mask is not None:
        w = jnp.where(mask, w, 0.0)
    return w.astype(BF16), -jnp.sum(spm, axis=1, keepdims=True)


def _sb_tile(q, k, v, uu, carry=None, mask=None):
    w, tot = _sb_weights(_qk(q, k), uu, carry, mask)
    return jnp.dot(w, v, preferred_element_type=F32), tot


PROMPT_TILE = 256
PROMPT_ALWAYS = 2


def _attn_prompt_kernel(q_ref, k_ref, v_ref, gt_ref, uu_ref, o_ref, acc_scr, car_scr, worst_ref, *, tq):
    uu = uu_ref[...]
    mask = _strictly_earlier((tq, tq))
    nq = q_ref.shape[0] // tq

    for qi in range(nq):
        rows = pl.ds(qi * tq, tq)
        q = q_ref[rows, :]
        acc, carry = _sb_tile(q, k_ref[rows, :], v_ref[rows, :], uu, None, mask)
        for back in range(1, min(qi, PROMPT_ALWAYS - 1) + 1):
            prev = pl.ds((qi - back) * tq, tq)
            pv, tot = _sb_tile(q, k_ref[prev, :], v_ref[prev, :], uu, carry)
            acc, carry = acc + pv, carry + tot
        o_ref[rows, :] = (acc * _silu(gt_ref[rows, :])).astype(o_ref.dtype)
        if qi >= PROMPT_ALWAYS:
            acc_scr[qi] = acc
            car_scr[qi] = carry
            worst_ref[qi] = jnp.max(carry)

    def finish(qi, _):
        @pl.when(worst_ref[qi] >= EXP_UNDERFLOW)
        def _():
            rows = pl.ds(pl.multiple_of(qi * tq, tq), tq)
            q = q_ref[rows, :]

            def more(s):
                return jnp.logical_and(s[0] >= 0, jnp.max(s[2]) >= EXP_UNDERFLOW)

            def older(s):
                j, acc, carry = s
                old = pl.ds(pl.multiple_of(j * tq, tq), tq)
                pv, tot = _sb_tile(q, k_ref[old, :], v_ref[old, :], uu, carry)
                return j - 1, acc + pv, carry + tot

            _, acc, _ = lax.while_loop(more, older, (qi - PROMPT_ALWAYS, acc_scr[qi], car_scr[qi]))
            o_ref[rows, :] = (acc * _silu(gt_ref[rows, :])).astype(o_ref.dtype)

        return 0

    lax.fori_loop(PROMPT_ALWAYS, nq, finish, 0)


def _attn_prompt(qkv, gate, *, bp, t, n_heads):
    tq = _tile(t, PROMPT_TILE)
    d_att = n_heads * HEAD_DIM
    nq = t // tq
    state = pltpu.VMEM((nq, tq, LANES), F32), pltpu.VMEM((nq, tq, 1), F32), pltpu.SMEM((nq,), F32)
    vmem = 2 * (3 * t * LANES * 2 + t * LANES * 4 + 2 * tq * tq * 2 + t * LANES * 2) + 2 * t * LANES * 4
    col = lambda c: pl.BlockSpec((t, LANES), lambda b, h: (b, c * n_heads + h))
    return pl.pallas_call(
        functools.partial(_attn_prompt_kernel, tq=tq),
        grid=(bp, n_heads),
        in_specs=[col(0), col(1), col(2), col(0), pl.BlockSpec((2 * tq, tq), lambda b, h: (0, 0))],
        out_specs=pl.BlockSpec((t, LANES), lambda b, h: (b, h)),
        out_shape=jax.ShapeDtypeStruct((bp * t, d_att), BF16),
        scratch_shapes=list(state),
        compiler_params=_params(("arbitrary", "arbitrary"), vmem),
        name="attn_prompt",
    )(qkv, qkv, qkv, gate, _suffix_matrix(tq))


def _attn_sample_kernel(q_ref, kn_ref, vn_ref, gt_ref, kc_hbm, vc_hbm, un_ref, uc_ref, o_ref,
                        kfirst, vfirst, kmore, vmore, acc_scr, car_scr, sem, *, layer, pc):
    b = pl.program_id(0)
    n_heads, ts = kmore.shape[0], q_ref.shape[0]
    newest = kc_hbm.shape[2] // pc - 1
    slot = b % 2

    def copies(stream, chunk, kdst, vdst, ksem, vsem):
        pos = pl.ds(chunk * pc, pc)
        out = []
        for h in range(n_heads):
            out.append(pltpu.make_async_copy(kc_hbm.at[layer, stream, pos, h, :], kdst.at[h], ksem))
            out.append(pltpu.make_async_copy(vc_hbm.at[layer, stream, pos, h, :], vdst.at[h], vsem))
        return out

    def first_copies(stream, s):
        return copies(stream, newest, kfirst.at[s], vfirst.at[s], sem.at[s, 0], sem.at[s, 1])

    @pl.when(b == 0)
    def _():
        for c in first_copies(0, 0):
            c.start()

    @pl.when(b + 1 < pl.num_programs(0))
    def _():
        for c in first_copies(b + 1, 1 - slot):
            c.start()

    def head(ref, h):
        return ref[:, h * HEAD_DIM:(h + 1) * HEAD_DIM]

    def scores(keys):
        return jnp.concatenate([_qk(head(q_ref, h), keys(h)) for h in range(n_heads)], axis=0)

    def weighted(w, values):
        return jnp.concatenate([jnp.dot(w[h * ts:(h + 1) * ts], values(h), preferred_element_type=F32)
                                for h in range(n_heads)], axis=0)

    pad = jnp.zeros((LANES - ts, HEAD_DIM), BF16)
    query = lax.broadcasted_iota(jnp.int32, (n_heads, ts, LANES), 1).reshape(n_heads * ts, LANES)
    mask = lax.broadcasted_iota(jnp.int32, (n_heads * ts, LANES), 1) < query
    w, carry = _sb_weights(scores(lambda h: jnp.concatenate([head(kn_ref, h), pad], axis=0)),
                           un_ref[...], None, mask)
    acc = weighted(w, lambda h: jnp.concatenate([head(vn_ref, h), pad], axis=0))

    def chunk(kbuf, vbuf, acc, carry):
        w, tot = _sb_weights(scores(lambda h: kbuf[h].astype(BF16)), uc_ref[...], carry)
        return acc + weighted(w, lambda h: vbuf[h].astype(BF16)), carry + tot

    for c in first_copies(b, slot):
        c.wait()
    acc_scr[...], car_scr[...] = chunk(kfirst.at[slot], vfirst.at[slot], acc, carry)

    def more(s):
        return jnp.logical_and(s[0] >= 0, s[1] >= EXP_UNDERFLOW)

    def older(s):
        cs = copies(b, s[0], kmore, vmore, sem.at[2, 0], sem.at[2, 1])
        for c in cs:
            c.start()
        for c in cs:
            c.wait()
        acc_scr[...], car_scr[...] = chunk(kmore, vmore, acc_scr[...], car_scr[...])
        return s[0] - 1, jnp.max(car_scr[...])

    lax.while_loop(more, older, (newest - 1, jnp.max(car_scr[...])))

    for h in range(n_heads):
        o_ref[:, h * HEAD_DIM:(h + 1) * HEAD_DIM] = (acc_scr[pl.ds(h * ts, ts), :]
                                                     * _silu(head(gt_ref, h))).astype(o_ref.dtype)


def _attn_sample(qkv, gate, cache_k, cache_v, *, layer, mp, bs, ts, n_heads):
    p = cache_k.shape[2]
    d_att = n_heads * HEAD_DIM
    pc = _tile(p, 256)
    r0 = mp // ts
    chunk_bytes = n_heads * pc * HEAD_DIM * 4
    vmem = (2 * (3 * ts * d_att * 2 + ts * d_att * 4 + ts * d_att * 2) + 6 * chunk_bytes
            + 2 * n_heads * ts * LANES * 4 + 2 * 2 * (LANES * LANES + pc * pc) * 2)
    new_spec = lambda c: pl.BlockSpec((ts, d_att), lambda b: (r0 + b, c))
    chunk_buf = lambda n: pltpu.VMEM(n + (n_heads, pc, HEAD_DIM), F32)
    return pl.pallas_call(
        functools.partial(_attn_sample_kernel, layer=layer, pc=pc),
        grid=(bs,),
        in_specs=[new_spec(0), new_spec(1), new_spec(2), new_spec(0),
                  pl.BlockSpec(memory_space=pl.ANY), pl.BlockSpec(memory_space=pl.ANY),
                  pl.BlockSpec((2 * LANES, LANES), lambda b: (0, 0)),
                  pl.BlockSpec((2 * pc, pc), lambda b: (0, 0))],
        out_specs=pl.BlockSpec((ts, d_att), lambda b: (b, 0)),
        out_shape=jax.ShapeDtypeStruct((bs * ts, d_att), BF16),
        scratch_shapes=[chunk_buf((2,)), chunk_buf((2,)), chunk_buf(()), chunk_buf(()),
                        pltpu.VMEM((n_heads * ts, LANES), F32), pltpu.VMEM((n_heads * ts, 1), F32),
                        pltpu.SemaphoreType.DMA((3, 2))],
        compiler_params=_params(("arbitrary",), vmem),
        name="attn_sample",
    )(qkv, qkv, qkv, gate, cache_k, cache_v, _suffix_matrix(LANES), _suffix_matrix(pc))


@jax.jit
def _step(x_prompt, x_sample, cache_k, cache_v, state_h, state_conv, state_pool,
          norm_rec, w_in_rec, conv_w, conv_b, gate_r_w, gate_r_b, gate_i_w, gate_i_b, rg_lambda,
          pool_w, pool_scale, w_out_rec, norm_att, w_in_att, w_out_att, norm_final):
    bp, t, d = x_prompt.shape
    bs, ts, _ = x_sample.shape
    n_rec, n_att = norm_rec.shape[0], norm_att.shape[0]
    d_rnn = state_h.shape[-1]
    d_pool = state_pool.shape[-1]
    n_heads = cache_k.shape[3]
    mp = bp * t
    assert ts >= POOL_BUF and ts % SUBLANES == 0 and t % ts == 0 and cache_k.shape[4] == HEAD_DIM
    assert gate_r_w.shape[2] == LANES and d_pool // len(POOL_WINDOWS) == 2 * LANES

    x = (x_prompt.reshape(mp, d), x_sample.reshape(bs * ts, d))
    sc_pad = jnp.pad(state_conv, ((0, 0), (0, 0), (SUBLANES - (CONV_W - 1), 0), (0, 0)))
    sp_pad = jnp.pad(state_pool, ((0, 0), (0, 0), (POOL_TAIL - POOL_BUF, 0), (0, 0)))
    w_in_rec, w_out_rec, w_in_att, w_out_att = (w.astype(BF16) for w in (w_in_rec, w_out_rec, w_in_att, w_out_att))

    outs = {k: [] for k in ("hp", "cp", "pp", "hs", "cs", "ps")}
    new_kv = None
    for layer in range(n_rec + n_att):
        j = layer // 2
        if layer % 2 == 0:
            proj = (_norm_matmul_rows(*x, norm_rec[j], w_in_rec, j) if isinstance(x, tuple)
                    else _norm_matmul(x, norm_rec[j], w_in_rec, j))
            ya, hp, hs = _rglru(proj, state_h[j], sc_pad[j], conv_w[j], conv_b[j], gate_r_w[j], gate_r_b[j],
                                gate_i_w[j], gate_i_b[j], rg_lambda[j], bp=bp, t=t, bs=bs, ts=ts, d_rnn=d_rnn)
            yb = _pool(proj, sp_pad[j], pool_w[j], pool_scale[j], bp=bp, t=t, bs=bs, ts=ts,
                       d_rnn=d_rnn, d_pool=d_pool)
            x = _matmul_residual([ya, yb], w_out_rec, j, x)
            frames = proj.reshape(-1, ts, proj.shape[1])

            def last_rows(n, c0, c1):
                prompt = lax.slice(frames, (t // ts - 1, ts - n, c0), (mp // ts, ts, c1), (t // ts, 1, 1))
                sample = lax.slice(frames, (mp // ts, ts - n, c0), (frames.shape[0], ts, c1))
                return prompt, sample

            cp, cs = last_rows(CONV_W - 1, 0, d_rnn)
            pp, ps = last_rows(POOL_BUF, 2 * d_rnn, 2 * d_rnn + d_pool)
            for key, val in (("hp", hp), ("hs", hs), ("cp", cp), ("cs", cs), ("pp", pp), ("ps", ps)):
                outs[key].append(val)
        else:
            qkv, gate, new_kv = _norm_matmul_att(x, norm_att[j], w_in_att, new_kv, layer=j, mp=mp, n_heads=n_heads)
            op = _attn_prompt(qkv, gate, bp=bp, t=t, n_heads=n_heads)
            os_ = _attn_sample(qkv, gate, cache_k, cache_v, layer=j, mp=mp, bs=bs, ts=ts, n_heads=n_heads)
            if layer + 1 < n_rec + n_att:
                x = _matmul_residual_rows(op, os_, w_out_att, j, x)
            else:
                y_prompt, y_sample = _matmul_residual_rows_norm(op, os_, w_out_att, j, x, norm_final)

    if (n_rec + n_att) % 2:
        y_prompt, y_sample = _final_norm(x, norm_final, mp)
    st = {k: jnp.stack(v) for k, v in outs.items()}
    kp, vp, ks, vs = new_kv
    prompt_shape = (n_att, bp, t, n_heads, HEAD_DIM)
    sample_shape = (n_att, bs, ts, n_heads, HEAD_DIM)
    return (y_prompt.reshape(bp, t, d), y_sample.reshape(bs, ts, d),
            kp.reshape(prompt_shape), vp.reshape(prompt_shape), st["hp"], st["cp"], st["pp"],
            ks.reshape(sample_shape), vs.reshape(sample_shape), st["hs"], st["cs"], st["ps"])


def kernel(x_prompt, x_sample, cache_k, cache_v, state_h, state_conv, state_pool, norm_rec, w_in_rec, conv_w, conv_b, gate_r_w, gate_r_b, gate_i_w, gate_i_b, rg_lambda, pool_w, pool_scale, w_out_rec, norm_att, w_in_att, w_out_att, norm_final):
    return _step(x_prompt, x_sample, cache_k, cache_v, state_h, state_conv, state_pool, norm_rec, w_in_rec,
                 conv_w, conv_b, gate_r_w, gate_r_b, gate_i_w, gate_i_b, rg_lambda, pool_w, pool_scale,
                 w_out_rec, norm_att, w_in_att, w_out_att, norm_final)
```

```python
import functools

import jax
import jax.numpy as jnp
from jax import lax
from jax.experimental import pallas as pl
from jax.experimental.pallas import tpu as pltpu

F32 = jnp.float32
BF16 = jnp.bfloat16

EPS = 1e-6
RG_C = 8.0
CONV_W = 4
POOL_WINDOWS = (2, 4, 8, 16)
POOL_BUF = max(POOL_WINDOWS) - 1
HEAD_DIM = 128
ATT_SCALE = HEAD_DIM ** -0.5
EXP_UNDERFLOW = -105.0

LANES = 128
SUBLANES = 8
VMEM_LIMIT_CAP = 60000 * 1024
VMEM_SLACK = 8 * 1024 * 1024


def _params(semantics, buffer_bytes):
    limit = min(VMEM_LIMIT_CAP, buffer_bytes + VMEM_SLACK)
    return pltpu.CompilerParams(dimension_semantics=semantics, vmem_limit_bytes=limit)


def _tile(n, pref):
    t = min(n, pref)
    while n % t:
        t //= 2
    return t


def _silu(x):
    half = 0.5 * x
    return half + half * jnp.tanh(half)


LOG2_E = 1.4426950408889634


def _softplus(x):
    return jnp.maximum(x, 0.0) + jnp.log(1.0 + jnp.exp2(jnp.abs(x) * -LOG2_E))


NORM_ROWS = 128


def _normalize_rows(x_ref, g_ref, xn_ref):
    @pl.when(pl.program_id(1) == 0)
    def _():
        g = g_ref[...]

        def body(c, _):
            rows = pl.ds(pl.multiple_of(c * NORM_ROWS, NORM_ROWS), NORM_ROWS)
            x = x_ref[rows, :]
            ms = jnp.mean(x * x, axis=-1, keepdims=True)
            xn_ref[rows, :] = (x * lax.rsqrt(ms + EPS) * g).astype(BF16)
            return 0

        lax.fori_loop(0, x_ref.shape[0] // NORM_ROWS, body, 0)


def _norm_mm_kernel(x_ref, g_ref, w_ref, o_ref, xn_ref):
    _normalize_rows(x_ref, g_ref, xn_ref)
    o_ref[...] = jnp.dot(xn_ref[...], w_ref[...], preferred_element_type=F32)


def _norm_mm_rows_kernel(top_ref, bottom_ref, g_ref, w_ref, o_ref, xn_ref, *, n_top):
    i = pl.program_id(0)
    pl.when(i < n_top)(functools.partial(_normalize_rows, top_ref, g_ref, xn_ref))
    pl.when(i >= n_top)(functools.partial(_normalize_rows, bottom_ref, g_ref, xn_ref))
    o_ref[...] = jnp.dot(xn_ref[...], w_ref[...], preferred_element_type=F32)


def _norm_matmul(x, g, w_all, layer, *, tm_pref=1024, tn_pref=1024):
    m, k = x.shape
    n = w_all.shape[2]
    tm, tn = _tile(m, tm_pref), _tile(n, tn_pref)
    vmem = 2 * tm * k * 4 + tm * k * 2 + 2 * k * tn * 2 + 2 * tm * tn * 4
    return pl.pallas_call(
        _norm_mm_kernel,
        grid=(m // tm, n // tn),
        in_specs=[pl.BlockSpec((tm, k), lambda i, j: (i, 0)),
                  pl.BlockSpec((1, k), lambda i, j: (0, 0)),
                  pl.BlockSpec((None, k, tn), lambda i, j: (layer, 0, j))],
        out_specs=pl.BlockSpec((tm, tn), lambda i, j: (i, j)),
        out_shape=jax.ShapeDtypeStruct((m, n), F32),
        scratch_shapes=[pltpu.VMEM((tm, k), BF16)],
        compiler_params=_params(("arbitrary", "arbitrary"), vmem),
        name="norm_matmul",
    )(x, g.reshape(1, k), w_all)


def _norm_matmul_rows(top, bottom, g, w_all, layer, *, tm_pref=1024, tn_pref=1024):
    k = top.shape[1]
    m, n = top.shape[0] + bottom.shape[0], w_all.shape[2]
    tm = _tile(top.shape[0], tm_pref)
    while bottom.shape[0] % tm:
        tm //= 2
    tn = _tile(n, tn_pref)
    n_top = top.shape[0] // tm
    vmem = 3 * tm * k * 4 + tm * k * 2 + 2 * k * tn * 2 + 2 * tm * tn * 4
    return pl.pallas_call(
        functools.partial(_norm_mm_rows_kernel, n_top=n_top),
        grid=(m // tm, n // tn),
        in_specs=[pl.BlockSpec((tm, k), lambda i, j: (jnp.minimum(i, n_top - 1), 0)),
                  pl.BlockSpec((tm, k), lambda i, j: (jnp.maximum(i - n_top, 0), 0),
                               pipeline_mode=pl.Buffered(1)),
                  pl.BlockSpec((1, k), lambda i, j: (0, 0)),
                  pl.BlockSpec((None, k, tn), lambda i, j: (layer, 0, j))],
        out_specs=pl.BlockSpec((tm, tn), lambda i, j: (i, j)),
        out_shape=jax.ShapeDtypeStruct((m, n), F32),
        scratch_shapes=[pltpu.VMEM((tm, k), BF16)],
        compiler_params=_params(("arbitrary", "arbitrary"), vmem),
        name="norm_matmul_rows",
    )(top, bottom, g.reshape(1, k), w_all)


def _norm_mm_att_kernel(x_ref, g_ref, w_ref, *refs, layer, n_top, per):
    qkv_ref, gt_ref, kp_hbm, vp_hbm, ks_hbm, vs_hbm, xn_ref, kv_scr, sem = refs[-9:]
    layers = (layer,) if len(refs) > 9 else range(kp_hbm.shape[0])
    _normalize_rows(x_ref, g_ref, xn_ref)
    i, j = pl.program_id(0), pl.program_id(1)
    tm, tn = qkv_ref.shape
    hpt = tn // HEAD_DIM
    top = i < n_top

    def project():
        return jnp.dot(xn_ref[...], w_ref[...], preferred_element_type=F32)

    def copies(jj, prompt_rows):
        slot = (jj - per) % 2
        dst = ((kp_hbm, ks_hbm), (vp_hbm, vs_hbm))[jj // per - 1][0 if prompt_rows else 1]
        rows = pl.ds(i * tm if prompt_rows else (i - n_top) * tm, tm)
        return [pltpu.make_async_copy(kv_scr.at[slot, :, pl.ds(h * HEAD_DIM, HEAD_DIM)],
                                      dst.at[l, rows, (jj % per) * hpt + h, :], sem.at[slot])
                for l in layers for h in range(hpt)]

    def for_rows(jj, action):
        @pl.when(top)
        def _():
            for c in copies(jj, True):
                action(c)

        @pl.when(jnp.logical_not(top))
        def _():
            for c in copies(jj, False):
                action(c)

    @pl.when(j < per)
    def _():
        qkv_ref[...] = (project() * ATT_SCALE).astype(BF16)

    for jj in range(per, 3 * per + 2):
        @pl.when(j == jj)
        def _(jj=jj):
            if jj - 2 >= per:
                for_rows(jj - 2, lambda c: c.wait())
            if jj < 3 * per:
                slot = (jj - per) % 2
                kv_scr[slot] = project()
                qkv_ref[...] = kv_scr[slot].astype(BF16)
                for_rows(jj, lambda c: c.start())

    @pl.when(j >= 3 * per)
    def _():
        gt_ref[...] = project()


def _norm_matmul_att(x, g, w_all, new_kv, *, layer, mp, n_heads, tm_pref=1024):
    m, k = x.shape
    n_layers = w_all.shape[0]
    d_att = n_heads * HEAD_DIM
    tn = d_att // 2
    per = d_att // tn
    assert per >= 2 and w_all.shape[2] == 4 * d_att
    tm = _tile(mp, tm_pref)
    while (m - mp) % tm:
        tm //= 2
    any_spec = pl.BlockSpec(memory_space=pl.ANY)
    carried = () if new_kv is None else tuple(new_kv)
    kv_shapes = [jax.ShapeDtypeStruct((n_layers, rows, n_heads, HEAD_DIM), F32) for rows in (mp, mp, m - mp, m - mp)]
    vmem = 2 * tm * k * 4 + tm * k * 2 + 2 * k * tn * 2 + 2 * tm * tn * (2 + 4) + 3 * tm * tn * 4
    outs = pl.pallas_call(
        functools.partial(_norm_mm_att_kernel, layer=layer, n_top=mp // tm, per=per),
        grid=(m // tm, 4 * per),
        in_specs=[pl.BlockSpec((tm, k), lambda i, j: (i, 0)),
                  pl.BlockSpec((1, k), lambda i, j: (0, 0)),
                  pl.BlockSpec((None, k, tn), lambda i, j: (layer, 0, j))] + [any_spec] * len(carried),
        out_specs=[pl.BlockSpec((tm, tn), lambda i, j: (i, jnp.minimum(j, 3 * per - 1))),
                   pl.BlockSpec((tm, tn), lambda i, j: (i, jnp.maximum(j - 3 * per, 0))),
                   any_spec, any_spec, any_spec, any_spec],
        out_shape=[jax.ShapeDtypeStruct((m, 3 * d_att), BF16), jax.ShapeDtypeStruct((m, d_att), F32)] + kv_shapes,
        input_output_aliases={3 + n: 2 + n for n in range(len(carried))},
        scratch_shapes=[pltpu.VMEM((tm, k), BF16), pltpu.VMEM((2, tm, tn), F32),
                        pltpu.SemaphoreType.DMA((2,))],
        compiler_params=_params(("arbitrary", "arbitrary"), vmem),
        name="norm_matmul_att",
    )(x, g.reshape(1, k), w_all, *carried)
    return outs[0], outs[1], tuple(outs[2:])


def _mm_res_kernel(*refs, n_pairs, n_top):
    lhs, ws = refs[:n_pairs], refs[n_pairs:2 * n_pairs]
    res_refs, out_ref = refs[2 * n_pairs:-1], refs[-1]

    def emit(res_ref):
        acc = res_ref[...]
        for l, w in zip(lhs, ws):
            acc = acc + jnp.dot(l[...], w[...], preferred_element_type=F32)
        out_ref[...] = acc

    if len(res_refs) == 1:
        emit(res_refs[0])
    else:
        i = pl.program_id(0)
        pl.when(i < n_top)(functools.partial(emit, res_refs[0]))
        pl.when(i >= n_top)(functools.partial(emit, res_refs[1]))


def _matmul_residual(lhs_list, w_all, layer, res, *, tm_pref=1024, tn_pref=1024):
    parts = res if isinstance(res, tuple) else (res,)
    m, n = sum(p.shape[0] for p in parts), parts[0].shape[1]
    tm = _tile(parts[0].shape[0], tm_pref)
    while any(p.shape[0] % tm for p in parts):
        tm //= 2
    tn = _tile(n, tn_pref)
    n_top = parts[0].shape[0] // tm
    ks = [l.shape[1] for l in lhs_list]
    starts = [sum(ks[:p]) for p in range(len(ks))]
    assert all(r % k == 0 for r, k in zip(starts, ks)) and sum(ks) == w_all.shape[1]
    vmem = sum(2 * tm * k * 2 + 2 * k * tn * 2 for k in ks) + (2 + 2 * len(parts)) * tm * tn * 4
    res_specs = ([pl.BlockSpec((tm, tn), lambda i, j: (i, j))] if len(parts) == 1 else
                 [pl.BlockSpec((tm, tn), lambda i, j: (jnp.minimum(i, n_top - 1), j)),
                  pl.BlockSpec((tm, tn), lambda i, j: (jnp.maximum(i - n_top, 0), j))])
    in_specs = ([pl.BlockSpec((tm, k), lambda i, j: (i, 0)) for k in ks]
                + [pl.BlockSpec((None, k, tn), lambda i, j, blk=r // k: (layer, blk, j)) for r, k in zip(starts, ks)]
                + res_specs)
    return pl.pallas_call(
        functools.partial(_mm_res_kernel, n_pairs=len(ks), n_top=n_top),
        grid=(m // tm, n // tn),
        in_specs=in_specs,
        out_specs=pl.BlockSpec((tm, tn), lambda i, j: (i, j)),
        out_shape=jax.ShapeDtypeStruct((m, n), F32),
        compiler_params=_params(("arbitrary", "arbitrary"), vmem),
        name="matmul_residual",
    )(*lhs_list, *([w_all] * len(ks)), *parts)


def _mm_res_rows_kernel(top_ref, bottom_ref, w_ref, res_ref, out_ref, *, n_top):
    i = pl.program_id(0)

    def emit(lhs_ref):
        out_ref[...] = res_ref[...] + jnp.dot(lhs_ref[...], w_ref[...], preferred_element_type=F32)

    pl.when(i < n_top)(functools.partial(emit, top_ref))
    pl.when(i >= n_top)(functools.partial(emit, bottom_ref))


def _mm_res_rows_norm_kernel(top_ref, bottom_ref, w_ref, res_ref, g_ref, top_out, bottom_out, *, n_top):
    i = pl.program_id(0)

    def emit(lhs_ref, out_ref):
        y = res_ref[...] + jnp.dot(lhs_ref[...], w_ref[...], preferred_element_type=F32)
        ms = jnp.mean(y * y, axis=-1, keepdims=True)
        out_ref[...] = y * lax.rsqrt(ms + EPS) * g_ref[...]

    pl.when(i < n_top)(functools.partial(emit, top_ref, top_out))
    pl.when(i >= n_top)(functools.partial(emit, bottom_ref, bottom_out))


def _matmul_residual_rows_norm(top, bottom, w_all, layer, res, g, *, tm_pref=512):
    m, n = res.shape
    k = w_all.shape[1]
    tm = _tile(top.shape[0], tm_pref)
    while bottom.shape[0] % tm:
        tm //= 2
    n_top = top.shape[0] // tm
    vmem = 4 * tm * k * 2 + 2 * k * n * 2 + 6 * tm * n * 4
    return pl.pallas_call(
        functools.partial(_mm_res_rows_norm_kernel, n_top=n_top),
        grid=(m // tm,),
        in_specs=[pl.BlockSpec((tm, k), lambda i: (jnp.minimum(i, n_top - 1), 0)),
                  pl.BlockSpec((tm, k), lambda i: (jnp.maximum(i - n_top, 0), 0)),
                  pl.BlockSpec((None, k, n), lambda i: (layer, 0, 0)),
                  pl.BlockSpec((tm, n), lambda i: (i, 0)),
                  pl.BlockSpec((1, n), lambda i: (0, 0))],
        out_specs=[pl.BlockSpec((tm, n), lambda i: (jnp.minimum(i, n_top - 1), 0)),
                   pl.BlockSpec((tm, n), lambda i: (jnp.maximum(i - n_top, 0), 0))],
        out_shape=[jax.ShapeDtypeStruct((top.shape[0], n), F32), jax.ShapeDtypeStruct((bottom.shape[0], n), F32)],
        compiler_params=_params(("arbitrary",), vmem),
        name="matmul_residual_rows_norm",
    )(top, bottom, w_all, res, g.reshape(1, n))


def _matmul_residual_rows(top, bottom, w_all, layer, res, *, tm_pref=1024, tn_pref=1024):
    m, n = res.shape
    k = w_all.shape[1]
    tm = _tile(top.shape[0], tm_pref)
    while bottom.shape[0] % tm:
        tm //= 2
    tn = _tile(n, tn_pref)
    n_top = top.shape[0] // tm
    vmem = 4 * tm * k * 2 + 2 * k * tn * 2 + 4 * tm * tn * 4
    return pl.pallas_call(
        functools.partial(_mm_res_rows_kernel, n_top=n_top),
        grid=(m // tm, n // tn),
        in_specs=[pl.BlockSpec((tm, k), lambda i, j: (jnp.minimum(i, n_top - 1), 0)),
                  pl.BlockSpec((tm, k), lambda i, j: (jnp.maximum(i - n_top, 0), 0)),
                  pl.BlockSpec((None, k, tn), lambda i, j: (layer, 0, j)),
                  pl.BlockSpec((tm, tn), lambda i, j: (i, j))],
        out_specs=pl.BlockSpec((tm, tn), lambda i, j: (i, j)),
        out_shape=jax.ShapeDtypeStruct((m, n), F32),
        compiler_params=_params(("arbitrary", "arbitrary"), vmem),
        name="matmul_residual_rows",
    )(top, bottom, w_all, res)


def _final_norm_kernel(x_ref, g_ref, top_ref, bottom_ref, *, n_top):
    x = x_ref[...]
    ms = jnp.mean(x * x, axis=-1, keepdims=True)
    y = x * lax.rsqrt(ms + EPS) * g_ref[...]
    i = pl.program_id(0)

    @pl.when(i < n_top)
    def _():
        top_ref[...] = y

    @pl.when(i >= n_top)
    def _():
        bottom_ref[...] = y


def _final_norm(x, g, mp):
    m, d = x.shape
    tm = _tile(mp, 256)
    while (m - mp) % tm:
        tm //= 2
    n_top = mp // tm
    return pl.pallas_call(
        functools.partial(_final_norm_kernel, n_top=n_top),
        grid=(m // tm,),
        in_specs=[pl.BlockSpec((tm, d), lambda i: (i, 0)), pl.BlockSpec((1, d), lambda i: (0, 0))],
        out_specs=[pl.BlockSpec((tm, d), lambda i: (jnp.minimum(i, n_top - 1), 0)),
                   pl.BlockSpec((tm, d), lambda i: (jnp.maximum(i - n_top, 0), 0))],
        out_shape=[jax.ShapeDtypeStruct((mp, d), F32), jax.ShapeDtypeStruct((m - mp, d), F32)],
        compiler_params=_params(("arbitrary",), 6 * tm * d * 4),
        name="final_norm",
    )(x, g.reshape(1, d))


SCAN_ROWS = 256
SCAN_STREAMS = 4


def _scan_tiles(a, u, carry):
    r = a.shape[0]
    nt = r // SUBLANES
    a3 = a.reshape(nt, SUBLANES, LANES)
    u3 = u.reshape(nt, SUBLANES, LANES)
    sub = lax.broadcasted_iota(jnp.int32, a3.shape, 1)
    for d in (1, 2, 4):
        keep = sub >= d
        a_prev = pltpu.roll(a3, d, 1)
        u_prev = pltpu.roll(u3, d, 1)
        u3 = jnp.where(keep, a3 * u_prev + u3, u3)
        a3 = jnp.where(keep, a3 * a_prev, a3)
    hs = []
    for t in range(nt):
        h_t = u3[t] + a3[t] * carry
        carry = h_t[SUBLANES - 1:SUBLANES, :]
        hs.append(h_t)
    return jnp.concatenate(hs, axis=0), carry


def _rglru_kernel(xa_ref, ga_ref, h0_ref, sc_ref, cw_ref, cb_ref, wr_ref, br_ref, wi_ref, bi_ref, lam_ref,
                  out_ref, hp_ref, hs_ref, *, bp, t, bs, ts):
    mp = bp * t
    cw = cw_ref[...]
    cb = cb_ref[...]
    half_wr = (0.5 * wr_ref[...]).astype(BF16)
    half_wi = (0.5 * wi_ref[...]).astype(BF16)
    half_br = 0.5 * br_ref[...]
    half_bi = 0.5 * bi_ref[...]
    half_c = (-0.5 * RG_C) * _softplus(-lam_ref[...])

    def conv_taps(tap):
        y = cb + tap(0) * cw[0:1]
        for k in range(1, CONV_W):
            y = y + tap(k) * cw[k:k + 1]
        return y

    def conv(ext, n):
        def tap(k):
            back = CONV_W - 1 - k
            return (ext if back == 0 else pltpu.roll(ext, back, 0))[SUBLANES:]

        return conv_taps(tap)

    def decay_and_input(xc):
        xb = xc.astype(BF16)
        tr = jnp.tanh(jnp.dot(xb, half_wr, preferred_element_type=F32) + half_br)
        ti = jnp.tanh(jnp.dot(xb, half_wi, preferred_element_type=F32) + half_bi)
        log_a = half_c + half_c * tr
        half_x = 0.5 * xc
        gated_x = half_x + half_x * ti
        a = jnp.exp(log_a)
        v = -jnp.tanh(log_a) * (1.0 + a * a)
        root = jnp.where(v > 0.0, v * lax.rsqrt(v), 0.0)
        return a, root * gated_x

    def emit(rows, a, u, carry):
        h, carry = _scan_tiles(a, u, carry)
        out_ref[rows, :] = (h * _silu(ga_ref[rows, :])).astype(out_ref.dtype)
        return carry

    rp = _tile(t, SCAN_ROWS)

    def first_chunk(b):
        rows = pl.ds(b * t, rp)
        ext = jnp.concatenate([jnp.zeros((SUBLANES, LANES), F32), xa_ref[rows, :]], axis=0)
        a, u = decay_and_input(conv(ext, rp))
        return emit(rows, a, u, jnp.zeros((1, LANES), F32))

    def later_chunk(c, carries):
        new = []
        for b in range(bp):
            r0 = b * t + c * rp
            rows = pl.ds(pl.multiple_of(r0, SUBLANES), rp)
            xc = conv_taps(lambda k: xa_ref[pl.ds(r0 - (CONV_W - 1 - k), rp), :])
            a, u = decay_and_input(xc)
            new.append(emit(rows, a, u, carries[b]))
        return tuple(new)

    carries = lax.fori_loop(1, t // rp, later_chunk, tuple(first_chunk(b) for b in range(bp)))
    for b in range(bp):
        hp_ref[b:b + 1, :] = carries[b]

    ns = _tile(bs, SCAN_STREAMS)

    def sample_group(c, _):
        rows = [pl.ds(pl.multiple_of(mp + (c * ns + k) * ts, SUBLANES), ts) for k in range(ns)]
        xc = [conv(jnp.concatenate([sc_ref[c * ns + k], xa_ref[rows[k], :]], axis=0), ts) for k in range(ns)]
        a, u = decay_and_input(jnp.concatenate(xc, axis=0))
        for k in range(ns):
            seg = slice(k * ts, (k + 1) * ts)
            s = c * ns + k
            hs_ref[pl.ds(s, 1), :] = emit(rows[k], a[seg], u[seg], h0_ref[pl.ds(s, 1), :])
        return 0

    lax.fori_loop(0, bs // ns, sample_group, 0)


def _rglru(proj, h0, sc_pad, cw, cb, wr, br, wi, bi, lam, *, bp, t, bs, ts, d_rnn):
    m = proj.shape[0]
    nb = d_rnn // LANES
    row = lambda v: v.reshape(1, d_rnn)
    vec_spec = pl.BlockSpec((1, LANES), lambda n: (0, n))
    w_spec = pl.BlockSpec((None, LANES, LANES), lambda n: (n, 0, 0))
    vmem = 2 * (2 * m * LANES * 4 + m * LANES * 2)
    return pl.pallas_call(
        functools.partial(_rglru_kernel, bp=bp, t=t, bs=bs, ts=ts),
        grid=(nb,),
        in_specs=[pl.BlockSpec((m, LANES), lambda n: (0, n)),
                  pl.BlockSpec((m, LANES), lambda n: (0, nb + n)),
                  pl.BlockSpec((bs, LANES), lambda n: (0, n)),
                  pl.BlockSpec((bs, SUBLANES, LANES), lambda n: (0, 0, n)),
                  pl.BlockSpec((CONV_W, LANES), lambda n: (0, n)),
                  vec_spec, w_spec, vec_spec, w_spec, vec_spec, vec_spec],
        out_specs=[pl.BlockSpec((m, LANES), lambda n: (0, n)),
                   pl.BlockSpec((bp, LANES), lambda n: (0, n)),
                   pl.BlockSpec((bs, LANES), lambda n: (0, n))],
        out_shape=[jax.ShapeDtypeStruct((m, d_rnn), BF16),
                   jax.ShapeDtypeStruct((bp, d_rnn), F32),
                   jax.ShapeDtypeStruct((bs, d_rnn), F32)],
        compiler_params=_params(("arbitrary",), vmem),
        name="rglru_mixer",
    )(proj, proj, h0, sc_pad, cw, row(cb), wr, row(br), wi, row(bi), row(lam))


POOL_TAIL = 16
POOL_ROWS = 128


def _pool_kernel(xb_ref, gb_ref, sp_ref, pw_ref, ps_ref, out_ref, m_scr, tail_scr, *, npb, bpt, ts):
    g = pl.program_id(0)
    rb = pl.program_id(1)
    rbk, gw = xb_ref.shape
    wf = lax.shift_left(jnp.int32(2), g).astype(F32)

    def window_means(ext, n, pos0):
        s2 = ext[1:] + ext[:-1]
        s4 = s2[2:] + s2[:-2]
        s8 = s4[4:] + s4[:-4]
        s16 = s8[8:] + s8[:-8]
        x = ext[POOL_TAIL:]
        win = jnp.where(g == 0, s2[POOL_TAIL - 1:],
                        jnp.where(g == 1, s4[POOL_TAIL - 3:],
                                  jnp.where(g == 2, s8[POOL_TAIL - 7:], s16[POOL_TAIL - 15:])))
        if pos0 is None:
            cnt = wf
        else:
            pos = pos0 + lax.broadcasted_iota(jnp.int32, (n, gw), 0)
            cnt = jnp.minimum(wf, (pos + 1).astype(F32))
        return win / cnt - x

    @pl.when(rb < npb)
    def _():
        blk = rb % bpt

        @pl.when(blk == 0)
        def _():
            tail_scr[...] = jnp.zeros_like(tail_scr)

        rc = _tile(rbk, POOL_ROWS)

        def body(c, tail):
            rows = pl.ds(pl.multiple_of(c * rc, SUBLANES), rc)
            x = xb_ref[rows, :]
            ext = jnp.concatenate([tail, x], axis=0)
            m_scr[rows, :] = window_means(ext, rc, blk * rbk + c * rc).astype(BF16)
            return x[rc - POOL_TAIL:, :]

        tail_scr[...] = lax.fori_loop(0, rbk // rc, body, tail_scr[...])

    @pl.when(rb >= npb)
    def _():
        s0 = (rb - npb) * (rbk // ts)

        def body(s, _):
            rows = pl.ds(pl.multiple_of(s * ts, SUBLANES), ts)
            ext = jnp.concatenate([sp_ref[s0 + s], xb_ref[rows, :]], axis=0)
            m_scr[rows, :] = window_means(ext, ts, None).astype(BF16)
            return 0

        lax.fori_loop(0, rbk // ts, body, 0)

    y = jnp.dot(m_scr[...], pw_ref[...].astype(BF16), preferred_element_type=F32) * ps_ref[...]
    out_ref[...] = (y * _silu(gb_ref[...])).astype(out_ref.dtype)


def _pool(proj, sp_pad, pw, ps, *, bp, t, bs, ts, d_rnn, d_pool):
    m = proj.shape[0]
    ng = len(POOL_WINDOWS)
    gw = d_pool // ng
    ms = bs * ts
    rbk = min(1024, t, ms)
    while t % rbk or ms % rbk:
        rbk //= 2
    npb = bp * t // rbk
    xcol = 2 * d_rnn // gw
    gcol = (2 * d_rnn + d_pool) // gw
    vmem = 2 * (2 * rbk * gw * 4 + rbk * gw * 2) + rbk * gw * 2 + 2 * bs * POOL_TAIL * gw * 4 + 2 * gw * gw * 4
    return pl.pallas_call(
        functools.partial(_pool_kernel, npb=npb, bpt=t // rbk, ts=ts),
        grid=(ng, m // rbk),
        in_specs=[pl.BlockSpec((rbk, gw), lambda g, r: (r, xcol + g)),
                  pl.BlockSpec((rbk, gw), lambda g, r: (r, gcol + g)),
                  pl.BlockSpec((bs, POOL_TAIL, gw), lambda g, r: (0, 0, g)),
                  pl.BlockSpec((None, gw, gw), lambda g, r: (g, 0, 0)),
                  pl.BlockSpec((1, gw), lambda g, r: (0, g))],
        out_specs=pl.BlockSpec((rbk, gw), lambda g, r: (r, g)),
        out_shape=jax.ShapeDtypeStruct((m, d_pool), BF16),
        scratch_shapes=[pltpu.VMEM((rbk, gw), BF16), pltpu.VMEM((POOL_TAIL, gw), F32)],
        compiler_params=_params(("arbitrary", "arbitrary"), vmem),
        name="pool_mixer",
    )(proj, proj, sp_pad, pw, ps.reshape(1, d_pool))


def _suffix_matrix(n):
    j = lax.broadcasted_iota(jnp.int32, (n, n), 0)
    s = lax.broadcasted_iota(jnp.int32, (n, n), 1)
    u = jnp.where(j > s, -1.0, 0.0).astype(BF16)
    return jnp.concatenate([u, u], axis=0)


def _qk(q, k):
    return lax.dot_general(q, k, (((1,), (1,)), ((), ())), preferred_element_type=F32)


def _strictly_earlier(shape):
    return lax.broadcasted_iota(jnp.int32, shape, 1) < lax.broadcasted_iota(jnp.int32, shape, 0)


def _sb_weights(z, uu, carry=None, mask=None):
    sp = _softplus(z)
    spm = sp if mask is None else jnp.where(mask, sp, 0.0)
    hi = spm.astype(BF16)
    lo = (spm - hi.astype(F32)).astype(BF16)
    e = z - sp + jnp.dot(jnp.concatenate([hi, lo], axis=1), uu, preferred_element_type=F32)
    w = jnp.exp(e if carry is None else e + carry)
    if mask is not None:
        w = jnp.where(mask, w, 0.0)
    return w.astype(BF16), -jnp.sum(spm, axis=1, keepdims=True)


def _sb_tile(q, k, v, uu, carry=None, mask=None):
    w, tot = _sb_weights(_qk(q, k), uu, carry, mask)
    return jnp.dot(w, v, preferred_element_type=F32), tot


PROMPT_TILE = 256
PROMPT_ALWAYS = 2


def _attn_prompt_kernel(q_ref, k_ref, v_ref, gt_ref, uu_ref, o_ref, acc_scr, car_scr, worst_ref, *, tq):
    uu = uu_ref[...]
    mask = _strictly_earlier((tq, tq))
    nq = q_ref.shape[0] // tq

    tiles = [(qi, qi - back) for qi in range(nq) for back in range(min(qi, PROMPT_ALWAYS - 1) + 1)]
    z, sp, spm = [], [], []
    for qi, kj in tiles:
        zt = _qk(q_ref[pl.ds(qi * tq, tq), :], k_ref[pl.ds(kj * tq, tq), :])
        st = _softplus(zt)
        z.append(zt)
        sp.append(st)
        spm.append(jnp.where(mask, st, 0.0) if kj == qi else st)
    stacked = jnp.concatenate(spm, axis=0)
    hi = stacked.astype(BF16)
    lo = (stacked - hi.astype(F32)).astype(BF16)
    suffix = jnp.dot(jnp.concatenate([hi, lo], axis=1), uu, preferred_element_type=F32)
    acc, carry = {}, {}
    for n, (qi, kj) in enumerate(tiles):
        e = z[n] - sp[n] + suffix[n * tq:(n + 1) * tq]
        w = jnp.exp(e if kj == qi else e + carry[qi])
        if kj == qi:
            w = jnp.where(mask, w, 0.0)
        pv = jnp.dot(w.astype(BF16), v_ref[pl.ds(kj * tq, tq), :], preferred_element_type=F32)
        tot = -jnp.sum(spm[n], axis=1, keepdims=True)
        acc[qi] = pv if kj == qi else acc[qi] + pv
        carry[qi] = tot if kj == qi else carry[qi] + tot
    for qi in range(nq):
        rows = pl.ds(qi * tq, tq)
        o_ref[rows, :] = (acc[qi] * _silu(gt_ref[rows, :])).astype(o_ref.dtype)
        if qi >= PROMPT_ALWAYS:
            acc_scr[qi] = acc[qi]
            car_scr[qi] = carry[qi]
            worst_ref[qi] = jnp.max(carry[qi])

    def finish(qi, _):
        @pl.when(worst_ref[qi] >= EXP_UNDERFLOW)
        def _():
            rows = pl.ds(pl.multiple_of(qi * tq, tq), tq)
            q = q_ref[rows, :]

            def more(s):
                return jnp.logical_and(s[0] >= 0, jnp.max(s[2]) >= EXP_UNDERFLOW)

            def older(s):
                j, acc, carry = s
                old = pl.ds(pl.multiple_of(j * tq, tq), tq)
                pv, tot = _sb_tile(q, k_ref[old, :], v_ref[old, :], uu, carry)
                return j - 1, acc + pv, carry + tot

            _, acc, _ = lax.while_loop(more, older, (qi - PROMPT_ALWAYS, acc_scr[qi], car_scr[qi]))
            o_ref[rows, :] = (acc * _silu(gt_ref[rows, :])).astype(o_ref.dtype)

        return 0

    lax.fori_loop(PROMPT_ALWAYS, nq, finish, 0)


def _attn_prompt(qkv, gate, *, bp, t, n_heads):
    tq = _tile(t, PROMPT_TILE)
    d_att = n_heads * HEAD_DIM
    nq = t // tq
    state = pltpu.VMEM((nq, tq, LANES), F32), pltpu.VMEM((nq, tq, 1), F32), pltpu.SMEM((nq,), F32)
    always = PROMPT_ALWAYS * nq
    vmem = (2 * (3 * t * LANES * 2 + t * LANES * 4 + 2 * tq * tq * 2 + t * LANES * 2) + 2 * t * LANES * 4
            + 6 * always * tq * tq * 4)
    col = lambda c: pl.BlockSpec((t, LANES), lambda b, h: (b, c * n_heads + h))
    return pl.pallas_call(
        functools.partial(_attn_prompt_kernel, tq=tq),
        grid=(bp, n_heads),
        in_specs=[col(0), col(1), col(2), col(0), pl.BlockSpec((2 * tq, tq), lambda b, h: (0, 0))],
        out_specs=pl.BlockSpec((t, LANES), lambda b, h: (b, h)),
        out_shape=jax.ShapeDtypeStruct((bp * t, d_att), BF16),
        scratch_shapes=list(state),
        compiler_params=_params(("arbitrary", "arbitrary"), vmem),
        name="attn_prompt",
    )(qkv, qkv, qkv, gate, _suffix_matrix(tq))


def _attn_sample_kernel(q_ref, kn_ref, vn_ref, gt_ref, kc_hbm, vc_hbm, un_ref, uc_ref, o_ref,
                        kfirst, vfirst, kmore, vmore, acc_scr, car_scr, sem, *, layer, pc):
    b = pl.program_id(0)
    n_heads, ts = kmore.shape[0], q_ref.shape[0]
    newest = kc_hbm.shape[2] // pc - 1
    slot = b % 2

    def copies(stream, chunk, kdst, vdst, ksem, vsem):
        pos = pl.ds(chunk * pc, pc)
        out = []
        for h in range(n_heads):
            out.append(pltpu.make_async_copy(kc_hbm.at[layer, stream, pos, h, :], kdst.at[h], ksem))
            out.append(pltpu.make_async_copy(vc_hbm.at[layer, stream, pos, h, :], vdst.at[h], vsem))
        return out

    def first_copies(stream, s):
        return copies(stream, newest, kfirst.at[s], vfirst.at[s], sem.at[s, 0], sem.at[s, 1])

    @pl.when(b == 0)
    def _():
        for c in first_copies(0, 0):
            c.start()

    @pl.when(b + 1 < pl.num_programs(0))
    def _():
        for c in first_copies(b + 1, 1 - slot):
            c.start()

    def head(ref, h):
        return ref[:, h * HEAD_DIM:(h + 1) * HEAD_DIM]

    def scores(keys):
        return jnp.concatenate([_qk(head(q_ref, h), keys(h)) for h in range(n_heads)], axis=0)

    def weighted(w, values):
        return jnp.concatenate([jnp.dot(w[h * ts:(h + 1) * ts], values(h), preferred_element_type=F32)
                                for h in range(n_heads)], axis=0)

    pad = jnp.zeros((LANES - ts, HEAD_DIM), BF16)
    query = lax.broadcasted_iota(jnp.int32, (n_heads, ts, LANES), 1).reshape(n_heads * ts, LANES)
    mask = lax.broadcasted_iota(jnp.int32, (n_heads * ts, LANES), 1) < query
    w, carry = _sb_weights(scores(lambda h: jnp.concatenate([head(kn_ref, h), pad], axis=0)),
                           un_ref[...], None, mask)
    acc = weighted(w, lambda h: jnp.concatenate([head(vn_ref, h), pad], axis=0))

    def chunk(kbuf, vbuf, acc, carry):
        w, tot = _sb_weights(scores(lambda h: kbuf[h].astype(BF16)), uc_ref[...], carry)
        return acc + weighted(w, lambda h: vbuf[h].astype(BF16)), carry + tot

    for c in first_copies(b, slot):
        c.wait()
    acc_scr[...], car_scr[...] = chunk(kfirst.at[slot], vfirst.at[slot], acc, carry)

    def more(s):
        return jnp.logical_and(s[0] >= 0, s[1] >= EXP_UNDERFLOW)

    def older(s):
        cs = copies(b, s[0], kmore, vmore, sem.at[2, 0], sem.at[2, 1])
        for c in cs:
            c.start()
        for c in cs:
            c.wait()
        acc_scr[...], car_scr[...] = chunk(kmore, vmore, acc_scr[...], car_scr[...])
        return s[0] - 1, jnp.max(car_scr[...])

    lax.while_loop(more, older, (newest - 1, jnp.max(car_scr[...])))

    for h in range(n_heads):
        o_ref[:, h * HEAD_DIM:(h + 1) * HEAD_DIM] = (acc_scr[pl.ds(h * ts, ts), :]
                                                     * _silu(head(gt_ref, h))).astype(o_ref.dtype)


def _attn_sample(qkv, gate, cache_k, cache_v, *, layer, mp, bs, ts, n_heads):
    p = cache_k.shape[2]
    d_att = n_heads * HEAD_DIM
    pc = _tile(p, 256)
    r0 = mp // ts
    chunk_bytes = n_heads * pc * HEAD_DIM * 4
    vmem = (2 * (3 * ts * d_att * 2 + ts * d_att * 4 + ts * d_att * 2) + 6 * chunk_bytes
            + 2 * n_heads * ts * LANES * 4 + 2 * 2 * (LANES * LANES + pc * pc) * 2)
    new_spec = lambda c: pl.BlockSpec((ts, d_att), lambda b: (r0 + b, c))
    chunk_buf = lambda n: pltpu.VMEM(n + (n_heads, pc, HEAD_DIM), F32)
    return pl.pallas_call(
        functools.partial(_attn_sample_kernel, layer=layer, pc=pc),
        grid=(bs,),
        in_specs=[new_spec(0), new_spec(1), new_spec(2), new_spec(0),
                  pl.BlockSpec(memory_space=pl.ANY), pl.BlockSpec(memory_space=pl.ANY),
                  pl.BlockSpec((2 * LANES, LANES), lambda b: (0, 0)),
                  pl.BlockSpec((2 * pc, pc), lambda b: (0, 0))],
        out_specs=pl.BlockSpec((ts, d_att), lambda b: (b, 0)),
        out_shape=jax.ShapeDtypeStruct((bs * ts, d_att), BF16),
        scratch_shapes=[chunk_buf((2,)), chunk_buf((2,)), chunk_buf(()), chunk_buf(()),
                        pltpu.VMEM((n_heads * ts, LANES), F32), pltpu.VMEM((n_heads * ts, 1), F32),
                        pltpu.SemaphoreType.DMA((3, 2))],
        compiler_params=_params(("arbitrary",), vmem),
        name="attn_sample",
    )(qkv, qkv, qkv, gate, cache_k, cache_v, _suffix_matrix(LANES), _suffix_matrix(pc))


@jax.jit
def _step(x_prompt, x_sample, cache_k, cache_v, state_h, state_conv, state_pool,
          norm_rec, w_in_rec, conv_w, conv_b, gate_r_w, gate_r_b, gate_i_w, gate_i_b, rg_lambda,
          pool_w, pool_scale, w_out_rec, norm_att, w_in_att, w_out_att, norm_final):
    bp, t, d = x_prompt.shape
    bs, ts, _ = x_sample.shape
    n_rec, n_att = norm_rec.shape[0], norm_att.shape[0]
    d_rnn = state_h.shape[-1]
    d_pool = state_pool.shape[-1]
    n_heads = cache_k.shape[3]
    mp = bp * t
    assert ts >= POOL_BUF and ts % SUBLANES == 0 and t % ts == 0 and cache_k.shape[4] == HEAD_DIM
    assert gate_r_w.shape[2] == LANES and d_pool // len(POOL_WINDOWS) == 2 * LANES

    x = (x_prompt.reshape(mp, d), x_sample.reshape(bs * ts, d))
    sc_pad = jnp.pad(state_conv, ((0, 0), (0, 0), (SUBLANES - (CONV_W - 1), 0), (0, 0)))
    sp_pad = jnp.pad(state_pool, ((0, 0), (0, 0), (POOL_TAIL - POOL_BUF, 0), (0, 0)))
    w_in_rec, w_out_rec, w_in_att, w_out_att = (w.astype(BF16) for w in (w_in_rec, w_out_rec, w_in_att, w_out_att))

    outs = {k: [] for k in ("hp", "cp", "pp", "hs", "cs", "ps")}
    new_kv = None
    for layer in range(n_rec + n_att):
        j = layer // 2
        if layer % 2 == 0:
            proj = (_norm_matmul_rows(*x, norm_rec[j], w_in_rec, j) if isinstance(x, tuple)
                    else _norm_matmul(x, norm_rec[j], w_in_rec, j))
            ya, hp, hs = _rglru(proj, state_h[j], sc_pad[j], conv_w[j], conv_b[j], gate_r_w[j], gate_r_b[j],
                                gate_i_w[j], gate_i_b[j], rg_lambda[j], bp=bp, t=t, bs=bs, ts=ts, d_rnn=d_rnn)
            yb = _pool(proj, sp_pad[j], pool_w[j], pool_scale[j], bp=bp, t=t, bs=bs, ts=ts,
                       d_rnn=d_rnn, d_pool=d_pool)
            x = _matmul_residual([ya, yb], w_out_rec, j, x)
            frames = proj.reshape(-1, ts, proj.shape[1])

            def last_rows(n, c0, c1):
                prompt = lax.slice(frames, (t // ts - 1, ts - n, c0), (mp // ts, ts, c1), (t // ts, 1, 1))
                sample = lax.slice(frames, (mp // ts, ts - n, c0), (frames.shape[0], ts, c1))
                return prompt, sample

            cp, cs = last_rows(CONV_W - 1, 0, d_rnn)
            pp, ps = last_rows(POOL_BUF, 2 * d_rnn, 2 * d_rnn + d_pool)
            for key, val in (("hp", hp), ("hs", hs), ("cp", cp), ("cs", cs), ("pp", pp), ("ps", ps)):
                outs[key].append(val)
        else:
            qkv, gate, new_kv = _norm_matmul_att(x, norm_att[j], w_in_att, new_kv, layer=j, mp=mp, n_heads=n_heads)
            op = _attn_prompt(qkv, gate, bp=bp, t=t, n_heads=n_heads)
            os_ = _attn_sample(qkv, gate, cache_k, cache_v, layer=j, mp=mp, bs=bs, ts=ts, n_heads=n_heads)
            if layer + 1 < n_rec + n_att:
                x = _matmul_residual_rows(op, os_, w_out_att, j, x)
            else:
                y_prompt, y_sample = _matmul_residual_rows_norm(op, os_, w_out_att, j, x, norm_final)

    if (n_rec + n_att) % 2:
        y_prompt, y_sample = _final_norm(x, norm_final, mp)
    st = {k: jnp.stack(v) for k, v in outs.items()}
    kp, vp, ks, vs = new_kv
    prompt_shape = (n_att, bp, t, n_heads, HEAD_DIM)
    sample_shape = (n_att, bs, ts, n_heads, HEAD_DIM)
    return (y_prompt.reshape(bp, t, d), y_sample.reshape(bs, ts, d),
            kp.reshape(prompt_shape), vp.reshape(prompt_shape), st["hp"], st["cp"], st["pp"],
            ks.reshape(sample_shape), vs.reshape(sample_shape), st["hs"], st["cs"], st["ps"])


def kernel(x_prompt, x_sample, cache_k, cache_v, state_h, state_conv, state_pool, norm_rec, w_in_rec, conv_w, conv_b, gate_r_w, gate_r_b, gate_i_w, gate_i_b, rg_lambda, pool_w, pool_scale, w_out_rec, norm_att, w_in_att, w_out_att, norm_final):
    return _step(x_prompt, x_sample, cache_k, cache_v, state_h, state_conv, state_pool, norm_rec, w_in_rec,
                 conv_w, conv_b, gate_r_w, gate_r_b, gate_i_w, gate_i_b, rg_lambda, pool_w, pool_scale,
                 w_out_rec, norm_att, w_in_att, w_out_att, norm_final)
```

```python
import functools

import jax
import jax.numpy as jnp
from jax import lax
from jax.experimental import pallas as pl
from jax.experimental.pallas import tpu as pltpu

F32 = jnp.float32
BF16 = jnp.bfloat16

EPS = 1e-6
RG_C = 8.0
CONV_W = 4
POOL_WINDOWS = (2, 4, 8, 16)
POOL_BUF = max(POOL_WINDOWS) - 1
HEAD_DIM = 128
ATT_SCALE = HEAD_DIM ** -0.5
EXP_UNDERFLOW = -105.0

LANES = 128
SUBLANES = 8
VMEM_LIMIT_CAP = 60000 * 1024
VMEM_SLACK = 8 * 1024 * 1024


def _params(semantics, buffer_bytes):
    limit = min(VMEM_LIMIT_CAP, buffer_bytes + VMEM_SLACK)
    return pltpu.CompilerParams(dimension_semantics=semantics, vmem_limit_bytes=limit)


def _tile(n, pref):
    t = min(n, pref)
    while n % t:
        t //= 2
    return t


def _silu(x):
    half = 0.5 * x
    return half + half * jnp.tanh(half)


LOG2_E = 1.4426950408889634


def _softplus(x):
    return jnp.maximum(x, 0.0) + jnp.log(1.0 + jnp.exp2(jnp.abs(x) * -LOG2_E))


NORM_ROWS = 128


def _normalize_rows(x_ref, g_ref, xn_ref):
    @pl.when(pl.program_id(1) == 0)
    def _():
        g = g_ref[...]

        def body(c, _):
            rows = pl.ds(pl.multiple_of(c * NORM_ROWS, NORM_ROWS), NORM_ROWS)
            x = x_ref[rows, :]
            ms = jnp.mean(x * x, axis=-1, keepdims=True)
            xn_ref[rows, :] = (x * lax.rsqrt(ms + EPS) * g).astype(BF16)
            return 0

        lax.fori_loop(0, x_ref.shape[0] // NORM_ROWS, body, 0)


def _norm_mm_kernel(x_ref, g_ref, w_ref, o_ref, xn_ref):
    _normalize_rows(x_ref, g_ref, xn_ref)
    o_ref[...] = jnp.dot(xn_ref[...], w_ref[...], preferred_element_type=F32)


def _norm_mm_rows_kernel(top_ref, bottom_ref, g_ref, w_ref, o_ref, xn_ref, *, n_top):
    i = pl.program_id(0)
    pl.when(i < n_top)(functools.partial(_normalize_rows, top_ref, g_ref, xn_ref))
    pl.when(i >= n_top)(functools.partial(_normalize_rows, bottom_ref, g_ref, xn_ref))
    o_ref[...] = jnp.dot(xn_ref[...], w_ref[...], preferred_element_type=F32)


def _norm_matmul(x, g, w_all, layer, *, tm_pref=1024, tn_pref=1024):
    m, k = x.shape
    n = w_all.shape[2]
    tm, tn = _tile(m, tm_pref), _tile(n, tn_pref)
    vmem = 2 * tm * k * 4 + tm * k * 2 + 2 * k * tn * 2 + 2 * tm * tn * 4
    return pl.pallas_call(
        _norm_mm_kernel,
        grid=(m // tm, n // tn),
        in_specs=[pl.BlockSpec((tm, k), lambda i, j: (i, 0)),
                  pl.BlockSpec((1, k), lambda i, j: (0, 0)),
                  pl.BlockSpec((None, k, tn), lambda i, j: (layer, 0, j))],
        out_specs=pl.BlockSpec((tm, tn), lambda i, j: (i, j)),
        out_shape=jax.ShapeDtypeStruct((m, n), F32),
        scratch_shapes=[pltpu.VMEM((tm, k), BF16)],
        compiler_params=_params(("arbitrary", "arbitrary"), vmem),
        name="norm_matmul",
    )(x, g.reshape(1, k), w_all)


def _norm_matmul_rows(top, bottom, g, w_all, layer, *, tm_pref=1024, tn_pref=1024):
    k = top.shape[1]
    m, n = top.shape[0] + bottom.shape[0], w_all.shape[2]
    tm = _tile(top.shape[0], tm_pref)
    while bottom.shape[0] % tm:
        tm //= 2
    tn = _tile(n, tn_pref)
    n_top = top.shape[0] // tm
    vmem = 3 * tm * k * 4 + tm * k * 2 + 2 * k * tn * 2 + 2 * tm * tn * 4
    return pl.pallas_call(
        functools.partial(_norm_mm_rows_kernel, n_top=n_top),
        grid=(m // tm, n // tn),
        in_specs=[pl.BlockSpec((tm, k), lambda i, j: (jnp.minimum(i, n_top - 1), 0)),
                  pl.BlockSpec((tm, k), lambda i, j: (jnp.maximum(i - n_top, 0), 0),
                               pipeline_mode=pl.Buffered(1)),
                  pl.BlockSpec((1, k), lambda i, j: (0, 0)),
                  pl.BlockSpec((None, k, tn), lambda i, j: (layer, 0, j))],
        out_specs=pl.BlockSpec((tm, tn), lambda i, j: (i, j)),
        out_shape=jax.ShapeDtypeStruct((m, n), F32),
        scratch_shapes=[pltpu.VMEM((tm, k), BF16)],
        compiler_params=_params(("arbitrary", "arbitrary"), vmem),
        name="norm_matmul_rows",
    )(top, bottom, g.reshape(1, k), w_all)


def _norm_mm_att_kernel(x_ref, g_ref, w_ref, *refs, layer, n_top, per):
    qkv_ref, gt_ref, kp_hbm, vp_hbm, ks_hbm, vs_hbm, xn_ref, kv_scr, sem = refs[-9:]
    layers = (layer,) if len(refs) > 9 else range(kp_hbm.shape[0])
    _normalize_rows(x_ref, g_ref, xn_ref)
    i, j = pl.program_id(0), pl.program_id(1)
    tm, tn = qkv_ref.shape
    hpt = tn // HEAD_DIM
    top = i < n_top

    def project():
        return jnp.dot(xn_ref[...], w_ref[...], preferred_element_type=F32)

    def copies(jj, prompt_rows):
        slot = (jj - per) % 2
        dst = ((kp_hbm, ks_hbm), (vp_hbm, vs_hbm))[jj // per - 1][0 if prompt_rows else 1]
        rows = pl.ds(i * tm if prompt_rows else (i - n_top) * tm, tm)
        return [pltpu.make_async_copy(kv_scr.at[slot, :, pl.ds(h * HEAD_DIM, HEAD_DIM)],
                                      dst.at[l, rows, (jj % per) * hpt + h, :], sem.at[slot])
                for l in layers for h in range(hpt)]

    def for_rows(jj, action):
        @pl.when(top)
        def _():
            for c in copies(jj, True):
                action(c)

        @pl.when(jnp.logical_not(top))
        def _():
            for c in copies(jj, False):
                action(c)

    @pl.when(j < per)
    def _():
        qkv_ref[...] = (project() * ATT_SCALE).astype(BF16)

    for jj in range(per, 3 * per + 2):
        @pl.when(j == jj)
        def _(jj=jj):
            if jj - 2 >= per:
                for_rows(jj - 2, lambda c: c.wait())
            if jj < 3 * per:
                slot = (jj - per) % 2
                kv_scr[slot] = project()
                qkv_ref[...] = kv_scr[slot].astype(BF16)
                for_rows(jj, lambda c: c.start())

    @pl.when(j >= 3 * per)
    def _():
        gt_ref[...] = project()


def _norm_matmul_att(x, g, w_all, new_kv, *, layer, mp, n_heads, tm_pref=1024):
    m, k = x.shape
    n_layers = w_all.shape[0]
    d_att = n_heads * HEAD_DIM
    tn = d_att // 2
    per = d_att // tn
    assert per >= 2 and w_all.shape[2] == 4 * d_att
    tm = _tile(mp, tm_pref)
    while (m - mp) % tm:
        tm //= 2
    any_spec = pl.BlockSpec(memory_space=pl.ANY)
    carried = () if new_kv is None else tuple(new_kv)
    kv_shapes = [jax.ShapeDtypeStruct((n_layers, rows, n_heads, HEAD_DIM), F32) for rows in (mp, mp, m - mp, m - mp)]
    vmem = 2 * tm * k * 4 + tm * k * 2 + 2 * k * tn * 2 + 2 * tm * tn * (2 + 4) + 3 * tm * tn * 4
    outs = pl.pallas_call(
        functools.partial(_norm_mm_att_kernel, layer=layer, n_top=mp // tm, per=per),
        grid=(m // tm, 4 * per),
        in_specs=[pl.BlockSpec((tm, k), lambda i, j: (i, 0)),
                  pl.BlockSpec((1, k), lambda i, j: (0, 0)),
                  pl.BlockSpec((None, k, tn), lambda i, j: (layer, 0, j))] + [any_spec] * len(carried),
        out_specs=[pl.BlockSpec((tm, tn), lambda i, j: (i, jnp.minimum(j, 3 * per - 1))),
                   pl.BlockSpec((tm, tn), lambda i, j: (i, jnp.maximum(j - 3 * per, 0))),
                   any_spec, any_spec, any_spec, any_spec],
        out_shape=[jax.ShapeDtypeStruct((m, 3 * d_att), BF16), jax.ShapeDtypeStruct((m, d_att), F32)] + kv_shapes,
        input_output_aliases={3 + n: 2 + n for n in range(len(carried))},
        scratch_shapes=[pltpu.VMEM((tm, k), BF16), pltpu.VMEM((2, tm, tn), F32),
                        pltpu.SemaphoreType.DMA((2,))],
        compiler_params=_params(("arbitrary", "arbitrary"), vmem),
        name="norm_matmul_att",
    )(x, g.reshape(1, k), w_all, *carried)
    return outs[0], outs[1], tuple(outs[2:])


def _mm_res_kernel(*refs, n_pairs, n_top):
    lhs, ws = refs[:n_pairs], refs[n_pairs:2 * n_pairs]
    res_refs, out_ref = refs[2 * n_pairs:-1], refs[-1]

    def emit(res_ref):
        acc = res_ref[...]
        for l, w in zip(lhs, ws):
            acc = acc + jnp.dot(l[...], w[...], preferred_element_type=F32)
        out_ref[...] = acc

    if len(res_refs) == 1:
        emit(res_refs[0])
    else:
        i = pl.program_id(0)
        pl.when(i < n_top)(functools.partial(emit, res_refs[0]))
        pl.when(i >= n_top)(functools.partial(emit, res_refs[1]))


def _matmul_residual(lhs_list, w_all, layer, res, *, tm_pref=1024, tn_pref=1024):
    parts = res if isinstance(res, tuple) else (res,)
    m, n = sum(p.shape[0] for p in parts), parts[0].shape[1]
    tm = _tile(parts[0].shape[0], tm_pref)
    while any(p.shape[0] % tm for p in parts):
        tm //= 2
    tn = _tile(n, tn_pref)
    n_top = parts[0].shape[0] // tm
    ks = [l.shape[1] for l in lhs_list]
    starts = [sum(ks[:p]) for p in range(len(ks))]
    assert all(r % k == 0 for r, k in zip(starts, ks)) and sum(ks) == w_all.shape[1]
    vmem = sum(2 * tm * k * 2 + 2 * k * tn * 2 for k in ks) + (2 + 2 * len(parts)) * tm * tn * 4
    res_specs = ([pl.BlockSpec((tm, tn), lambda i, j: (i, j))] if len(parts) == 1 else
                 [pl.BlockSpec((tm, tn), lambda i, j: (jnp.minimum(i, n_top - 1), j)),
                  pl.BlockSpec((tm, tn), lambda i, j: (jnp.maximum(i - n_top, 0), j))])
    in_specs = ([pl.BlockSpec((tm, k), lambda i, j: (i, 0)) for k in ks]
                + [pl.BlockSpec((None, k, tn), lambda i, j, blk=r // k: (layer, blk, j)) for r, k in zip(starts, ks)]
                + res_specs)
    return pl.pallas_call(
        functools.partial(_mm_res_kernel, n_pairs=len(ks), n_top=n_top),
        grid=(m // tm, n // tn),
        in_specs=in_specs,
        out_specs=pl.BlockSpec((tm, tn), lambda i, j: (i, j)),
        out_shape=jax.ShapeDtypeStruct((m, n), F32),
        compiler_params=_params(("arbitrary", "arbitrary"), vmem),
        name="matmul_residual",
    )(*lhs_list, *([w_all] * len(ks)), *parts)


def _mm_res_rows_kernel(top_ref, bottom_ref, w_ref, res_ref, out_ref, *, n_top):
    i = pl.program_id(0)

    def emit(lhs_ref):
        out_ref[...] = res_ref[...] + jnp.dot(lhs_ref[...], w_ref[...], preferred_element_type=F32)

    pl.when(i < n_top)(functools.partial(emit, top_ref))
    pl.when(i >= n_top)(functools.partial(emit, bottom_ref))


def _mm_res_rows_norm_kernel(top_ref, bottom_ref, w_ref, res_ref, g_ref, top_out, bottom_out, *, n_top):
    i = pl.program_id(0)

    def emit(lhs_ref, out_ref):
        y = res_ref[...] + jnp.dot(lhs_ref[...], w_ref[...], preferred_element_type=F32)
        ms = jnp.mean(y * y, axis=-1, keepdims=True)
        out_ref[...] = y * lax.rsqrt(ms + EPS) * g_ref[...]

    pl.when(i < n_top)(functools.partial(emit, top_ref, top_out))
    pl.when(i >= n_top)(functools.partial(emit, bottom_ref, bottom_out))


def _matmul_residual_rows_norm(top, bottom, w_all, layer, res, g, *, tm_pref=512):
    m, n = res.shape
    k = w_all.shape[1]
    tm = _tile(top.shape[0], tm_pref)
    while bottom.shape[0] % tm:
        tm //= 2
    n_top = top.shape[0] // tm
    vmem = 4 * tm * k * 2 + 2 * k * n * 2 + 6 * tm * n * 4
    return pl.pallas_call(
        functools.partial(_mm_res_rows_norm_kernel, n_top=n_top),
        grid=(m // tm,),
        in_specs=[pl.BlockSpec((tm, k), lambda i: (jnp.minimum(i, n_top - 1), 0)),
                  pl.BlockSpec((tm, k), lambda i: (jnp.maximum(i - n_top, 0), 0)),
                  pl.BlockSpec((None, k, n), lambda i: (layer, 0, 0)),
                  pl.BlockSpec((tm, n), lambda i: (i, 0)),
                  pl.BlockSpec((1, n), lambda i: (0, 0))],
        out_specs=[pl.BlockSpec((tm, n), lambda i: (jnp.minimum(i, n_top - 1), 0)),
                   pl.BlockSpec((tm, n), lambda i: (jnp.maximum(i - n_top, 0), 0))],
        out_shape=[jax.ShapeDtypeStruct((top.shape[0], n), F32), jax.ShapeDtypeStruct((bottom.shape[0], n), F32)],
        compiler_params=_params(("arbitrary",), vmem),
        name="matmul_residual_rows_norm",
    )(top, bottom, w_all, res, g.reshape(1, n))


def _matmul_residual_rows(top, bottom, w_all, layer, res, *, tm_pref=1024, tn_pref=1024):
    m, n = res.shape
    k = w_all.shape[1]
    tm = _tile(top.shape[0], tm_pref)
    while bottom.shape[0] % tm:
        tm //= 2
    tn = _tile(n, tn_pref)
    n_top = top.shape[0] // tm
    vmem = 4 * tm * k * 2 + 2 * k * tn * 2 + 4 * tm * tn * 4
    return pl.pallas_call(
        functools.partial(_mm_res_rows_kernel, n_top=n_top),
        grid=(m // tm, n // tn),
        in_specs=[pl.BlockSpec((tm, k), lambda i, j: (jnp.minimum(i, n_top - 1), 0)),
                  pl.BlockSpec((tm, k), lambda i, j: (jnp.maximum(i - n_top, 0), 0)),
                  pl.BlockSpec((None, k, tn), lambda i, j: (layer, 0, j)),
                  pl.BlockSpec((tm, tn), lambda i, j: (i, j))],
        out_specs=pl.BlockSpec((tm, tn), lambda i, j: (i, j)),
        out_shape=jax.ShapeDtypeStruct((m, n), F32),
        compiler_params=_params(("arbitrary", "arbitrary"), vmem),
        name="matmul_residual_rows",
    )(top, bottom, w_all, res)


def _final_norm_kernel(x_ref, g_ref, top_ref, bottom_ref, *, n_top):
    x = x_ref[...]
    ms = jnp.mean(x * x, axis=-1, keepdims=True)
    y = x * lax.rsqrt(ms + EPS) * g_ref[...]
    i = pl.program_id(0)

    @pl.when(i < n_top)
    def _():
        top_ref[...] = y

    @pl.when(i >= n_top)
    def _():
        bottom_ref[...] = y


def _final_norm(x, g, mp):
    m, d = x.shape
    tm = _tile(mp, 256)
    while (m - mp) % tm:
        tm //= 2
    n_top = mp // tm
    return pl.pallas_call(
        functools.partial(_final_norm_kernel, n_top=n_top),
        grid=(m // tm,),
        in_specs=[pl.BlockSpec((tm, d), lambda i: (i, 0)), pl.BlockSpec((1, d), lambda i: (0, 0))],
        out_specs=[pl.BlockSpec((tm, d), lambda i: (jnp.minimum(i, n_top - 1), 0)),
                   pl.BlockSpec((tm, d), lambda i: (jnp.maximum(i - n_top, 0), 0))],
        out_shape=[jax.ShapeDtypeStruct((mp, d), F32), jax.ShapeDtypeStruct((m - mp, d), F32)],
        compiler_params=_params(("arbitrary",), 6 * tm * d * 4),
        name="final_norm",
    )(x, g.reshape(1, d))


SCAN_ROWS = 256
SCAN_STREAMS = 8


def _scan_tiles(a, u, carry):
    r = a.shape[0]
    nt = r // SUBLANES
    a3 = a.reshape(nt, SUBLANES, LANES)
    u3 = u.reshape(nt, SUBLANES, LANES)
    sub = lax.broadcasted_iota(jnp.int32, a3.shape, 1)
    for d in (1, 2, 4):
        keep = sub >= d
        a_prev = pltpu.roll(a3, d, 1)
        u_prev = pltpu.roll(u3, d, 1)
        u3 = jnp.where(keep, a3 * u_prev + u3, u3)
        a3 = jnp.where(keep, a3 * a_prev, a3)
    hs = []
    for t in range(nt):
        h_t = u3[t] + a3[t] * carry
        carry = h_t[SUBLANES - 1:SUBLANES, :]
        hs.append(h_t)
    return jnp.concatenate(hs, axis=0), carry


def _rglru_kernel(xa_ref, ga_ref, h0_ref, sc_ref, cw_ref, cb_ref, wr_ref, br_ref, wi_ref, bi_ref, lam_ref,
                  out_ref, hp_ref, hs_ref, *, bp, t, bs, ts):
    mp = bp * t
    cw = cw_ref[...]
    cb = cb_ref[...]
    half_wr = (0.5 * wr_ref[...]).astype(BF16)
    half_wi = (0.5 * wi_ref[...]).astype(BF16)
    half_br = 0.5 * br_ref[...]
    half_bi = 0.5 * bi_ref[...]
    half_c = (-0.5 * RG_C) * _softplus(-lam_ref[...])

    def conv_taps(tap):
        y = cb + tap(0) * cw[0:1]
        for k in range(1, CONV_W):
            y = y + tap(k) * cw[k:k + 1]
        return y

    def conv(ext, n):
        def tap(k):
            back = CONV_W - 1 - k
            return (ext if back == 0 else pltpu.roll(ext, back, 0))[SUBLANES:]

        return conv_taps(tap)

    def decay_and_input(xc):
        xb = xc.astype(BF16)
        tr = jnp.tanh(jnp.dot(xb, half_wr, preferred_element_type=F32) + half_br)
        ti = jnp.tanh(jnp.dot(xb, half_wi, preferred_element_type=F32) + half_bi)
        log_a = half_c + half_c * tr
        half_x = 0.5 * xc
        gated_x = half_x + half_x * ti
        a = jnp.exp(log_a)
        v = -jnp.tanh(log_a) * (1.0 + a * a)
        root = jnp.where(v > 0.0, v * lax.rsqrt(v), 0.0)
        return a, root * gated_x

    def emit(rows, a, u, carry):
        h, carry = _scan_tiles(a, u, carry)
        out_ref[rows, :] = (h * _silu(ga_ref[rows, :])).astype(out_ref.dtype)
        return carry

    rp = _tile(t, SCAN_ROWS)

    def first_chunk(b):
        rows = pl.ds(b * t, rp)
        ext = jnp.concatenate([jnp.zeros((SUBLANES, LANES), F32), xa_ref[rows, :]], axis=0)
        a, u = decay_and_input(conv(ext, rp))
        return emit(rows, a, u, jnp.zeros((1, LANES), F32))

    def later_chunk(c, carries):
        new = []
        for b in range(bp):
            r0 = b * t + c * rp
            rows = pl.ds(pl.multiple_of(r0, SUBLANES), rp)
            xc = conv_taps(lambda k: xa_ref[pl.ds(r0 - (CONV_W - 1 - k), rp), :])
            a, u = decay_and_input(xc)
            new.append(emit(rows, a, u, carries[b]))
        return tuple(new)

    carries = lax.fori_loop(1, t // rp, later_chunk, tuple(first_chunk(b) for b in range(bp)))
    for b in range(bp):
        hp_ref[b:b + 1, :] = carries[b]

    ns = _tile(bs, SCAN_STREAMS)

    def sample_group(c, _):
        rows = [pl.ds(pl.multiple_of(mp + (c * ns + k) * ts, SUBLANES), ts) for k in range(ns)]
        xc = [conv(jnp.concatenate([sc_ref[c * ns + k], xa_ref[rows[k], :]], axis=0), ts) for k in range(ns)]
        a, u = decay_and_input(jnp.concatenate(xc, axis=0))
        for k in range(ns):
            seg = slice(k * ts, (k + 1) * ts)
            s = c * ns + k
            hs_ref[pl.ds(s, 1), :] = emit(rows[k], a[seg], u[seg], h0_ref[pl.ds(s, 1), :])
        return 0

    lax.fori_loop(0, bs // ns, sample_group, 0)


def _rglru(proj, h0, sc_pad, cw, cb, wr, br, wi, bi, lam, *, bp, t, bs, ts, d_rnn):
    m = proj.shape[0]
    nb = d_rnn // LANES
    row = lambda v: v.reshape(1, d_rnn)
    vec_spec = pl.BlockSpec((1, LANES), lambda n: (0, n))
    w_spec = pl.BlockSpec((None, LANES, LANES), lambda n: (n, 0, 0))
    vmem = 2 * (2 * m * LANES * 4 + m * LANES * 2)
    return pl.pallas_call(
        functools.partial(_rglru_kernel, bp=bp, t=t, bs=bs, ts=ts),
        grid=(nb,),
        in_specs=[pl.BlockSpec((m, LANES), lambda n: (0, n)),
                  pl.BlockSpec((m, LANES), lambda n: (0, nb + n)),
                  pl.BlockSpec((bs, LANES), lambda n: (0, n)),
                  pl.BlockSpec((bs, SUBLANES, LANES), lambda n: (0, 0, n)),
                  pl.BlockSpec((CONV_W, LANES), lambda n: (0, n)),
                  vec_spec, w_spec, vec_spec, w_spec, vec_spec, vec_spec],
        out_specs=[pl.BlockSpec((m, LANES), lambda n: (0, n)),
                   pl.BlockSpec((bp, LANES), lambda n: (0, n)),
                   pl.BlockSpec((bs, LANES), lambda n: (0, n))],
        out_shape=[jax.ShapeDtypeStruct((m, d_rnn), BF16),
                   jax.ShapeDtypeStruct((bp, d_rnn), F32),
                   jax.ShapeDtypeStruct((bs, d_rnn), F32)],
        compiler_params=_params(("arbitrary",), vmem),
        name="rglru_mixer",
    )(proj, proj, h0, sc_pad, cw, row(cb), wr, row(br), wi, row(bi), row(lam))


POOL_TAIL = 16
POOL_ROWS = 128


def _pool_kernel(xb_ref, gb_ref, sp_ref, pw_ref, ps_ref, out_ref, m_scr, tail_scr, *, npb, bpt, ts):
    g = pl.program_id(0)
    rb = pl.program_id(1)
    rbk, gw = xb_ref.shape
    wf = lax.shift_left(jnp.int32(2), g).astype(F32)

    def window_means(ext, n, pos0):
        s2 = ext[1:] + ext[:-1]
        s4 = s2[2:] + s2[:-2]
        s8 = s4[4:] + s4[:-4]
        s16 = s8[8:] + s8[:-8]
        x = ext[POOL_TAIL:]
        win = jnp.where(g == 0, s2[POOL_TAIL - 1:],
                        jnp.where(g == 1, s4[POOL_TAIL - 3:],
                                  jnp.where(g == 2, s8[POOL_TAIL - 7:], s16[POOL_TAIL - 15:])))
        if pos0 is None:
            cnt = wf
        else:
            pos = pos0 + lax.broadcasted_iota(jnp.int32, (n, gw), 0)
            cnt = jnp.minimum(wf, (pos + 1).astype(F32))
        return win / cnt - x

    @pl.when(rb < npb)
    def _():
        blk = rb % bpt

        @pl.when(blk == 0)
        def _():
            tail_scr[...] = jnp.zeros_like(tail_scr)

        rc = _tile(rbk, POOL_ROWS)

        def body(c, tail):
            rows = pl.ds(pl.multiple_of(c * rc, SUBLANES), rc)
            x = xb_ref[rows, :]
            ext = jnp.concatenate([tail, x], axis=0)
            m_scr[rows, :] = window_means(ext, rc, blk * rbk + c * rc).astype(BF16)
            return x[rc - POOL_TAIL:, :]

        tail_scr[...] = lax.fori_loop(0, rbk // rc, body, tail_scr[...])

    @pl.when(rb >= npb)
    def _():
        s0 = (rb - npb) * (rbk // ts)

        def body(s, _):
            rows = pl.ds(pl.multiple_of(s * ts, SUBLANES), ts)
            ext = jnp.concatenate([sp_ref[s0 + s], xb_ref[rows, :]], axis=0)
            m_scr[rows, :] = window_means(ext, ts, None).astype(BF16)
            return 0

        lax.fori_loop(0, rbk // ts, body, 0)

    y = jnp.dot(m_scr[...], pw_ref[...].astype(BF16), preferred_element_type=F32) * ps_ref[...]
    out_ref[...] = (y * _silu(gb_ref[...])).astype(out_ref.dtype)


def _pool(proj, sp_pad, pw, ps, *, bp, t, bs, ts, d_rnn, d_pool):
    m = proj.shape[0]
    ng = len(POOL_WINDOWS)
    gw = d_pool // ng
    ms = bs * ts
    rbk = min(1024, t, ms)
    while t % rbk or ms % rbk:
        rbk //= 2
    npb = bp * t // rbk
    xcol = 2 * d_rnn // gw
    gcol = (2 * d_rnn + d_pool) // gw
    vmem = 2 * (2 * rbk * gw * 4 + rbk * gw * 2) + rbk * gw * 2 + 2 * bs * POOL_TAIL * gw * 4 + 2 * gw * gw * 4
    return pl.pallas_call(
        functools.partial(_pool_kernel, npb=npb, bpt=t // rbk, ts=ts),
        grid=(ng, m // rbk),
        in_specs=[pl.BlockSpec((rbk, gw), lambda g, r: (r, xcol + g)),
                  pl.BlockSpec((rbk, gw), lambda g, r: (r, gcol + g)),
                  pl.BlockSpec((bs, POOL_TAIL, gw), lambda g, r: (0, 0, g)),
                  pl.BlockSpec((None, gw, gw), lambda g, r: (g, 0, 0)),
                  pl.BlockSpec((1, gw), lambda g, r: (0, g))],
        out_specs=pl.BlockSpec((rbk, gw), lambda g, r: (r, g)),
        out_shape=jax.ShapeDtypeStruct((m, d_pool), BF16),
        scratch_shapes=[pltpu.VMEM((rbk, gw), BF16), pltpu.VMEM((POOL_TAIL, gw), F32)],
        compiler_params=_params(("arbitrary", "arbitrary"), vmem),
        name="pool_mixer",
    )(proj, proj, sp_pad, pw, ps.reshape(1, d_pool))


def _suffix_matrix(n):
    j = lax.broadcasted_iota(jnp.int32, (n, n), 0)
    s = lax.broadcasted_iota(jnp.int32, (n, n), 1)
    u = jnp.where(j > s, -1.0, 0.0).astype(BF16)
    return jnp.concatenate([u, u], axis=0)


def _qk(q, k):
    return lax.dot_general(q, k, (((1,), (1,)), ((), ())), preferred_element_type=F32)


def _strictly_earlier(shape):
    return lax.broadcasted_iota(jnp.int32, shape, 1) < lax.broadcasted_iota(jnp.int32, shape, 0)


def _sb_weights(z, uu, carry=None, mask=None):
    sp = _softplus(z)
    spm = sp if mask is None else jnp.where(mask, sp, 0.0)
    hi = spm.astype(BF16)
    lo = (spm - hi.astype(F32)).astype(BF16)
    e = z - sp + jnp.dot(jnp.concatenate([hi, lo], axis=1), uu, preferred_element_type=F32)
    w = jnp.exp(e if carry is None else e + carry)
    if mask is not None:
        w = jnp.where(mask, w, 0.0)
    return w.astype(BF16), -jnp.sum(spm, axis=1, keepdims=True)


def _sb_tile(q, k, v, uu, carry=None, mask=None):
    w, tot = _sb_weights(_qk(q, k), uu, carry, mask)
    return jnp.dot(w, v, preferred_element_type=F32), tot


PROMPT_TILE = 256
PROMPT_ALWAYS = 2


def _attn_prompt_kernel(q_ref, k_ref, v_ref, gt_ref, uu_ref, o_ref, acc_scr, car_scr, worst_ref, *, tq):
    uu = uu_ref[...]
    mask = _strictly_earlier((tq, tq))
    nq = q_ref.shape[0] // tq

    tiles = [(qi, qi - back) for qi in range(nq) for back in range(min(qi, PROMPT_ALWAYS - 1) + 1)]
    log_beta, spm = [], []
    for qi, kj in tiles:
        zt = _qk(q_ref[pl.ds(qi * tq, tq), :], k_ref[pl.ds(kj * tq, tq), :])
        st = _softplus(zt)
        log_beta.append(zt - st)
        spm.append(jnp.where(mask, st, 0.0) if kj == qi else st)
    stacked = jnp.concatenate(spm, axis=0)
    hi = stacked.astype(BF16)
    lo = (stacked - hi.astype(F32)).astype(BF16)
    suffix = jnp.dot(jnp.concatenate([hi, lo], axis=1), uu, preferred_element_type=F32)
    acc, carry = {}, {}
    for n, (qi, kj) in enumerate(tiles):
        e = log_beta[n] + suffix[n * tq:(n + 1) * tq]
        w = jnp.exp(e if kj == qi else e + carry[qi])
        if kj == qi:
            w = jnp.where(mask, w, 0.0)
        pv = jnp.dot(w.astype(BF16), v_ref[pl.ds(kj * tq, tq), :], preferred_element_type=F32)
        tot = -jnp.sum(spm[n], axis=1, keepdims=True)
        acc[qi] = pv if kj == qi else acc[qi] + pv
        carry[qi] = tot if kj == qi else carry[qi] + tot
    for qi in range(nq):
        rows = pl.ds(qi * tq, tq)
        o_ref[rows, :] = (acc[qi] * _silu(gt_ref[rows, :])).astype(o_ref.dtype)
        if qi >= PROMPT_ALWAYS:
            acc_scr[qi] = acc[qi]
            car_scr[qi] = carry[qi]
            worst_ref[qi] = jnp.max(carry[qi])

    def finish(qi, _):
        @pl.when(worst_ref[qi] >= EXP_UNDERFLOW)
        def _():
            rows = pl.ds(pl.multiple_of(qi * tq, tq), tq)
            q = q_ref[rows, :]

            def more(s):
                return jnp.logical_and(s[0] >= 0, jnp.max(s[2]) >= EXP_UNDERFLOW)

            def older(s):
                j, acc, carry = s
                old = pl.ds(pl.multiple_of(j * tq, tq), tq)
                pv, tot = _sb_tile(q, k_ref[old, :], v_ref[old, :], uu, carry)
                return j - 1, acc + pv, carry + tot

            _, acc, _ = lax.while_loop(more, older, (qi - PROMPT_ALWAYS, acc_scr[qi], car_scr[qi]))
            o_ref[rows, :] = (acc * _silu(gt_ref[rows, :])).astype(o_ref.dtype)

        return 0

    lax.fori_loop(PROMPT_ALWAYS, nq, finish, 0)


def _attn_prompt(qkv, gate, *, bp, t, n_heads):
    tq = _tile(t, PROMPT_TILE)
    d_att = n_heads * HEAD_DIM
    nq = t // tq
    state = pltpu.VMEM((nq, tq, LANES), F32), pltpu.VMEM((nq, tq, 1), F32), pltpu.SMEM((nq,), F32)
    always = PROMPT_ALWAYS * nq
    vmem = (2 * (3 * t * LANES * 2 + t * LANES * 4 + 2 * tq * tq * 2 + t * LANES * 2) + 2 * t * LANES * 4
            + 5 * always * tq * tq * 4)
    col = lambda c: pl.BlockSpec((t, LANES), lambda b, h: (b, c * n_heads + h))
    return pl.pallas_call(
        functools.partial(_attn_prompt_kernel, tq=tq),
        grid=(bp, n_heads),
        in_specs=[col(0), col(1), col(2), col(0), pl.BlockSpec((2 * tq, tq), lambda b, h: (0, 0))],
        out_specs=pl.BlockSpec((t, LANES), lambda b, h: (b, h)),
        out_shape=jax.ShapeDtypeStruct((bp * t, d_att), BF16),
        scratch_shapes=list(state),
        compiler_params=_params(("arbitrary", "arbitrary"), vmem),
        name="attn_prompt",
    )(qkv, qkv, qkv, gate, _suffix_matrix(tq))


def _attn_sample_kernel(q_ref, kn_ref, vn_ref, gt_ref, kc_hbm, vc_hbm, un_ref, uc_ref, o_ref,
                        kfirst, vfirst, kmore, vmore, acc_scr, car_scr, sem, *, layer, pc):
    b = pl.program_id(0)
    n_heads, ts = kmore.shape[0], q_ref.shape[0]
    newest = kc_hbm.shape[2] // pc - 1
    slot = b % 2

    def copies(stream, chunk, kdst, vdst, ksem, vsem):
        pos = pl.ds(chunk * pc, pc)
        out = []
        for h in range(n_heads):
            out.append(pltpu.make_async_copy(kc_hbm.at[layer, stream, pos, h, :], kdst.at[h], ksem))
            out.append(pltpu.make_async_copy(vc_hbm.at[layer, stream, pos, h, :], vdst.at[h], vsem))
        return out

    def first_copies(stream, s):
        return copies(stream, newest, kfirst.at[s], vfirst.at[s], sem.at[s, 0], sem.at[s, 1])

    @pl.when(b == 0)
    def _():
        for c in first_copies(0, 0):
            c.start()

    @pl.when(b + 1 < pl.num_programs(0))
    def _():
        for c in first_copies(b + 1, 1 - slot):
            c.start()

    def head(ref, h):
        return ref[:, h * HEAD_DIM:(h + 1) * HEAD_DIM]

    def scores(keys):
        return jnp.concatenate([_qk(head(q_ref, h), keys(h)) for h in range(n_heads)], axis=0)

    def weighted(w, values):
        return jnp.concatenate([jnp.dot(w[h * ts:(h + 1) * ts], values(h), preferred_element_type=F32)
                                for h in range(n_heads)], axis=0)

    pad = jnp.zeros((LANES - ts, HEAD_DIM), BF16)
    query = lax.broadcasted_iota(jnp.int32, (n_heads, ts, LANES), 1).reshape(n_heads * ts, LANES)
    mask = lax.broadcasted_iota(jnp.int32, (n_heads * ts, LANES), 1) < query
    w, carry = _sb_weights(scores(lambda h: jnp.concatenate([head(kn_ref, h), pad], axis=0)),
                           un_ref[...], None, mask)
    acc = weighted(w, lambda h: jnp.concatenate([head(vn_ref, h), pad], axis=0))

    def chunk(kbuf, vbuf, acc, carry):
        w, tot = _sb_weights(scores(lambda h: kbuf[h].astype(BF16)), uc_ref[...], carry)
        return acc + weighted(w, lambda h: vbuf[h].astype(BF16)), carry + tot

    for c in first_copies(b, slot):
        c.wait()
    acc_scr[...], car_scr[...] = chunk(kfirst.at[slot], vfirst.at[slot], acc, carry)

    def more(s):
        return jnp.logical_and(s[0] >= 0, s[1] >= EXP_UNDERFLOW)

    def older(s):
        cs = copies(b, s[0], kmore, vmore, sem.at[2, 0], sem.at[2, 1])
        for c in cs:
            c.start()
        for c in cs:
            c.wait()
        acc_scr[...], car_scr[...] = chunk(kmore, vmore, acc_scr[...], car_scr[...])
        return s[0] - 1, jnp.max(car_scr[...])

    lax.while_loop(more, older, (newest - 1, jnp.max(car_scr[...])))

    for h in range(n_heads):
        o_ref[:, h * HEAD_DIM:(h + 1) * HEAD_DIM] = (acc_scr[pl.ds(h * ts, ts), :]
                                                     * _silu(head(gt_ref, h))).astype(o_ref.dtype)


def _attn_sample(qkv, gate, cache_k, cache_v, *, layer, mp, bs, ts, n_heads):
    p = cache_k.shape[2]
    d_att = n_heads * HEAD_DIM
    pc = _tile(p, 256)
    r0 = mp // ts
    chunk_bytes = n_heads * pc * HEAD_DIM * 4
    vmem = (2 * (3 * ts * d_att * 2 + ts * d_att * 4 + ts * d_att * 2) + 6 * chunk_bytes
            + 2 * n_heads * ts * LANES * 4 + 2 * 2 * (LANES * LANES + pc * pc) * 2)
    new_spec = lambda c: pl.BlockSpec((ts, d_att), lambda b: (r0 + b, c))
    chunk_buf = lambda n: pltpu.VMEM(n + (n_heads, pc, HEAD_DIM), F32)
    return pl.pallas_call(
        functools.partial(_attn_sample_kernel, layer=layer, pc=pc),
        grid=(bs,),
        in_specs=[new_spec(0), new_spec(1), new_spec(2), new_spec(0),
                  pl.BlockSpec(memory_space=pl.ANY), pl.BlockSpec(memory_space=pl.ANY),
                  pl.BlockSpec((2 * LANES, LANES), lambda b: (0, 0)),
                  pl.BlockSpec((2 * pc, pc), lambda b: (0, 0))],
        out_specs=pl.BlockSpec((ts, d_att), lambda b: (b, 0)),
        out_shape=jax.ShapeDtypeStruct((bs * ts, d_att), BF16),
        scratch_shapes=[chunk_buf((2,)), chunk_buf((2,)), chunk_buf(()), chunk_buf(()),
                        pltpu.VMEM((n_heads * ts, LANES), F32), pltpu.VMEM((n_heads * ts, 1), F32),
                        pltpu.SemaphoreType.DMA((3, 2))],
        compiler_params=_params(("arbitrary",), vmem),
        name="attn_sample",
    )(qkv, qkv, qkv, gate, cache_k, cache_v, _suffix_matrix(LANES), _suffix_matrix(pc))


@jax.jit
def _step(x_prompt, x_sample, cache_k, cache_v, state_h, state_conv, state_pool,
          norm_rec, w_in_rec, conv_w, conv_b, gate_r_w, gate_r_b, gate_i_w, gate_i_b, rg_lambda,
          pool_w, pool_scale, w_out_rec, norm_att, w_in_att, w_out_att, norm_final):
    bp, t, d = x_prompt.shape
    bs, ts, _ = x_sample.shape
    n_rec, n_att = norm_rec.shape[0], norm_att.shape[0]
    d_rnn = state_h.shape[-1]
    d_pool = state_pool.shape[-1]
    n_heads = cache_k.shape[3]
    mp = bp * t
    assert ts >= POOL_BUF and ts % SUBLANES == 0 and t % ts == 0 and cache_k.shape[4] == HEAD_DIM
    assert gate_r_w.shape[2] == LANES and d_pool // len(POOL_WINDOWS) == 2 * LANES

    x = (x_prompt.reshape(mp, d), x_sample.reshape(bs * ts, d))
    sc_pad = jnp.pad(state_conv, ((0, 0), (0, 0), (SUBLANES - (CONV_W - 1), 0), (0, 0)))
    sp_pad = jnp.pad(state_pool, ((0, 0), (0, 0), (POOL_TAIL - POOL_BUF, 0), (0, 0)))
    w_in_rec, w_out_rec, w_in_att, w_out_att = (w.astype(BF16) for w in (w_in_rec, w_out_rec, w_in_att, w_out_att))

    outs = {k: [] for k in ("hp", "cp", "pp", "hs", "cs", "ps")}
    new_kv = None
    for layer in range(n_rec + n_att):
        j = layer // 2
        if layer % 2 == 0:
            proj = (_norm_matmul_rows(*x, norm_rec[j], w_in_rec, j) if isinstance(x, tuple)
                    else _norm_matmul(x, norm_rec[j], w_in_rec, j))
            ya, hp, hs = _rglru(proj, state_h[j], sc_pad[j], conv_w[j], conv_b[j], gate_r_w[j], gate_r_b[j],
                                gate_i_w[j], gate_i_b[j], rg_lambda[j], bp=bp, t=t, bs=bs, ts=ts, d_rnn=d_rnn)
            yb = _pool(proj, sp_pad[j], pool_w[j], pool_scale[j], bp=bp, t=t, bs=bs, ts=ts,
                       d_rnn=d_rnn, d_pool=d_pool)
            x = _matmul_residual([ya, yb], w_out_rec, j, x)
            frames = proj.reshape(-1, ts, proj.shape[1])

            def last_rows(n, c0, c1):
                prompt = lax.slice(frames, (t // ts - 1, ts - n, c0), (mp // ts, ts, c1), (t // ts, 1, 1))
                sample = lax.slice(frames, (mp // ts, ts - n, c0), (frames.shape[0], ts, c1))
                return prompt, sample

            cp, cs = last_rows(CONV_W - 1, 0, d_rnn)
            pp, ps = last_rows(POOL_BUF, 2 * d_rnn, 2 * d_rnn + d_pool)
            for key, val in (("hp", hp), ("hs", hs), ("cp", cp), ("cs", cs), ("pp", pp), ("ps", ps)):
                outs[key].append(val)
        else:
            qkv, gate, new_kv = _norm_matmul_att(x, norm_att[j], w_in_att, new_kv, layer=j, mp=mp, n_heads=n_heads)
            op = _attn_prompt(qkv, gate, bp=bp, t=t, n_heads=n_heads)
            os_ = _attn_sample(qkv, gate, cache_k, cache_v, layer=j, mp=mp, bs=bs, ts=ts, n_heads=n_heads)
            if layer + 1 < n_rec + n_att:
                x = _matmul_residual_rows(op, os_, w_out_att, j, x)
            else:
                y_prompt, y_sample = _matmul_residual_rows_norm(op, os_, w_out_att, j, x, norm_final)

    if (n_rec + n_att) % 2:
        y_prompt, y_sample = _final_norm(x, norm_final, mp)
    st = {k: jnp.stack(v) for k, v in outs.items()}
    kp, vp, ks, vs = new_kv
    prompt_shape = (n_att, bp, t, n_heads, HEAD_DIM)
    sample_shape = (n_att, bs, ts, n_heads, HEAD_DIM)
    return (y_prompt.reshape(bp, t, d), y_sample.reshape(bs, ts, d),
            kp.reshape(prompt_shape), vp.reshape(prompt_shape), st["hp"], st["cp"], st["pp"],
            ks.reshape(sample_shape), vs.reshape(sample_shape), st["hs"], st["cs"], st["ps"])


def kernel(x_prompt, x_sample, cache_k, cache_v, state_h, state_conv, state_pool, norm_rec, w_in_rec, conv_w, conv_b, gate_r_w, gate_r_b, gate_i_w, gate_i_b, rg_lambda, pool_w, pool_scale, w_out_rec, norm_att, w_in_att, w_out_att, norm_final):
    return _step(x_prompt, x_sample, cache_k, cache_v, state_h, state_conv, state_pool, norm_rec, w_in_rec,
                 conv_w, conv_b, gate_r_w, gate_r_b, gate_i_w, gate_i_b, rg_lambda, pool_w, pool_scale,
                 w_out_rec, norm_att, w_in_att, w_out_att, norm_final)
```

```python
import functools

import jax
import jax.numpy as jnp
from jax import lax
from jax.experimental import pallas as pl
from jax.experimental.pallas import tpu as pltpu

F32 = jnp.float32
BF16 = jnp.bfloat16

EPS = 1e-6
RG_C = 8.0
CONV_W = 4
POOL_WINDOWS = (2, 4, 8, 16)
POOL_BUF = max(POOL_WINDOWS) - 1
HEAD_DIM = 128
ATT_SCALE = HEAD_DIM ** -0.5
EXP_UNDERFLOW = -105.0

LANES = 128
SUBLANES = 8
VMEM_LIMIT_CAP = 60000 * 1024
VMEM_SLACK = 8 * 1024 * 1024


def _params(semantics, buffer_bytes):
    limit = min(VMEM_LIMIT_CAP, buffer_bytes + VMEM_SLACK)
    return pltpu.CompilerParams(dimension_semantics=semantics, vmem_limit_bytes=limit)


def _tile(n, pref):
    t = min(n, pref)
    while n % t:
        t //= 2
    return t


def _silu(x):
    half = 0.5 * x
    return half + half * jnp.tanh(half)


LOG2_E = 1.4426950408889634


def _softplus(x):
    return jnp.maximum(x, 0.0) + jnp.log(1.0 + jnp.exp2(jnp.abs(x) * -LOG2_E))


NORM_ROWS = 128


def _normalize_rows(x_ref, g_ref, xn_ref):
    @pl.when(pl.program_id(1) == 0)
    def _():
        g = g_ref[...]

        def body(c, _):
            rows = pl.ds(pl.multiple_of(c * NORM_ROWS, NORM_ROWS), NORM_ROWS)
            x = x_ref[rows, :]
            ms = jnp.mean(x * x, axis=-1, keepdims=True)
            xn_ref[rows, :] = (x * lax.rsqrt(ms + EPS) * g).astype(BF16)
            return 0

        lax.fori_loop(0, x_ref.shape[0] // NORM_ROWS, body, 0)


def _norm_mm_kernel(x_ref, g_ref, w_ref, o_ref, xn_ref):
    _normalize_rows(x_ref, g_ref, xn_ref)
    o_ref[...] = jnp.dot(xn_ref[...], w_ref[...], preferred_element_type=F32)


def _norm_mm_rows_kernel(top_ref, bottom_ref, g_ref, w_ref, o_ref, xn_ref, *, n_top):
    i = pl.program_id(0)
    pl.when(i < n_top)(functools.partial(_normalize_rows, top_ref, g_ref, xn_ref))
    pl.when(i >= n_top)(functools.partial(_normalize_rows, bottom_ref, g_ref, xn_ref))
    o_ref[...] = jnp.dot(xn_ref[...], w_ref[...], preferred_element_type=F32)


def _norm_matmul(x, g, w_all, layer, *, tm_pref=1024, tn_pref=1024):
    m, k = x.shape
    n = w_all.shape[2]
    tm, tn = _tile(m, tm_pref), _tile(n, tn_pref)
    vmem = 2 * tm * k * 4 + tm * k * 2 + 2 * k * tn * 2 + 2 * tm * tn * 4
    return pl.pallas_call(
        _norm_mm_kernel,
        grid=(m // tm, n // tn),
        in_specs=[pl.BlockSpec((tm, k), lambda i, j: (i, 0)),
                  pl.BlockSpec((1, k), lambda i, j: (0, 0)),
                  pl.BlockSpec((None, k, tn), lambda i, j: (layer, 0, j))],
        out_specs=pl.BlockSpec((tm, tn), lambda i, j: (i, j)),
        out_shape=jax.ShapeDtypeStruct((m, n), F32),
        scratch_shapes=[pltpu.VMEM((tm, k), BF16)],
        compiler_params=_params(("arbitrary", "arbitrary"), vmem),
        name="norm_matmul",
    )(x, g.reshape(1, k), w_all)


def _norm_matmul_rows(top, bottom, g, w_all, layer, *, tm_pref=1024, tn_pref=1024):
    k = top.shape[1]
    m, n = top.shape[0] + bottom.shape[0], w_all.shape[2]
    tm = _tile(top.shape[0], tm_pref)
    while bottom.shape[0] % tm:
        tm //= 2
    tn = _tile(n, tn_pref)
    n_top = top.shape[0] // tm
    vmem = 3 * tm * k * 4 + tm * k * 2 + 2 * k * tn * 2 + 2 * tm * tn * 4
    return pl.pallas_call(
        functools.partial(_norm_mm_rows_kernel, n_top=n_top),
        grid=(m // tm, n // tn),
        in_specs=[pl.BlockSpec((tm, k), lambda i, j: (jnp.minimum(i, n_top - 1), 0)),
                  pl.BlockSpec((tm, k), lambda i, j: (jnp.maximum(i - n_top, 0), 0),
                               pipeline_mode=pl.Buffered(1)),
                  pl.BlockSpec((1, k), lambda i, j: (0, 0)),
                  pl.BlockSpec((None, k, tn), lambda i, j: (layer, 0, j))],
        out_specs=pl.BlockSpec((tm, tn), lambda i, j: (i, j)),
        out_shape=jax.ShapeDtypeStruct((m, n), F32),
        scratch_shapes=[pltpu.VMEM((tm, k), BF16)],
        compiler_params=_params(("arbitrary", "arbitrary"), vmem),
        name="norm_matmul_rows",
    )(top, bottom, g.reshape(1, k), w_all)


def _norm_mm_att_kernel(x_ref, g_ref, w_ref, *refs, layer, n_top, per):
    qkv_ref, gt_ref, kp_hbm, vp_hbm, ks_hbm, vs_hbm, xn_ref, kv_scr, sem = refs[-9:]
    layers = (layer,) if len(refs) > 9 else range(kp_hbm.shape[0])
    _normalize_rows(x_ref, g_ref, xn_ref)
    i, j = pl.program_id(0), pl.program_id(1)
    tm, tn = qkv_ref.shape
    hpt = tn // HEAD_DIM
    top = i < n_top

    def project():
        return jnp.dot(xn_ref[...], w_ref[...], preferred_element_type=F32)

    def copies(jj, prompt_rows):
        slot = (jj - per) % 2
        dst = ((kp_hbm, ks_hbm), (vp_hbm, vs_hbm))[jj // per - 1][0 if prompt_rows else 1]
        rows = pl.ds(i * tm if prompt_rows else (i - n_top) * tm, tm)
        return [pltpu.make_async_copy(kv_scr.at[slot, :, pl.ds(h * HEAD_DIM, HEAD_DIM)],
                                      dst.at[l, rows, (jj % per) * hpt + h, :], sem.at[slot])
                for l in layers for h in range(hpt)]

    def for_rows(jj, action):
        @pl.when(top)
        def _():
            for c in copies(jj, True):
                action(c)

        @pl.when(jnp.logical_not(top))
        def _():
            for c in copies(jj, False):
                action(c)

    @pl.when(j < per)
    def _():
        qkv_ref[...] = (project() * ATT_SCALE).astype(BF16)

    for jj in range(per, 3 * per + 2):
        @pl.when(j == jj)
        def _(jj=jj):
            if jj - 2 >= per:
                for_rows(jj - 2, lambda c: c.wait())
            if jj < 3 * per:
                slot = (jj - per) % 2
                kv_scr[slot] = project()
                qkv_ref[...] = kv_scr[slot].astype(BF16)
                for_rows(jj, lambda c: c.start())

    @pl.when(j >= 3 * per)
    def _():
        gt_ref[...] = project()


def _norm_matmul_att(x, g, w_all, new_kv, *, layer, mp, n_heads, tm_pref=1024):
    m, k = x.shape
    n_layers = w_all.shape[0]
    d_att = n_heads * HEAD_DIM
    tn = d_att // 2
    per = d_att // tn
    assert per >= 2 and w_all.shape[2] == 4 * d_att
    tm = _tile(mp, tm_pref)
    while (m - mp) % tm:
        tm //= 2
    any_spec = pl.BlockSpec(memory_space=pl.ANY)
    carried = () if new_kv is None else tuple(new_kv)
    kv_shapes = [jax.ShapeDtypeStruct((n_layers, rows, n_heads, HEAD_DIM), F32) for rows in (mp, mp, m - mp, m - mp)]
    vmem = 2 * tm * k * 4 + tm * k * 2 + 2 * k * tn * 2 + 2 * tm * tn * (2 + 4) + 3 * tm * tn * 4
    outs = pl.pallas_call(
        functools.partial(_norm_mm_att_kernel, layer=layer, n_top=mp // tm, per=per),
        grid=(m // tm, 4 * per),
        in_specs=[pl.BlockSpec((tm, k), lambda i, j: (i, 0)),
                  pl.BlockSpec((1, k), lambda i, j: (0, 0)),
                  pl.BlockSpec((None, k, tn), lambda i, j: (layer, 0, j))] + [any_spec] * len(carried),
        out_specs=[pl.BlockSpec((tm, tn), lambda i, j: (i, jnp.minimum(j, 3 * per - 1))),
                   pl.BlockSpec((tm, tn), lambda i, j: (i, jnp.maximum(j - 3 * per, 0))),
                   any_spec, any_spec, any_spec, any_spec],
        out_shape=[jax.ShapeDtypeStruct((m, 3 * d_att), BF16), jax.ShapeDtypeStruct((m, d_att), F32)] + kv_shapes,
        input_output_aliases={3 + n: 2 + n for n in range(len(carried))},
        scratch_shapes=[pltpu.VMEM((tm, k), BF16), pltpu.VMEM((2, tm, tn), F32),
                        pltpu.SemaphoreType.DMA((2,))],
        compiler_params=_params(("arbitrary", "arbitrary"), vmem),
        name="norm_matmul_att",
    )(x, g.reshape(1, k), w_all, *carried)
    return outs[0], outs[1], tuple(outs[2:])


def _mm_res_kernel(*refs, n_pairs, n_top):
    lhs, ws = refs[:n_pairs], refs[n_pairs:2 * n_pairs]
    res_refs, out_ref = refs[2 * n_pairs:-1], refs[-1]

    def emit(res_ref):
        acc = res_ref[...]
        for l, w in zip(lhs, ws):
            acc = acc + jnp.dot(l[...], w[...], preferred_element_type=F32)
        out_ref[...] = acc

    if len(res_refs) == 1:
        emit(res_refs[0])
    else:
        i = pl.program_id(0)
        pl.when(i < n_top)(functools.partial(emit, res_refs[0]))
        pl.when(i >= n_top)(functools.partial(emit, res_refs[1]))


def _matmul_residual(lhs_list, w_all, layer, res, *, tm_pref=1024, tn_pref=1024):
    parts = res if isinstance(res, tuple) else (res,)
    m, n = sum(p.shape[0] for p in parts), parts[0].shape[1]
    tm = _tile(parts[0].shape[0], tm_pref)
    while any(p.shape[0] % tm for p in parts):
        tm //= 2
    tn = _tile(n, tn_pref)
    n_top = parts[0].shape[0] // tm
    ks = [l.shape[1] for l in lhs_list]
    starts = [sum(ks[:p]) for p in range(len(ks))]
    assert all(r % k == 0 for r, k in zip(starts, ks)) and sum(ks) == w_all.shape[1]
    vmem = sum(2 * tm * k * 2 + 2 * k * tn * 2 for k in ks) + (2 + 2 * len(parts)) * tm * tn * 4
    res_specs = ([pl.BlockSpec((tm, tn), lambda i, j: (i, j))] if len(parts) == 1 else
                 [pl.BlockSpec((tm, tn), lambda i, j: (jnp.minimum(i, n_top - 1), j)),
                  pl.BlockSpec((tm, tn), lambda i, j: (jnp.maximum(i - n_top, 0), j))])
    in_specs = ([pl.BlockSpec((tm, k), lambda i, j: (i, 0)) for k in ks]
                + [pl.BlockSpec((None, k, tn), lambda i, j, blk=r // k: (layer, blk, j)) for r, k in zip(starts, ks)]
                + res_specs)
    return pl.pallas_call(
        functools.partial(_mm_res_kernel, n_pairs=len(ks), n_top=n_top),
        grid=(m // tm, n // tn),
        in_specs=in_specs,
        out_specs=pl.BlockSpec((tm, tn), lambda i, j: (i, j)),
        out_shape=jax.ShapeDtypeStruct((m, n), F32),
        compiler_params=_params(("arbitrary", "arbitrary"), vmem),
        name="matmul_residual",
    )(*lhs_list, *([w_all] * len(ks)), *parts)


def _mm_res_rows_kernel(top_ref, bottom_ref, w_ref, res_ref, out_ref, *, n_top):
    i = pl.program_id(0)

    def emit(lhs_ref):
        out_ref[...] = res_ref[...] + jnp.dot(lhs_ref[...], w_ref[...], preferred_element_type=F32)

    pl.when(i < n_top)(functools.partial(emit, top_ref))
    pl.when(i >= n_top)(functools.partial(emit, bottom_ref))


def _mm_res_rows_norm_kernel(top_ref, bottom_ref, w_ref, res_ref, g_ref, top_out, bottom_out, *, n_top):
    i = pl.program_id(0)

    def emit(lhs_ref, out_ref):
        y = res_ref[...] + jnp.dot(lhs_ref[...], w_ref[...], preferred_element_type=F32)
        ms = jnp.mean(y * y, axis=-1, keepdims=True)
        out_ref[...] = y * lax.rsqrt(ms + EPS) * g_ref[...]

    pl.when(i < n_top)(functools.partial(emit, top_ref, top_out))
    pl.when(i >= n_top)(functools.partial(emit, bottom_ref, bottom_out))


def _matmul_residual_rows_norm(top, bottom, w_all, layer, res, g, *, tm_pref=512):
    m, n = res.shape
    k = w_all.shape[1]
    tm = _tile(top.shape[0], tm_pref)
    while bottom.shape[0] % tm:
        tm //= 2
    n_top = top.shape[0] // tm
    vmem = 4 * tm * k * 2 + 2 * k * n * 2 + 6 * tm * n * 4
    return pl.pallas_call(
        functools.partial(_mm_res_rows_norm_kernel, n_top=n_top),
        grid=(m // tm,),
        in_specs=[pl.BlockSpec((tm, k), lambda i: (jnp.minimum(i, n_top - 1), 0)),
                  pl.BlockSpec((tm, k), lambda i: (jnp.maximum(i - n_top, 0), 0)),
                  pl.BlockSpec((None, k, n), lambda i: (layer, 0, 0)),
                  pl.BlockSpec((tm, n), lambda i: (i, 0)),
                  pl.BlockSpec((1, n), lambda i: (0, 0))],
        out_specs=[pl.BlockSpec((tm, n), lambda i: (jnp.minimum(i, n_top - 1), 0)),
                   pl.BlockSpec((tm, n), lambda i: (jnp.maximum(i - n_top, 0), 0))],
        out_shape=[jax.ShapeDtypeStruct((top.shape[0], n), F32), jax.ShapeDtypeStruct((bottom.shape[0], n), F32)],
        compiler_params=_params(("arbitrary",), vmem),
        name="matmul_residual_rows_norm",
    )(top, bottom, w_all, res, g.reshape(1, n))


def _matmul_residual_rows(top, bottom, w_all, layer, res, *, tm_pref=1024, tn_pref=1024):
    m, n = res.shape
    k = w_all.shape[1]
    tm = _tile(top.shape[0], tm_pref)
    while bottom.shape[0] % tm:
        tm //= 2
    tn = _tile(n, tn_pref)
    n_top = top.shape[0] // tm
    vmem = 4 * tm * k * 2 + 2 * k * tn * 2 + 4 * tm * tn * 4
    return pl.pallas_call(
        functools.partial(_mm_res_rows_kernel, n_top=n_top),
        grid=(m // tm, n // tn),
        in_specs=[pl.BlockSpec((tm, k), lambda i, j: (jnp.minimum(i, n_top - 1), 0)),
                  pl.BlockSpec((tm, k), lambda i, j: (jnp.maximum(i - n_top, 0), 0)),
                  pl.BlockSpec((None, k, tn), lambda i, j: (layer, 0, j)),
                  pl.BlockSpec((tm, tn), lambda i, j: (i, j))],
        out_specs=pl.BlockSpec((tm, tn), lambda i, j: (i, j)),
        out_shape=jax.ShapeDtypeStruct((m, n), F32),
        compiler_params=_params(("arbitrary", "arbitrary"), vmem),
        name="matmul_residual_rows",
    )(top, bottom, w_all, res)


def _final_norm_kernel(x_ref, g_ref, top_ref, bottom_ref, *, n_top):
    x = x_ref[...]
    ms = jnp.mean(x * x, axis=-1, keepdims=True)
    y = x * lax.rsqrt(ms + EPS) * g_ref[...]
    i = pl.program_id(0)

    @pl.when(i < n_top)
    def _():
        top_ref[...] = y

    @pl.when(i >= n_top)
    def _():
        bottom_ref[...] = y


def _final_norm(x, g, mp):
    m, d = x.shape
    tm = _tile(mp, 256)
    while (m - mp) % tm:
        tm //= 2
    n_top = mp // tm
    return pl.pallas_call(
        functools.partial(_final_norm_kernel, n_top=n_top),
        grid=(m // tm,),
        in_specs=[pl.BlockSpec((tm, d), lambda i: (i, 0)), pl.BlockSpec((1, d), lambda i: (0, 0))],
        out_specs=[pl.BlockSpec((tm, d), lambda i: (jnp.minimum(i, n_top - 1), 0)),
                   pl.BlockSpec((tm, d), lambda i: (jnp.maximum(i - n_top, 0), 0))],
        out_shape=[jax.ShapeDtypeStruct((mp, d), F32), jax.ShapeDtypeStruct((m - mp, d), F32)],
        compiler_params=_params(("arbitrary",), 6 * tm * d * 4),
        name="final_norm",
    )(x, g.reshape(1, d))


SCAN_ROWS = 256
SCAN_STREAMS = 8


def _scan_tiles(a, u, carry):
    r = a.shape[0]
    nt = r // SUBLANES
    a3 = a.reshape(nt, SUBLANES, LANES)
    u3 = u.reshape(nt, SUBLANES, LANES)
    sub = lax.broadcasted_iota(jnp.int32, a3.shape, 1)
    for d in (1, 2, 4):
        keep = sub >= d
        a_prev = pltpu.roll(a3, d, 1)
        u_prev = pltpu.roll(u3, d, 1)
        u3 = jnp.where(keep, a3 * u_prev + u3, u3)
        a3 = jnp.where(keep, a3 * a_prev, a3)
    hs = []
    for t in range(nt):
        h_t = u3[t] + a3[t] * carry
        carry = h_t[SUBLANES - 1:SUBLANES, :]
        hs.append(h_t)
    return jnp.concatenate(hs, axis=0), carry


def _rglru_kernel(xa_ref, ga_ref, h0_ref, sc_ref, cw_ref, cb_ref, wr_ref, br_ref, wi_ref, bi_ref, lam_ref,
                  out_ref, hp_ref, hs_ref, *, bp, t, bs, ts):
    mp = bp * t
    cw = cw_ref[...]
    cb = cb_ref[...]
    half_wr = (0.5 * wr_ref[...]).astype(BF16)
    half_wi = (0.5 * wi_ref[...]).astype(BF16)
    half_br = 0.5 * br_ref[...]
    half_bi = 0.5 * bi_ref[...]
    half_c = (-0.5 * RG_C) * _softplus(-lam_ref[...])

    def conv_taps(tap):
        y = cb + tap(0) * cw[0:1]
        for k in range(1, CONV_W):
            y = y + tap(k) * cw[k:k + 1]
        return y

    def conv(ext, n):
        def tap(k):
            back = CONV_W - 1 - k
            return (ext if back == 0 else pltpu.roll(ext, back, 0))[SUBLANES:]

        return conv_taps(tap)

    def decay_and_input(xc):
        xb = xc.astype(BF16)
        tr = jnp.tanh(jnp.dot(xb, half_wr, preferred_element_type=F32) + half_br)
        ti = jnp.tanh(jnp.dot(xb, half_wi, preferred_element_type=F32) + half_bi)
        log_a = half_c + half_c * tr
        half_x = 0.5 * xc
        gated_x = half_x + half_x * ti
        a = jnp.exp(log_a)
        v = -jnp.tanh(log_a) * (1.0 + a * a)
        root = jnp.where(v > 0.0, v * lax.rsqrt(v), 0.0)
        return a, root * gated_x

    def emit(rows, a, u, carry):
        h, carry = _scan_tiles(a, u, carry)
        out_ref[rows, :] = (h * _silu(ga_ref[rows, :])).astype(out_ref.dtype)
        return carry

    rp = _tile(t, SCAN_ROWS)

    def first_chunk(b):
        rows = pl.ds(b * t, rp)
        ext = jnp.concatenate([jnp.zeros((SUBLANES, LANES), F32), xa_ref[rows, :]], axis=0)
        a, u = decay_and_input(conv(ext, rp))
        return emit(rows, a, u, jnp.zeros((1, LANES), F32))

    def later_chunk(c, carries):
        new = []
        for b in range(bp):
            r0 = b * t + c * rp
            rows = pl.ds(pl.multiple_of(r0, SUBLANES), rp)
            xc = conv_taps(lambda k: xa_ref[pl.ds(r0 - (CONV_W - 1 - k), rp), :])
            a, u = decay_and_input(xc)
            new.append(emit(rows, a, u, carries[b]))
        return tuple(new)

    carries = lax.fori_loop(1, t // rp, later_chunk, tuple(first_chunk(b) for b in range(bp)))
    for b in range(bp):
        hp_ref[b:b + 1, :] = carries[b]

    ns = _tile(bs, SCAN_STREAMS)

    def sample_group(c, _):
        rows = [pl.ds(pl.multiple_of(mp + (c * ns + k) * ts, SUBLANES), ts) for k in range(ns)]
        xc = [conv(jnp.concatenate([sc_ref[c * ns + k], xa_ref[rows[k], :]], axis=0), ts) for k in range(ns)]
        a, u = decay_and_input(jnp.concatenate(xc, axis=0))
        for k in range(ns):
            seg = slice(k * ts, (k + 1) * ts)
            s = c * ns + k
            hs_ref[pl.ds(s, 1), :] = emit(rows[k], a[seg], u[seg], h0_ref[pl.ds(s, 1), :])
        return 0

    lax.fori_loop(0, bs // ns, sample_group, 0)


def _rglru(proj, h0, sc_pad, cw, cb, wr, br, wi, bi, lam, *, bp, t, bs, ts, d_rnn):
    m = proj.shape[0]
    nb = d_rnn // LANES
    row = lambda v: v.reshape(1, d_rnn)
    vec_spec = pl.BlockSpec((1, LANES), lambda n: (0, n))
    w_spec = pl.BlockSpec((None, LANES, LANES), lambda n: (n, 0, 0))
    vmem = 2 * (2 * m * LANES * 4 + m * LANES * 2)
    return pl.pallas_call(
        functools.partial(_rglru_kernel, bp=bp, t=t, bs=bs, ts=ts),
        grid=(nb,),
        in_specs=[pl.BlockSpec((m, LANES), lambda n: (0, n)),
                  pl.BlockSpec((m, LANES), lambda n: (0, nb + n)),
                  pl.BlockSpec((bs, LANES), lambda n: (0, n)),
                  pl.BlockSpec((bs, SUBLANES, LANES), lambda n: (0, 0, n)),
                  pl.BlockSpec((CONV_W, LANES), lambda n: (0, n)),
                  vec_spec, w_spec, vec_spec, w_spec, vec_spec, vec_spec],
        out_specs=[pl.BlockSpec((m, LANES), lambda n: (0, n)),
                   pl.BlockSpec((bp, LANES), lambda n: (0, n)),
                   pl.BlockSpec((bs, LANES), lambda n: (0, n))],
        out_shape=[jax.ShapeDtypeStruct((m, d_rnn), BF16),
                   jax.ShapeDtypeStruct((bp, d_rnn), F32),
                   jax.ShapeDtypeStruct((bs, d_rnn), F32)],
        compiler_params=_params(("arbitrary",), vmem),
        name="rglru_mixer",
    )(proj, proj, h0, sc_pad, cw, row(cb), wr, row(br), wi, row(bi), row(lam))


POOL_TAIL = 16
POOL_ROWS = 128


def _pool_kernel(xb_ref, gb_ref, sp_ref, pw_ref, ps_ref, out_ref, m_scr, tail_scr, *, npb, bpt, ts):
    g = pl.program_id(0)
    rb = pl.program_id(1)
    rbk, gw = xb_ref.shape
    wf = lax.shift_left(jnp.int32(2), g).astype(F32)

    def window_means(ext, n, pos0):
        s2 = ext[1:] + ext[:-1]
        s4 = s2[2:] + s2[:-2]
        s8 = s4[4:] + s4[:-4]
        s16 = s8[8:] + s8[:-8]
        x = ext[POOL_TAIL:]
        win = jnp.where(g == 0, s2[POOL_TAIL - 1:],
                        jnp.where(g == 1, s4[POOL_TAIL - 3:],
                                  jnp.where(g == 2, s8[POOL_TAIL - 7:], s16[POOL_TAIL - 15:])))
        if pos0 is None:
            cnt = wf
        else:
            pos = pos0 + lax.broadcasted_iota(jnp.int32, (n, gw), 0)
            cnt = jnp.minimum(wf, (pos + 1).astype(F32))
        return win / cnt - x

    @pl.when(rb < npb)
    def _():
        blk = rb % bpt

        @pl.when(blk == 0)
        def _():
            tail_scr[...] = jnp.zeros_like(tail_scr)

        rc = _tile(rbk, POOL_ROWS)

        def body(c, tail):
            rows = pl.ds(pl.multiple_of(c * rc, SUBLANES), rc)
            x = xb_ref[rows, :]
            ext = jnp.concatenate([tail, x], axis=0)
            m_scr[rows, :] = window_means(ext, rc, blk * rbk + c * rc).astype(BF16)
            return x[rc - POOL_TAIL:, :]

        tail_scr[...] = lax.fori_loop(0, rbk // rc, body, tail_scr[...])

    @pl.when(rb >= npb)
    def _():
        s0 = (rb - npb) * (rbk // ts)

        def body(s, _):
            rows = pl.ds(pl.multiple_of(s * ts, SUBLANES), ts)
            ext = jnp.concatenate([sp_ref[s0 + s], xb_ref[rows, :]], axis=0)
            m_scr[rows, :] = window_means(ext, ts, None).astype(BF16)
            return 0

        lax.fori_loop(0, rbk // ts, body, 0)

    y = jnp.dot(m_scr[...], pw_ref[...].astype(BF16), preferred_element_type=F32) * ps_ref[...]
    out_ref[...] = (y * _silu(gb_ref[...])).astype(out_ref.dtype)


def _pool(proj, sp_pad, pw, ps, *, bp, t, bs, ts, d_rnn, d_pool):
    m = proj.shape[0]
    ng = len(POOL_WINDOWS)
    gw = d_pool // ng
    ms = bs * ts
    rbk = min(1024, t, ms)
    while t % rbk or ms % rbk:
        rbk //= 2
    npb = bp * t // rbk
    xcol = 2 * d_rnn // gw
    gcol = (2 * d_rnn + d_pool) // gw
    vmem = 2 * (2 * rbk * gw * 4 + rbk * gw * 2) + rbk * gw * 2 + 2 * bs * POOL_TAIL * gw * 4 + 2 * gw * gw * 4
    return pl.pallas_call(
        functools.partial(_pool_kernel, npb=npb, bpt=t // rbk, ts=ts),
        grid=(ng, m // rbk),
        in_specs=[pl.BlockSpec((rbk, gw), lambda g, r: (r, xcol + g)),
                  pl.BlockSpec((rbk, gw), lambda g, r: (r, gcol + g)),
                  pl.BlockSpec((bs, POOL_TAIL, gw), lambda g, r: (0, 0, g)),
                  pl.BlockSpec((None, gw, gw), lambda g, r: (g, 0, 0)),
                  pl.BlockSpec((1, gw), lambda g, r: (0, g))],
        out_specs=pl.BlockSpec((rbk, gw), lambda g, r: (r, g)),
        out_shape=jax.ShapeDtypeStruct((m, d_pool), BF16),
        scratch_shapes=[pltpu.VMEM((rbk, gw), BF16), pltpu.VMEM((POOL_TAIL, gw), F32)],
        compiler_params=_params(("arbitrary", "arbitrary"), vmem),
        name="pool_mixer",
    )(proj, proj, sp_pad, pw, ps.reshape(1, d_pool))


def _suffix_matrix(n):
    j = lax.broadcasted_iota(jnp.int32, (n, n), 0)
    s = lax.broadcasted_iota(jnp.int32, (n, n), 1)
    u = jnp.where(j > s, -1.0, 0.0).astype(BF16)
    return jnp.concatenate([u, u], axis=0)


def _qk(q, k):
    return lax.dot_general(q, k, (((1,), (1,)), ((), ())), preferred_element_type=F32)


def _strictly_earlier(shape):
    return lax.broadcasted_iota(jnp.int32, shape, 1) < lax.broadcasted_iota(jnp.int32, shape, 0)


def _sb_weights(z, uu, carry=None, mask=None):
    sp = _softplus(z)
    spm = sp if mask is None else jnp.where(mask, sp, 0.0)
    hi = spm.astype(BF16)
    lo = (spm - hi.astype(F32)).astype(BF16)
    e = z - sp + jnp.dot(jnp.concatenate([hi, lo], axis=1), uu, preferred_element_type=F32)
    w = jnp.exp(e if carry is None else e + carry)
    if mask is not None:
        w = jnp.where(mask, w, 0.0)
    return w.astype(BF16), -jnp.sum(spm, axis=1, keepdims=True)


def _sb_tile(q, k, v, uu, carry=None, mask=None):
    w, tot = _sb_weights(_qk(q, k), uu, carry, mask)
    return jnp.dot(w, v, preferred_element_type=F32), tot


PROMPT_TILE = 256
PROMPT_ALWAYS = 2


def _attn_prompt_kernel(q_ref, k_ref, v_ref, gt_ref, uu_ref, o_ref, acc_scr, car_scr, worst_ref, *, tq):
    uu = uu_ref[...]
    mask = _strictly_earlier((tq, tq))
    nq = q_ref.shape[0] // tq

    groups = [range(kj, min(kj + PROMPT_ALWAYS, nq)) for kj in range(nq)]
    log_beta, spm = {}, {}
    for kj, qs in enumerate(groups):
        zb = _qk(q_ref[pl.ds(kj * tq, len(qs) * tq), :], k_ref[pl.ds(kj * tq, tq), :])
        for n, qi in enumerate(qs):
            zt = zb[n * tq:(n + 1) * tq]
            st = _softplus(zt)
            log_beta[qi, kj] = zt - st
            spm[qi, kj] = jnp.where(mask, st, 0.0) if qi == kj else st
    order = list(spm)
    stacked = jnp.concatenate([spm[tile] for tile in order], axis=0)
    hi = stacked.astype(BF16)
    lo = (stacked - hi.astype(F32)).astype(BF16)
    suffix = jnp.dot(jnp.concatenate([hi, lo], axis=1), uu, preferred_element_type=F32)
    suffix = {tile: suffix[n * tq:(n + 1) * tq] for n, tile in enumerate(order)}
    total = {tile: -jnp.sum(spm[tile], axis=1, keepdims=True) for tile in order}
    acc = {}
    for kj, qs in enumerate(groups):
        ws = []
        for qi in qs:
            e = log_beta[qi, kj] + suffix[qi, kj]
            for newer in range(qi, kj, -1):
                e = e + total[qi, newer]
            w = jnp.exp(e)
            ws.append((jnp.where(mask, w, 0.0) if qi == kj else w).astype(BF16))
        pv = jnp.dot(jnp.concatenate(ws, axis=0), v_ref[pl.ds(kj * tq, tq), :], preferred_element_type=F32)
        for n, qi in enumerate(qs):
            part = pv[n * tq:(n + 1) * tq]
            acc[qi] = acc[qi] + part if qi in acc else part
    for qi in range(nq):
        rows = pl.ds(qi * tq, tq)
        o_ref[rows, :] = (acc[qi] * _silu(gt_ref[rows, :])).astype(o_ref.dtype)
        if qi >= PROMPT_ALWAYS:
            carry = total[qi, qi]
            for kj in range(qi - 1, qi - PROMPT_ALWAYS, -1):
                carry = carry + total[qi, kj]
            acc_scr[qi] = acc[qi]
            car_scr[qi] = carry
            worst_ref[qi] = jnp.max(carry)

    def finish(qi, _):
        @pl.when(worst_ref[qi] >= EXP_UNDERFLOW)
        def _():
            rows = pl.ds(pl.multiple_of(qi * tq, tq), tq)
            q = q_ref[rows, :]

            def more(s):
                return jnp.logical_and(s[0] >= 0, jnp.max(s[2]) >= EXP_UNDERFLOW)

            def older(s):
                j, acc, carry = s
                old = pl.ds(pl.multiple_of(j * tq, tq), tq)
                pv, tot = _sb_tile(q, k_ref[old, :], v_ref[old, :], uu, carry)
                return j - 1, acc + pv, carry + tot

            _, acc, _ = lax.while_loop(more, older, (qi - PROMPT_ALWAYS, acc_scr[qi], car_scr[qi]))
            o_ref[rows, :] = (acc * _silu(gt_ref[rows, :])).astype(o_ref.dtype)

        return 0

    lax.fori_loop(PROMPT_ALWAYS, nq, finish, 0)


def _attn_prompt(qkv, gate, *, bp, t, n_heads):
    tq = _tile(t, PROMPT_TILE)
    d_att = n_heads * HEAD_DIM
    nq = t // tq
    state = pltpu.VMEM((nq, tq, LANES), F32), pltpu.VMEM((nq, tq, 1), F32), pltpu.SMEM((nq,), F32)
    always = PROMPT_ALWAYS * nq
    vmem = (2 * (3 * t * LANES * 2 + t * LANES * 4 + 2 * tq * tq * 2 + t * LANES * 2) + 2 * t * LANES * 4
            + 5 * always * tq * tq * 4)
    col = lambda c: pl.BlockSpec((t, LANES), lambda b, h: (b, c * n_heads + h))
    return pl.pallas_call(
        functools.partial(_attn_prompt_kernel, tq=tq),
        grid=(bp, n_heads),
        in_specs=[col(0), col(1), col(2), col(0), pl.BlockSpec((2 * tq, tq), lambda b, h: (0, 0))],
        out_specs=pl.BlockSpec((t, LANES), lambda b, h: (b, h)),
        out_shape=jax.ShapeDtypeStruct((bp * t, d_att), BF16),
        scratch_shapes=list(state),
        compiler_params=_params(("arbitrary", "arbitrary"), vmem),
        name="attn_prompt",
    )(qkv, qkv, qkv, gate, _suffix_matrix(tq))


def _attn_sample_kernel(q_ref, kn_ref, vn_ref, gt_ref, kc_hbm, vc_hbm, un_ref, uc_ref, o_ref,
                        kfirst, vfirst, kmore, vmore, acc_scr, car_scr, sem, *, layer, pc):
    b = pl.program_id(0)
    n_heads, ts = kmore.shape[0], q_ref.shape[0]
    newest = kc_hbm.shape[2] // pc - 1
    slot = b % 2

    def copies(stream, chunk, kdst, vdst, ksem, vsem):
        pos = pl.ds(chunk * pc, pc)
        out = []
        for h in range(n_heads):
            out.append(pltpu.make_async_copy(kc_hbm.at[layer, stream, pos, h, :], kdst.at[h], ksem))
            out.append(pltpu.make_async_copy(vc_hbm.at[layer, stream, pos, h, :], vdst.at[h], vsem))
        return out

    def first_copies(stream, s):
        return copies(stream, newest, kfirst.at[s], vfirst.at[s], sem.at[s, 0], sem.at[s, 1])

    @pl.when(b == 0)
    def _():
        for c in first_copies(0, 0):
            c.start()

    @pl.when(b + 1 < pl.num_programs(0))
    def _():
        for c in first_copies(b + 1, 1 - slot):
            c.start()

    def head(ref, h):
        return ref[:, h * HEAD_DIM:(h + 1) * HEAD_DIM]

    def scores(keys):
        return jnp.concatenate([_qk(head(q_ref, h), keys(h)) for h in range(n_heads)], axis=0)

    def weighted(w, values):
        return jnp.concatenate([jnp.dot(w[h * ts:(h + 1) * ts], values(h), preferred_element_type=F32)
                                for h in range(n_heads)], axis=0)

    pad = jnp.zeros((LANES - ts, HEAD_DIM), BF16)
    query = lax.broadcasted_iota(jnp.int32, (n_heads, ts, LANES), 1).reshape(n_heads * ts, LANES)
    mask = lax.broadcasted_iota(jnp.int32, (n_heads * ts, LANES), 1) < query
    w, carry = _sb_weights(scores(lambda h: jnp.concatenate([head(kn_ref, h), pad], axis=0)),
                           un_ref[...], None, mask)
    acc = weighted(w, lambda h: jnp.concatenate([head(vn_ref, h), pad], axis=0))

    def chunk(kbuf, vbuf, acc, carry):
        w, tot = _sb_weights(scores(lambda h: kbuf[h].astype(BF16)), uc_ref[...], carry)
        return acc + weighted(w, lambda h: vbuf[h].astype(BF16)), carry + tot

    for c in first_copies(b, slot):
        c.wait()
    acc_scr[...], car_scr[...] = chunk(kfirst.at[slot], vfirst.at[slot], acc, carry)

    def more(s):
        return jnp.logical_and(s[0] >= 0, s[1] >= EXP_UNDERFLOW)

    def older(s):
        cs = copies(b, s[0], kmore, vmore, sem.at[2, 0], sem.at[2, 1])
        for c in cs:
            c.start()
        for c in cs:
            c.wait()
        acc_scr[...], car_scr[...] = chunk(kmore, vmore, acc_scr[...], car_scr[...])
        return s[0] - 1, jnp.max(car_scr[...])

    lax.while_loop(more, older, (newest - 1, jnp.max(car_scr[...])))

    for h in range(n_heads):
        o_ref[:, h * HEAD_DIM:(h + 1) * HEAD_DIM] = (acc_scr[pl.ds(h * ts, ts), :]
                                                     * _silu(head(gt_ref, h))).astype(o_ref.dtype)


def _attn_sample(qkv, gate, cache_k, cache_v, *, layer, mp, bs, ts, n_heads):
    p = cache_k.shape[2]
    d_att = n_heads * HEAD_DIM
    pc = _tile(p, 256)
    r0 = mp // ts
    chunk_bytes = n_heads * pc * HEAD_DIM * 4
    vmem = (2 * (3 * ts * d_att * 2 + ts * d_att * 4 + ts * d_att * 2) + 6 * chunk_bytes
            + 2 * n_heads * ts * LANES * 4 + 2 * 2 * (LANES * LANES + pc * pc) * 2)
    new_spec = lambda c: pl.BlockSpec((ts, d_att), lambda b: (r0 + b, c))
    chunk_buf = lambda n: pltpu.VMEM(n + (n_heads, pc, HEAD_DIM), F32)
    return pl.pallas_call(
        functools.partial(_attn_sample_kernel, layer=layer, pc=pc),
        grid=(bs,),
        in_specs=[new_spec(0), new_spec(1), new_spec(2), new_spec(0),
                  pl.BlockSpec(memory_space=pl.ANY), pl.BlockSpec(memory_space=pl.ANY),
                  pl.BlockSpec((2 * LANES, LANES), lambda b: (0, 0)),
                  pl.BlockSpec((2 * pc, pc), lambda b: (0, 0))],
        out_specs=pl.BlockSpec((ts, d_att), lambda b: (b, 0)),
        out_shape=jax.ShapeDtypeStruct((bs * ts, d_att), BF16),
        scratch_shapes=[chunk_buf((2,)), chunk_buf((2,)), chunk_buf(()), chunk_buf(()),
                        pltpu.VMEM((n_heads * ts, LANES), F32), pltpu.VMEM((n_heads * ts, 1), F32),
                        pltpu.SemaphoreType.DMA((3, 2))],
        compiler_params=_params(("arbitrary",), vmem),
        name="attn_sample",
    )(qkv, qkv, qkv, gate, cache_k, cache_v, _suffix_matrix(LANES), _suffix_matrix(pc))


@jax.jit
def _step(x_prompt, x_sample, cache_k, cache_v, state_h, state_conv, state_pool,
          norm_rec, w_in_rec, conv_w, conv_b, gate_r_w, gate_r_b, gate_i_w, gate_i_b, rg_lambda,
          pool_w, pool_scale, w_out_rec, norm_att, w_in_att, w_out_att, norm_final):
    bp, t, d = x_prompt.shape
    bs, ts, _ = x_sample.shape
    n_rec, n_att = norm_rec.shape[0], norm_att.shape[0]
    d_rnn = state_h.shape[-1]
    d_pool = state_pool.shape[-1]
    n_heads = cache_k.shape[3]
    mp = bp * t
    assert ts >= POOL_BUF and ts % SUBLANES == 0 and t % ts == 0 and cache_k.shape[4] == HEAD_DIM
    assert gate_r_w.shape[2] == LANES and d_pool // len(POOL_WINDOWS) == 2 * LANES

    x = (x_prompt.reshape(mp, d), x_sample.reshape(bs * ts, d))
    sc_pad = jnp.pad(state_conv, ((0, 0), (0, 0), (SUBLANES - (CONV_W - 1), 0), (0, 0)))
    sp_pad = jnp.pad(state_pool, ((0, 0), (0, 0), (POOL_TAIL - POOL_BUF, 0), (0, 0)))
    w_in_rec, w_out_rec, w_in_att, w_out_att = (w.astype(BF16) for w in (w_in_rec, w_out_rec, w_in_att, w_out_att))

    outs = {k: [] for k in ("hp", "cp", "pp", "hs", "cs", "ps")}
    new_kv = None
    for layer in range(n_rec + n_att):
        j = layer // 2
        if layer % 2 == 0:
            proj = (_norm_matmul_rows(*x, norm_rec[j], w_in_rec, j) if isinstance(x, tuple)
                    else _norm_matmul(x, norm_rec[j], w_in_rec, j))
            ya, hp, hs = _rglru(proj, state_h[j], sc_pad[j], conv_w[j], conv_b[j], gate_r_w[j], gate_r_b[j],
                                gate_i_w[j], gate_i_b[j], rg_lambda[j], bp=bp, t=t, bs=bs, ts=ts, d_rnn=d_rnn)
            yb = _pool(proj, sp_pad[j], pool_w[j], pool_scale[j], bp=bp, t=t, bs=bs, ts=ts,
                       d_rnn=d_rnn, d_pool=d_pool)
            x = _matmul_residual([ya, yb], w_out_rec, j, x)
            frames = proj.reshape(-1, ts, proj.shape[1])

            def last_rows(n, c0, c1):
                prompt = lax.slice(frames, (t // ts - 1, ts - n, c0), (mp // ts, ts, c1), (t // ts, 1, 1))
                sample = lax.slice(frames, (mp // ts, ts - n, c0), (frames.shape[0], ts, c1))
                return prompt, sample

            cp, cs = last_rows(CONV_W - 1, 0, d_rnn)
            pp, ps = last_rows(POOL_BUF, 2 * d_rnn, 2 * d_rnn + d_pool)
            for key, val in (("hp", hp), ("hs", hs), ("cp", cp), ("cs", cs), ("pp", pp), ("ps", ps)):
                outs[key].append(val)
        else:
            qkv, gate, new_kv = _norm_matmul_att(x, norm_att[j], w_in_att, new_kv, layer=j, mp=mp, n_heads=n_heads)
            op = _attn_prompt(qkv, gate, bp=bp, t=t, n_heads=n_heads)
            os_ = _attn_sample(qkv, gate, cache_k, cache_v, layer=j, mp=mp, bs=bs, ts=ts, n_heads=n_heads)
            if layer + 1 < n_rec + n_att:
                x = _matmul_residual_rows(op, os_, w_out_att, j, x)
            else:
                y_prompt, y_sample = _matmul_residual_rows_norm(op, os_, w_out_att, j, x, norm_final)

    if (n_rec + n_att) % 2:
        y_prompt, y_sample = _final_norm(x, norm_final, mp)
    st = {k: jnp.stack(v) for k, v in outs.items()}
    kp, vp, ks, vs = new_kv
    prompt_shape = (n_att, bp, t, n_heads, HEAD_DIM)
    sample_shape = (n_att, bs, ts, n_heads, HEAD_DIM)
    return (y_prompt.reshape(bp, t, d), y_sample.reshape(bs, ts, d),
            kp.reshape(prompt_shape), vp.reshape(prompt_shape), st["hp"], st["cp"], st["pp"],
            ks.reshape(sample_shape), vs.reshape(sample_shape), st["hs"], st["cs"], st["ps"])


def kernel(x_prompt, x_sample, cache_k, cache_v, state_h, state_conv, state_pool, norm_rec, w_in_rec, conv_w, conv_b, gate_r_w, gate_r_b, gate_i_w, gate_i_b, rg_lambda, pool_w, pool_scale, w_out_rec, norm_att, w_in_att, w_out_att, norm_final):
    return _step(x_prompt, x_sample, cache_k, cache_v, state_h, state_conv, state_pool, norm_rec, w_in_rec,
                 conv_w, conv_b, gate_r_w, gate_r_b, gate_i_w, gate_i_b, rg_lambda, pool_w, pool_scale,
                 w_out_rec, norm_att, w_in_att, w_out_att, norm_final)
```

```python
import functools

import jax
import jax.numpy as jnp
from jax import lax
from jax.experimental import pallas as pl
from jax.experimental.pallas import tpu as pltpu

F32 = jnp.float32
BF16 = jnp.bfloat16

EPS = 1e-6
RG_C = 8.0
CONV_W = 4
POOL_WINDOWS = (2, 4, 8, 16)
POOL_BUF = max(POOL_WINDOWS) - 1
HEAD_DIM = 128
ATT_SCALE = HEAD_DIM ** -0.5
EXP_UNDERFLOW = -105.0

LANES = 128
SUBLANES = 8
VMEM_LIMIT_CAP = 60000 * 1024
VMEM_SLACK = 8 * 1024 * 1024


def _params(semantics, buffer_bytes):
    limit = min(VMEM_LIMIT_CAP, buffer_bytes + VMEM_SLACK)
    return pltpu.CompilerParams(dimension_semantics=semantics, vmem_limit_bytes=limit)


def _tile(n, pref):
    t = min(n, pref)
    while n % t:
        t //= 2
    return t


def _silu(x):
    half = 0.5 * x
    return half + half * jnp.tanh(half)


LOG2_E = 1.4426950408889634


def _softplus(x):
    return jnp.maximum(x, 0.0) + jnp.log(1.0 + jnp.exp2(jnp.abs(x) * -LOG2_E))


NORM_ROWS = 256


def _normalize_rows(x_ref, g_ref, xn_ref):
    @pl.when(pl.program_id(1) == 0)
    def _():
        g = g_ref[...]

        def body(c, _):
            rows = pl.ds(pl.multiple_of(c * NORM_ROWS, NORM_ROWS), NORM_ROWS)
            x = x_ref[rows, :]
            ms = jnp.mean(x * x, axis=-1, keepdims=True)
            xn_ref[rows, :] = (x * lax.rsqrt(ms + EPS) * g).astype(BF16)
            return 0

        lax.fori_loop(0, x_ref.shape[0] // NORM_ROWS, body, 0)


def _norm_mm_kernel(x_ref, g_ref, w_ref, o_ref, xn_ref):
    _normalize_rows(x_ref, g_ref, xn_ref)
    o_ref[...] = jnp.dot(xn_ref[...], w_ref[...], preferred_element_type=F32)


def _norm_mm_rows_kernel(top_ref, bottom_ref, g_ref, w_ref, o_ref, xn_ref, *, n_top):
    i = pl.program_id(0)
    pl.when(i < n_top)(functools.partial(_normalize_rows, top_ref, g_ref, xn_ref))
    pl.when(i >= n_top)(functools.partial(_normalize_rows, bottom_ref, g_ref, xn_ref))
    o_ref[...] = jnp.dot(xn_ref[...], w_ref[...], preferred_element_type=F32)


def _norm_matmul(x, g, w_all, layer, *, tm_pref=1024, tn_pref=1024):
    m, k = x.shape
    n = w_all.shape[2]
    tm, tn = _tile(m, tm_pref), _tile(n, tn_pref)
    vmem = 2 * tm * k * 4 + tm * k * 2 + 2 * k * tn * 2 + 2 * tm * tn * 4
    return pl.pallas_call(
        _norm_mm_kernel,
        grid=(m // tm, n // tn),
        in_specs=[pl.BlockSpec((tm, k), lambda i, j: (i, 0)),
                  pl.BlockSpec((1, k), lambda i, j: (0, 0)),
                  pl.BlockSpec((None, k, tn), lambda i, j: (layer, 0, j))],
        out_specs=pl.BlockSpec((tm, tn), lambda i, j: (i, j)),
        out_shape=jax.ShapeDtypeStruct((m, n), F32),
        scratch_shapes=[pltpu.VMEM((tm, k), BF16)],
        compiler_params=_params(("arbitrary", "arbitrary"), vmem),
        name="norm_matmul",
    )(x, g.reshape(1, k), w_all)


def _norm_matmul_rows(top, bottom, g, w_all, layer, *, tm_pref=1024, tn_pref=1024):
    k = top.shape[1]
    m, n = top.shape[0] + bottom.shape[0], w_all.shape[2]
    tm = _tile(top.shape[0], tm_pref)
    while bottom.shape[0] % tm:
        tm //= 2
    tn = _tile(n, tn_pref)
    n_top = top.shape[0] // tm
    vmem = 3 * tm * k * 4 + tm * k * 2 + 2 * k * tn * 2 + 2 * tm * tn * 4
    return pl.pallas_call(
        functools.partial(_norm_mm_rows_kernel, n_top=n_top),
        grid=(m // tm, n // tn),
        in_specs=[pl.BlockSpec((tm, k), lambda i, j: (jnp.minimum(i, n_top - 1), 0)),
                  pl.BlockSpec((tm, k), lambda i, j: (jnp.maximum(i - n_top, 0), 0),
                               pipeline_mode=pl.Buffered(1)),
                  pl.BlockSpec((1, k), lambda i, j: (0, 0)),
                  pl.BlockSpec((None, k, tn), lambda i, j: (layer, 0, j))],
        out_specs=pl.BlockSpec((tm, tn), lambda i, j: (i, j)),
        out_shape=jax.ShapeDtypeStruct((m, n), F32),
        scratch_shapes=[pltpu.VMEM((tm, k), BF16)],
        compiler_params=_params(("arbitrary", "arbitrary"), vmem),
        name="norm_matmul_rows",
    )(top, bottom, g.reshape(1, k), w_all)


def _norm_mm_att_kernel(x_ref, g_ref, w_ref, *refs, layer, n_top, per):
    qkv_ref, gt_ref, kp_hbm, vp_hbm, ks_hbm, vs_hbm, xn_ref, kv_scr, sem = refs[-9:]
    layers = (layer,) if len(refs) > 9 else range(kp_hbm.shape[0])
    _normalize_rows(x_ref, g_ref, xn_ref)
    i, j = pl.program_id(0), pl.program_id(1)
    tm, tn = qkv_ref.shape
    hpt = tn // HEAD_DIM
    top = i < n_top

    def project():
        return jnp.dot(xn_ref[...], w_ref[...], preferred_element_type=F32)

    def copies(jj, prompt_rows):
        slot = (jj - per) % 2
        dst = ((kp_hbm, ks_hbm), (vp_hbm, vs_hbm))[jj // per - 1][0 if prompt_rows else 1]
        rows = pl.ds(i * tm if prompt_rows else (i - n_top) * tm, tm)
        return [pltpu.make_async_copy(kv_scr.at[slot, :, pl.ds(h * HEAD_DIM, HEAD_DIM)],
                                      dst.at[l, rows, (jj % per) * hpt + h, :], sem.at[slot])
                for l in layers for h in range(hpt)]

    def for_rows(jj, action):
        @pl.when(top)
        def _():
            for c in copies(jj, True):
                action(c)

        @pl.when(jnp.logical_not(top))
        def _():
            for c in copies(jj, False):
                action(c)

    @pl.when(j < per)
    def _():
        qkv_ref[...] = (project() * ATT_SCALE).astype(BF16)

    for jj in range(per, 3 * per + 2):
        @pl.when(j == jj)
        def _(jj=jj):
            if jj - 2 >= per:
                for_rows(jj - 2, lambda c: c.wait())
            if jj < 3 * per:
                slot = (jj - per) % 2
                kv_scr[slot] = project()
                qkv_ref[...] = kv_scr[slot].astype(BF16)
                for_rows(jj, lambda c: c.start())

    @pl.when(j >= 3 * per)
    def _():
        gt_ref[...] = project()


def _norm_matmul_att(x, g, w_all, new_kv, *, layer, mp, n_heads, tm_pref=1024):
    m, k = x.shape
    n_layers = w_all.shape[0]
    d_att = n_heads * HEAD_DIM
    tn = d_att // 2
    per = d_att // tn
    assert per >= 2 and w_all.shape[2] == 4 * d_att
    tm = _tile(mp, tm_pref)
    while (m - mp) % tm:
        tm //= 2
    any_spec = pl.BlockSpec(memory_space=pl.ANY)
    carried = () if new_kv is None else tuple(new_kv)
    kv_shapes = [jax.ShapeDtypeStruct((n_layers, rows, n_heads, HEAD_DIM), F32) for rows in (mp, mp, m - mp, m - mp)]
    vmem = 2 * tm * k * 4 + tm * k * 2 + 2 * k * tn * 2 + 2 * tm * tn * (2 + 4) + 3 * tm * tn * 4
    outs = pl.pallas_call(
        functools.partial(_norm_mm_att_kernel, layer=layer, n_top=mp // tm, per=per),
        grid=(m // tm, 4 * per),
        in_specs=[pl.BlockSpec((tm, k), lambda i, j: (i, 0)),
                  pl.BlockSpec((1, k), lambda i, j: (0, 0)),
                  pl.BlockSpec((None, k, tn), lambda i, j: (layer, 0, j))] + [any_spec] * len(carried),
        out_specs=[pl.BlockSpec((tm, tn), lambda i, j: (i, jnp.minimum(j, 3 * per - 1))),
                   pl.BlockSpec((tm, tn), lambda i, j: (i, jnp.maximum(j - 3 * per, 0))),
                   any_spec, any_spec, any_spec, any_spec],
        out_shape=[jax.ShapeDtypeStruct((m, 3 * d_att), BF16), jax.ShapeDtypeStruct((m, d_att), F32)] + kv_shapes,
        input_output_aliases={3 + n: 2 + n for n in range(len(carried))},
        scratch_shapes=[pltpu.VMEM((tm, k), BF16), pltpu.VMEM((2, tm, tn), F32),
                        pltpu.SemaphoreType.DMA((2,))],
        compiler_params=_params(("arbitrary", "arbitrary"), vmem),
        name="norm_matmul_att",
    )(x, g.reshape(1, k), w_all, *carried)
    return outs[0], outs[1], tuple(outs[2:])


def _mm_res_kernel(*refs, n_pairs, n_top):
    lhs, ws = refs[:n_pairs], refs[n_pairs:2 * n_pairs]
    res_refs, out_ref = refs[2 * n_pairs:-1], refs[-1]

    def emit(res_ref):
        acc = res_ref[...]
        for l, w in zip(lhs, ws):
            acc = acc + jnp.dot(l[...], w[...], preferred_element_type=F32)
        out_ref[...] = acc

    if len(res_refs) == 1:
        emit(res_refs[0])
    else:
        i = pl.program_id(0)
        pl.when(i < n_top)(functools.partial(emit, res_refs[0]))
        pl.when(i >= n_top)(functools.partial(emit, res_refs[1]))


def _matmul_residual(lhs_list, w_all, layer, res, *, tm_pref=1024, tn_pref=1024):
    parts = res if isinstance(res, tuple) else (res,)
    m, n = sum(p.shape[0] for p in parts), parts[0].shape[1]
    tm = _tile(parts[0].shape[0], tm_pref)
    while any(p.shape[0] % tm for p in parts):
        tm //= 2
    tn = _tile(n, tn_pref)
    n_top = parts[0].shape[0] // tm
    ks = [l.shape[1] for l in lhs_list]
    starts = [sum(ks[:p]) for p in range(len(ks))]
    assert all(r % k == 0 for r, k in zip(starts, ks)) and sum(ks) == w_all.shape[1]
    vmem = sum(2 * tm * k * 2 + 2 * k * tn * 2 for k in ks) + (2 + 2 * len(parts)) * tm * tn * 4
    res_specs = ([pl.BlockSpec((tm, tn), lambda i, j: (i, j))] if len(parts) == 1 else
                 [pl.BlockSpec((tm, tn), lambda i, j: (jnp.minimum(i, n_top - 1), j)),
                  pl.BlockSpec((tm, tn), lambda i, j: (jnp.maximum(i - n_top, 0), j))])
    in_specs = ([pl.BlockSpec((tm, k), lambda i, j: (i, 0)) for k in ks]
                + [pl.BlockSpec((None, k, tn), lambda i, j, blk=r // k: (layer, blk, j)) for r, k in zip(starts, ks)]
                + res_specs)
    return pl.pallas_call(
        functools.partial(_mm_res_kernel, n_pairs=len(ks), n_top=n_top),
        grid=(m // tm, n // tn),
        in_specs=in_specs,
        out_specs=pl.BlockSpec((tm, tn), lambda i, j: (i, j)),
        out_shape=jax.ShapeDtypeStruct((m, n), F32),
        compiler_params=_params(("arbitrary", "arbitrary"), vmem),
        name="matmul_residual",
    )(*lhs_list, *([w_all] * len(ks)), *parts)


def _mm_res_rows_kernel(top_ref, bottom_ref, w_ref, res_ref, out_ref, *, n_top):
    i = pl.program_id(0)

    def emit(lhs_ref):
        out_ref[...] = res_ref[...] + jnp.dot(lhs_ref[...], w_ref[...], preferred_element_type=F32)

    pl.when(i < n_top)(functools.partial(emit, top_ref))
    pl.when(i >= n_top)(functools.partial(emit, bottom_ref))


def _mm_res_rows_norm_kernel(top_ref, bottom_ref, w_ref, res_ref, g_ref, top_out, bottom_out, *, n_top):
    i = pl.program_id(0)

    def emit(lhs_ref, out_ref):
        y = res_ref[...] + jnp.dot(lhs_ref[...], w_ref[...], preferred_element_type=F32)
        ms = jnp.mean(y * y, axis=-1, keepdims=True)
        out_ref[...] = y * lax.rsqrt(ms + EPS) * g_ref[...]

    pl.when(i < n_top)(functools.partial(emit, top_ref, top_out))
    pl.when(i >= n_top)(functools.partial(emit, bottom_ref, bottom_out))


def _matmul_residual_rows_norm(top, bottom, w_all, layer, res, g, *, tm_pref=512):
    m, n = res.shape
    k = w_all.shape[1]
    tm = _tile(top.shape[0], tm_pref)
    while bottom.shape[0] % tm:
        tm //= 2
    n_top = top.shape[0] // tm
    vmem = 4 * tm * k * 2 + 2 * k * n * 2 + 6 * tm * n * 4
    return pl.pallas_call(
        functools.partial(_mm_res_rows_norm_kernel, n_top=n_top),
        grid=(m // tm,),
        in_specs=[pl.BlockSpec((tm, k), lambda i: (jnp.minimum(i, n_top - 1), 0)),
                  pl.BlockSpec((tm, k), lambda i: (jnp.maximum(i - n_top, 0), 0)),
                  pl.BlockSpec((None, k, n), lambda i: (layer, 0, 0)),
                  pl.BlockSpec((tm, n), lambda i: (i, 0)),
                  pl.BlockSpec((1, n), lambda i: (0, 0))],
        out_specs=[pl.BlockSpec((tm, n), lambda i: (jnp.minimum(i, n_top - 1), 0)),
                   pl.BlockSpec((tm, n), lambda i: (jnp.maximum(i - n_top, 0), 0))],
        out_shape=[jax.ShapeDtypeStruct((top.shape[0], n), F32), jax.ShapeDtypeStruct((bottom.shape[0], n), F32)],
        compiler_params=_params(("arbitrary",), vmem),
        name="matmul_residual_rows_norm",
    )(top, bottom, w_all, res, g.reshape(1, n))


def _matmul_residual_rows(top, bottom, w_all, layer, res, *, tm_pref=1024, tn_pref=1024):
    m, n = res.shape
    k = w_all.shape[1]
    tm = _tile(top.shape[0], tm_pref)
    while bottom.shape[0] % tm:
        tm //= 2
    tn = _tile(n, tn_pref)
    n_top = top.shape[0] // tm
    vmem = 4 * tm * k * 2 + 2 * k * tn * 2 + 4 * tm * tn * 4
    return pl.pallas_call(
        functools.partial(_mm_res_rows_kernel, n_top=n_top),
        grid=(m // tm, n // tn),
        in_specs=[pl.BlockSpec((tm, k), lambda i, j: (jnp.minimum(i, n_top - 1), 0)),
                  pl.BlockSpec((tm, k), lambda i, j: (jnp.maximum(i - n_top, 0), 0)),
                  pl.BlockSpec((None, k, tn), lambda i, j: (layer, 0, j)),
                  pl.BlockSpec((tm, tn), lambda i, j: (i, j))],
        out_specs=pl.BlockSpec((tm, tn), lambda i, j: (i, j)),
        out_shape=jax.ShapeDtypeStruct((m, n), F32),
        compiler_params=_params(("arbitrary", "arbitrary"), vmem),
        name="matmul_residual_rows",
    )(top, bottom, w_all, res)


def _final_norm_kernel(x_ref, g_ref, top_ref, bottom_ref, *, n_top):
    x = x_ref[...]
    ms = jnp.mean(x * x, axis=-1, keepdims=True)
    y = x * lax.rsqrt(ms + EPS) * g_ref[...]
    i = pl.program_id(0)

    @pl.when(i < n_top)
    def _():
        top_ref[...] = y

    @pl.when(i >= n_top)
    def _():
        bottom_ref[...] = y


def _final_norm(x, g, mp):
    m, d = x.shape
    tm = _tile(mp, 256)
    while (m - mp) % tm:
        tm //= 2
    n_top = mp // tm
    return pl.pallas_call(
        functools.partial(_final_norm_kernel, n_top=n_top),
        grid=(m // tm,),
        in_specs=[pl.BlockSpec((tm, d), lambda i: (i, 0)), pl.BlockSpec((1, d), lambda i: (0, 0))],
        out_specs=[pl.BlockSpec((tm, d), lambda i: (jnp.minimum(i, n_top - 1), 0)),
                   pl.BlockSpec((tm, d), lambda i: (jnp.maximum(i - n_top, 0), 0))],
        out_shape=[jax.ShapeDtypeStruct((mp, d), F32), jax.ShapeDtypeStruct((m - mp, d), F32)],
        compiler_params=_params(("arbitrary",), 6 * tm * d * 4),
        name="final_norm",
    )(x, g.reshape(1, d))


SCAN_ROWS = 256
SCAN_STREAMS = 8


def _scan_tiles(a, u, carry):
    r = a.shape[0]
    nt = r // SUBLANES
    a3 = a.reshape(nt, SUBLANES, LANES)
    u3 = u.reshape(nt, SUBLANES, LANES)
    sub = lax.broadcasted_iota(jnp.int32, a3.shape, 1)
    for d in (1, 2, 4):
        keep = sub >= d
        a_prev = pltpu.roll(a3, d, 1)
        u_prev = pltpu.roll(u3, d, 1)
        u3 = jnp.where(keep, a3 * u_prev + u3, u3)
        a3 = jnp.where(keep, a3 * a_prev, a3)
    hs = []
    for t in range(nt):
        h_t = u3[t] + a3[t] * carry
        carry = h_t[SUBLANES - 1:SUBLANES, :]
        hs.append(h_t)
    return jnp.concatenate(hs, axis=0), carry


def _rglru_kernel(xa_ref, ga_ref, h0_ref, sc_ref, cw_ref, cb_ref, wr_ref, br_ref, wi_ref, bi_ref, lam_ref,
                  out_ref, hp_ref, hs_ref, *, bp, t, bs, ts):
    mp = bp * t
    cw = cw_ref[...]
    cb = cb_ref[...]
    half_wr = (0.5 * wr_ref[...]).astype(BF16)
    half_wi = (0.5 * wi_ref[...]).astype(BF16)
    half_br = 0.5 * br_ref[...]
    half_bi = 0.5 * bi_ref[...]
    half_c = (-0.5 * RG_C) * _softplus(-lam_ref[...])

    def conv_taps(tap):
        y = cb + tap(0) * cw[0:1]
        for k in range(1, CONV_W):
            y = y + tap(k) * cw[k:k + 1]
        return y

    def conv(ext, n):
        def tap(k):
            back = CONV_W - 1 - k
            return (ext if back == 0 else pltpu.roll(ext, back, 0))[SUBLANES:]

        return conv_taps(tap)

    def decay_and_input(xc):
        xb = xc.astype(BF16)
        tr = jnp.tanh(jnp.dot(xb, half_wr, preferred_element_type=F32) + half_br)
        ti = jnp.tanh(jnp.dot(xb, half_wi, preferred_element_type=F32) + half_bi)
        log_a = half_c + half_c * tr
        half_x = 0.5 * xc
        gated_x = half_x + half_x * ti
        a = jnp.exp(log_a)
        v = -jnp.tanh(log_a) * (1.0 + a * a)
        root = jnp.where(v > 0.0, v * lax.rsqrt(v), 0.0)
        return a, root * gated_x

    def emit(rows, a, u, carry):
        h, carry = _scan_tiles(a, u, carry)
        out_ref[rows, :] = (h * _silu(ga_ref[rows, :])).astype(out_ref.dtype)
        return carry

    rp = _tile(t, SCAN_ROWS)

    def first_chunk(b):
        rows = pl.ds(b * t, rp)
        ext = jnp.concatenate([jnp.zeros((SUBLANES, LANES), F32), xa_ref[rows, :]], axis=0)
        a, u = decay_and_input(conv(ext, rp))
        return emit(rows, a, u, jnp.zeros((1, LANES), F32))

    def later_chunk(c, carries):
        new = []
        for b in range(bp):
            r0 = b * t + c * rp
            rows = pl.ds(pl.multiple_of(r0, SUBLANES), rp)
            xc = conv_taps(lambda k: xa_ref[pl.ds(r0 - (CONV_W - 1 - k), rp), :])
            a, u = decay_and_input(xc)
            new.append(emit(rows, a, u, carries[b]))
        return tuple(new)

    carries = lax.fori_loop(1, t // rp, later_chunk, tuple(first_chunk(b) for b in range(bp)))
    for b in range(bp):
        hp_ref[b:b + 1, :] = carries[b]

    ns = _tile(bs, SCAN_STREAMS)

    def sample_group(c, _):
        rows = [pl.ds(pl.multiple_of(mp + (c * ns + k) * ts, SUBLANES), ts) for k in range(ns)]
        xc = [conv(jnp.concatenate([sc_ref[c * ns + k], xa_ref[rows[k], :]], axis=0), ts) for k in range(ns)]
        a, u = decay_and_input(jnp.concatenate(xc, axis=0))
        for k in range(ns):
            seg = slice(k * ts, (k + 1) * ts)
            s = c * ns + k
            hs_ref[pl.ds(s, 1), :] = emit(rows[k], a[seg], u[seg], h0_ref[pl.ds(s, 1), :])
        return 0

    lax.fori_loop(0, bs // ns, sample_group, 0)


def _rglru(proj, h0, sc_pad, cw, cb, wr, br, wi, bi, lam, *, bp, t, bs, ts, d_rnn):
    m = proj.shape[0]
    nb = d_rnn // LANES
    row = lambda v: v.reshape(1, d_rnn)
    vec_spec = pl.BlockSpec((1, LANES), lambda n: (0, n))
    w_spec = pl.BlockSpec((None, LANES, LANES), lambda n: (n, 0, 0))
    vmem = 2 * (2 * m * LANES * 4 + m * LANES * 2)
    return pl.pallas_call(
        functools.partial(_rglru_kernel, bp=bp, t=t, bs=bs, ts=ts),
        grid=(nb,),
        in_specs=[pl.BlockSpec((m, LANES), lambda n: (0, n)),
                  pl.BlockSpec((m, LANES), lambda n: (0, nb + n)),
                  pl.BlockSpec((bs, LANES), lambda n: (0, n)),
                  pl.BlockSpec((bs, SUBLANES, LANES), lambda n: (0, 0, n)),
                  pl.BlockSpec((CONV_W, LANES), lambda n: (0, n)),
                  vec_spec, w_spec, vec_spec, w_spec, vec_spec, vec_spec],
        out_specs=[pl.BlockSpec((m, LANES), lambda n: (0, n)),
                   pl.BlockSpec((bp, LANES), lambda n: (0, n)),
                   pl.BlockSpec((bs, LANES), lambda n: (0, n))],
        out_shape=[jax.ShapeDtypeStruct((m, d_rnn), BF16),
                   jax.ShapeDtypeStruct((bp, d_rnn), F32),
                   jax.ShapeDtypeStruct((bs, d_rnn), F32)],
        compiler_params=_params(("arbitrary",), vmem),
        name="rglru_mixer",
    )(proj, proj, h0, sc_pad, cw, row(cb), wr, row(br), wi, row(bi), row(lam))


POOL_TAIL = 16
POOL_ROWS = 256


def _pool_kernel(xb_ref, gb_ref, sp_ref, pw_ref, ps_ref, out_ref, m_scr, tail_scr, *, npb, bpt, ts):
    g = pl.program_id(0)
    rb = pl.program_id(1)
    rbk, gw = xb_ref.shape
    wf = lax.shift_left(jnp.int32(2), g).astype(F32)

    def window_means(ext, n, pos0):
        s2 = ext[1:] + ext[:-1]
        s4 = s2[2:] + s2[:-2]
        s8 = s4[4:] + s4[:-4]
        s16 = s8[8:] + s8[:-8]
        x = ext[POOL_TAIL:]
        win = jnp.where(g == 0, s2[POOL_TAIL - 1:],
                        jnp.where(g == 1, s4[POOL_TAIL - 3:],
                                  jnp.where(g == 2, s8[POOL_TAIL - 7:], s16[POOL_TAIL - 15:])))
        if pos0 is None:
            cnt = wf
        else:
            pos = pos0 + lax.broadcasted_iota(jnp.int32, (n, gw), 0)
            cnt = jnp.minimum(wf, (pos + 1).astype(F32))
        return win / cnt - x

    @pl.when(rb < npb)
    def _():
        blk = rb % bpt

        @pl.when(blk == 0)
        def _():
            tail_scr[...] = jnp.zeros_like(tail_scr)

        rc = _tile(rbk, POOL_ROWS)

        def body(c, tail):
            rows = pl.ds(pl.multiple_of(c * rc, SUBLANES), rc)
            x = xb_ref[rows, :]
            ext = jnp.concatenate([tail, x], axis=0)
            m_scr[rows, :] = window_means(ext, rc, blk * rbk + c * rc).astype(BF16)
            return x[rc - POOL_TAIL:, :]

        tail_scr[...] = lax.fori_loop(0, rbk // rc, body, tail_scr[...])

    @pl.when(rb >= npb)
    def _():
        s0 = (rb - npb) * (rbk // ts)

        def body(s, _):
            rows = pl.ds(pl.multiple_of(s * ts, SUBLANES), ts)
            ext = jnp.concatenate([sp_ref[s0 + s], xb_ref[rows, :]], axis=0)
            m_scr[rows, :] = window_means(ext, ts, None).astype(BF16)
            return 0

        lax.fori_loop(0, rbk // ts, body, 0)

    y = jnp.dot(m_scr[...], pw_ref[...].astype(BF16), preferred_element_type=F32) * ps_ref[...]
    out_ref[...] = (y * _silu(gb_ref[...])).astype(out_ref.dtype)


def _pool(proj, sp_pad, pw, ps, *, bp, t, bs, ts, d_rnn, d_pool):
    m = proj.shape[0]
    ng = len(POOL_WINDOWS)
    gw = d_pool // ng
    ms = bs * ts
    rbk = min(1024, t, ms)
    while t % rbk or ms % rbk:
        rbk //= 2
    npb = bp * t // rbk
    xcol = 2 * d_rnn // gw
    gcol = (2 * d_rnn + d_pool) // gw
    vmem = 2 * (2 * rbk * gw * 4 + rbk * gw * 2) + rbk * gw * 2 + 2 * bs * POOL_TAIL * gw * 4 + 2 * gw * gw * 4
    return pl.pallas_call(
        functools.partial(_pool_kernel, npb=npb, bpt=t // rbk, ts=ts),
        grid=(ng, m // rbk),
        in_specs=[pl.BlockSpec((rbk, gw), lambda g, r: (r, xcol + g)),
                  pl.BlockSpec((rbk, gw), lambda g, r: (r, gcol + g)),
                  pl.BlockSpec((bs, POOL_TAIL, gw), lambda g, r: (0, 0, g)),
                  pl.BlockSpec((None, gw, gw), lambda g, r: (g, 0, 0)),
                  pl.BlockSpec((1, gw), lambda g, r: (0, g))],
        out_specs=pl.BlockSpec((rbk, gw), lambda g, r: (r, g)),
        out_shape=jax.ShapeDtypeStruct((m, d_pool), BF16),
        scratch_shapes=[pltpu.VMEM((rbk, gw), BF16), pltpu.VMEM((POOL_TAIL, gw), F32)],
        compiler_params=_params(("arbitrary", "arbitrary"), vmem),
        name="pool_mixer",
    )(proj, proj, sp_pad, pw, ps.reshape(1, d_pool))


def _suffix_matrix(n):
    j = lax.broadcasted_iota(jnp.int32, (n, n), 0)
    s = lax.broadcasted_iota(jnp.int32, (n, n), 1)
    u = jnp.where(j > s, -1.0, 0.0).astype(BF16)
    return jnp.concatenate([u, u], axis=0)


def _qk(q, k):
    return lax.dot_general(q, k, (((1,), (1,)), ((), ())), preferred_element_type=F32)


def _strictly_earlier(shape):
    return lax.broadcasted_iota(jnp.int32, shape, 1) < lax.broadcasted_iota(jnp.int32, shape, 0)


def _sb_weights(z, uu, carry=None, mask=None):
    sp = _softplus(z)
    spm = sp if mask is None else jnp.where(mask, sp, 0.0)
    hi = spm.astype(BF16)
    lo = (spm - hi.astype(F32)).astype(BF16)
    e = z - sp + jnp.dot(jnp.concatenate([hi, lo], axis=1), uu, preferred_element_type=F32)
    w = jnp.exp(e if carry is None else e + carry)
    if mask is not None:
        w = jnp.where(mask, w, 0.0)
    return w.astype(BF16), -jnp.sum(spm, axis=1, keepdims=True)


def _sb_tile(q, k, v, uu, carry=None, mask=None):
    w, tot = _sb_weights(_qk(q, k), uu, carry, mask)
    return jnp.dot(w, v, preferred_element_type=F32), tot


PROMPT_TILE = 256
PROMPT_ALWAYS = 2


def _attn_prompt_kernel(q_ref, k_ref, v_ref, gt_ref, uu_ref, o_ref, acc_scr, car_scr, worst_ref, *, tq):
    uu = uu_ref[...]
    mask = _strictly_earlier((tq, tq))
    nq = q_ref.shape[0] // tq

    groups = [range(kj, min(kj + PROMPT_ALWAYS, nq)) for kj in range(nq)]
    log_beta, spm = {}, {}
    for kj, qs in enumerate(groups):
        zb = _qk(q_ref[pl.ds(kj * tq, len(qs) * tq), :], k_ref[pl.ds(kj * tq, tq), :])
        for n, qi in enumerate(qs):
            zt = zb[n * tq:(n + 1) * tq]
            st = _softplus(zt)
            log_beta[qi, kj] = zt - st
            spm[qi, kj] = jnp.where(mask, st, 0.0) if qi == kj else st
    order = list(spm)
    stacked = jnp.concatenate([spm[tile] for tile in order], axis=0)
    hi = stacked.astype(BF16)
    lo = (stacked - hi.astype(F32)).astype(BF16)
    suffix = jnp.dot(jnp.concatenate([hi, lo], axis=1), uu, preferred_element_type=F32)
    suffix = {tile: suffix[n * tq:(n + 1) * tq] for n, tile in enumerate(order)}
    total = {tile: -jnp.sum(spm[tile], axis=1, keepdims=True) for tile in order}
    acc = {}
    for kj, qs in enumerate(groups):
        ws = []
        for qi in qs:
            e = log_beta[qi, kj] + suffix[qi, kj]
            for newer in range(qi, kj, -1):
                e = e + total[qi, newer]
            w = jnp.exp(e)
            ws.append((jnp.where(mask, w, 0.0) if qi == kj else w).astype(BF16))
        pv = jnp.dot(jnp.concatenate(ws, axis=0), v_ref[pl.ds(kj * tq, tq), :], preferred_element_type=F32)
        for n, qi in enumerate(qs):
            part = pv[n * tq:(n + 1) * tq]
            acc[qi] = acc[qi] + part if qi in acc else part
    for qi in range(nq):
        rows = pl.ds(qi * tq, tq)
        o_ref[rows, :] = (acc[qi] * _silu(gt_ref[rows, :])).astype(o_ref.dtype)
        if qi >= PROMPT_ALWAYS:
            carry = total[qi, qi]
            for kj in range(qi - 1, qi - PROMPT_ALWAYS, -1):
                carry = carry + total[qi, kj]
            acc_scr[qi] = acc[qi]
            car_scr[qi] = carry
            worst_ref[qi] = jnp.max(carry)

    def finish(qi, _):
        @pl.when(worst_ref[qi] >= EXP_UNDERFLOW)
        def _():
            rows = pl.ds(pl.multiple_of(qi * tq, tq), tq)
            q = q_ref[rows, :]

            def more(s):
                return jnp.logical_and(s[0] >= 0, jnp.max(s[2]) >= EXP_UNDERFLOW)

            def older(s):
                j, acc, carry = s
                old = pl.ds(pl.multiple_of(j * tq, tq), tq)
                pv, tot = _sb_tile(q, k_ref[old, :], v_ref[old, :], uu, carry)
                return j - 1, acc + pv, carry + tot

            _, acc, _ = lax.while_loop(more, older, (qi - PROMPT_ALWAYS, acc_scr[qi], car_scr[qi]))
            o_ref[rows, :] = (acc * _silu(gt_ref[rows, :])).astype(o_ref.dtype)

        return 0

    lax.fori_loop(PROMPT_ALWAYS, nq, finish, 0)


def _attn_prompt(qkv, gate, *, bp, t, n_heads):
    tq = _tile(t, PROMPT_TILE)
    d_att = n_heads * HEAD_DIM
    nq = t // tq
    state = pltpu.VMEM((nq, tq, LANES), F32), pltpu.VMEM((nq, tq, 1), F32), pltpu.SMEM((nq,), F32)
    always = PROMPT_ALWAYS * nq
    vmem = (2 * (3 * t * LANES * 2 + t * LANES * 4 + 2 * tq * tq * 2 + t * LANES * 2) + 2 * t * LANES * 4
            + 5 * always * tq * tq * 4)
    col = lambda c: pl.BlockSpec((t, LANES), lambda b, h: (b, c * n_heads + h))
    return pl.pallas_call(
        functools.partial(_attn_prompt_kernel, tq=tq),
        grid=(bp, n_heads),
        in_specs=[col(0), col(1), col(2), col(0), pl.BlockSpec((2 * tq, tq), lambda b, h: (0, 0))],
        out_specs=pl.BlockSpec((t, LANES), lambda b, h: (b, h)),
        out_shape=jax.ShapeDtypeStruct((bp * t, d_att), BF16),
        scratch_shapes=list(state),
        compiler_params=_params(("arbitrary", "arbitrary"), vmem),
        name="attn_prompt",
    )(qkv, qkv, qkv, gate, _suffix_matrix(tq))


def _attn_sample_kernel(q_ref, kn_ref, vn_ref, gt_ref, kc_hbm, vc_hbm, un_ref, uc_ref, o_ref,
                        kfirst, vfirst, kmore, vmore, acc_scr, car_scr, sem, *, layer, pc):
    b = pl.program_id(0)
    n_heads, ts = kmore.shape[0], q_ref.shape[0]
    newest = kc_hbm.shape[2] // pc - 1
    slot = b % 2

    def copies(stream, chunk, kdst, vdst, ksem, vsem):
        pos = pl.ds(chunk * pc, pc)
        out = []
        for h in range(n_heads):
            out.append(pltpu.make_async_copy(kc_hbm.at[layer, stream, pos, h, :], kdst.at[h], ksem))
            out.append(pltpu.make_async_copy(vc_hbm.at[layer, stream, pos, h, :], vdst.at[h], vsem))
        return out

    def first_copies(stream, s):
        return copies(stream, newest, kfirst.at[s], vfirst.at[s], sem.at[s, 0], sem.at[s, 1])

    @pl.when(b == 0)
    def _():
        for c in first_copies(0, 0):
            c.start()

    @pl.when(b + 1 < pl.num_programs(0))
    def _():
        for c in first_copies(b + 1, 1 - slot):
            c.start()

    def head(ref, h):
        return ref[:, h * HEAD_DIM:(h + 1) * HEAD_DIM]

    def scores(keys):
        return jnp.concatenate([_qk(head(q_ref, h), keys(h)) for h in range(n_heads)], axis=0)

    def weighted(w, values):
        return jnp.concatenate([jnp.dot(w[h * ts:(h + 1) * ts], values(h), preferred_element_type=F32)
                                for h in range(n_heads)], axis=0)

    pad = jnp.zeros((LANES - ts, HEAD_DIM), BF16)
    query = lax.broadcasted_iota(jnp.int32, (n_heads, ts, LANES), 1).reshape(n_heads * ts, LANES)
    mask = lax.broadcasted_iota(jnp.int32, (n_heads * ts, LANES), 1) < query
    w, carry = _sb_weights(scores(lambda h: jnp.concatenate([head(kn_ref, h), pad], axis=0)),
                           un_ref[...], None, mask)
    acc = weighted(w, lambda h: jnp.concatenate([head(vn_ref, h), pad], axis=0))

    def chunk(kbuf, vbuf, acc, carry):
        w, tot = _sb_weights(scores(lambda h: kbuf[h].astype(BF16)), uc_ref[...], carry)
        return acc + weighted(w, lambda h: vbuf[h].astype(BF16)), carry + tot

    for c in first_copies(b, slot):
        c.wait()
    acc_scr[...], car_scr[...] = chunk(kfirst.at[slot], vfirst.at[slot], acc, carry)

    def more(s):
        return jnp.logical_and(s[0] >= 0, s[1] >= EXP_UNDERFLOW)

    def older(s):
        cs = copies(b, s[0], kmore, vmore, sem.at[2, 0], sem.at[2, 1])
        for c in cs:
            c.start()
        for c in cs:
            c.wait()
        acc_scr[...], car_scr[...] = chunk(kmore, vmore, acc_scr[...], car_scr[...])
        return s[0] - 1, jnp.max(car_scr[...])

    lax.while_loop(more, older, (newest - 1, jnp.max(car_scr[...])))

    for h in range(n_heads):
        o_ref[:, h * HEAD_DIM:(h + 1) * HEAD_DIM] = (acc_scr[pl.ds(h * ts, ts), :]
                                                     * _silu(head(gt_ref, h))).astype(o_ref.dtype)


def _attn_sample(qkv, gate, cache_k, cache_v, *, layer, mp, bs, ts, n_heads):
    p = cache_k.shape[2]
    d_att = n_heads * HEAD_DIM
    pc = _tile(p, 256)
    r0 = mp // ts
    chunk_bytes = n_heads * pc * HEAD_DIM * 4
    vmem = (2 * (3 * ts * d_att * 2 + ts * d_att * 4 + ts * d_att * 2) + 6 * chunk_bytes
            + 2 * n_heads * ts * LANES * 4 + 2 * 2 * (LANES * LANES + pc * pc) * 2)
    new_spec = lambda c: pl.BlockSpec((ts, d_att), lambda b: (r0 + b, c))
    chunk_buf = lambda n: pltpu.VMEM(n + (n_heads, pc, HEAD_DIM), F32)
    return pl.pallas_call(
        functools.partial(_attn_sample_kernel, layer=layer, pc=pc),
        grid=(bs,),
        in_specs=[new_spec(0), new_spec(1), new_spec(2), new_spec(0),
                  pl.BlockSpec(memory_space=pl.ANY), pl.BlockSpec(memory_space=pl.ANY),
                  pl.BlockSpec((2 * LANES, LANES), lambda b: (0, 0)),
                  pl.BlockSpec((2 * pc, pc), lambda b: (0, 0))],
        out_specs=pl.BlockSpec((ts, d_att), lambda b: (b, 0)),
        out_shape=jax.ShapeDtypeStruct((bs * ts, d_att), BF16),
        scratch_shapes=[chunk_buf((2,)), chunk_buf((2,)), chunk_buf(()), chunk_buf(()),
                        pltpu.VMEM((n_heads * ts, LANES), F32), pltpu.VMEM((n_heads * ts, 1), F32),
                        pltpu.SemaphoreType.DMA((3, 2))],
        compiler_params=_params(("arbitrary",), vmem),
        name="attn_sample",
    )(qkv, qkv, qkv, gate, cache_k, cache_v, _suffix_matrix(LANES), _suffix_matrix(pc))


@jax.jit
def _step(x_prompt, x_sample, cache_k, cache_v, state_h, state_conv, state_pool,
          norm_rec, w_in_rec, conv_w, conv_b, gate_r_w, gate_r_b, gate_i_w, gate_i_b, rg_lambda,
          pool_w, pool_scale, w_out_rec, norm_att, w_in_att, w_out_att, norm_final):
    bp, t, d = x_prompt.shape
    bs, ts, _ = x_sample.shape
    n_rec, n_att = norm_rec.shape[0], norm_att.shape[0]
    d_rnn = state_h.shape[-1]
    d_pool = state_pool.shape[-1]
    n_heads = cache_k.shape[3]
    mp = bp * t
    assert ts >= POOL_BUF and ts % SUBLANES == 0 and t % ts == 0 and cache_k.shape[4] == HEAD_DIM
    assert gate_r_w.shape[2] == LANES and d_pool // len(POOL_WINDOWS) == 2 * LANES

    x = (x_prompt.reshape(mp, d), x_sample.reshape(bs * ts, d))
    sc_pad = jnp.pad(state_conv, ((0, 0), (0, 0), (SUBLANES - (CONV_W - 1), 0), (0, 0)))
    sp_pad = jnp.pad(state_pool, ((0, 0), (0, 0), (POOL_TAIL - POOL_BUF, 0), (0, 0)))
    w_in_rec, w_out_rec, w_in_att, w_out_att = (w.astype(BF16) for w in (w_in_rec, w_out_rec, w_in_att, w_out_att))

    outs = {k: [] for k in ("hp", "cp", "pp", "hs", "cs", "ps")}
    new_kv = None
    for layer in range(n_rec + n_att):
        j = layer // 2
        if layer % 2 == 0:
            proj = (_norm_matmul_rows(*x, norm_rec[j], w_in_rec, j) if isinstance(x, tuple)
                    else _norm_matmul(x, norm_rec[j], w_in_rec, j))
            ya, hp, hs = _rglru(proj, state_h[j], sc_pad[j], conv_w[j], conv_b[j], gate_r_w[j], gate_r_b[j],
                                gate_i_w[j], gate_i_b[j], rg_lambda[j], bp=bp, t=t, bs=bs, ts=ts, d_rnn=d_rnn)
            yb = _pool(proj, sp_pad[j], pool_w[j], pool_scale[j], bp=bp, t=t, bs=bs, ts=ts,
                       d_rnn=d_rnn, d_pool=d_pool)
            x = _matmul_residual([ya, yb], w_out_rec, j, x)
            frames = proj.reshape(-1, ts, proj.shape[1])

            def last_rows(n, c0, c1):
                prompt = lax.slice(frames, (t // ts - 1, ts - n, c0), (mp // ts, ts, c1), (t // ts, 1, 1))
                sample = lax.slice(frames, (mp // ts, ts - n, c0), (frames.shape[0], ts, c1))
                return prompt, sample

            cp, cs = last_rows(CONV_W - 1, 0, d_rnn)
            pp, ps = last_rows(POOL_BUF, 2 * d_rnn, 2 * d_rnn + d_pool)
            for key, val in (("hp", hp), ("hs", hs), ("cp", cp), ("cs", cs), ("pp", pp), ("ps", ps)):
                outs[key].append(val)
        else:
            qkv, gate, new_kv = _norm_matmul_att(x, norm_att[j], w_in_att, new_kv, layer=j, mp=mp, n_heads=n_heads)
            op = _attn_prompt(qkv, gate, bp=bp, t=t, n_heads=n_heads)
            os_ = _attn_sample(qkv, gate, cache_k, cache_v, layer=j, mp=mp, bs=bs, ts=ts, n_heads=n_heads)
            if layer + 1 < n_rec + n_att:
                x = _matmul_residual_rows(op, os_, w_out_att, j, x)
            else:
                y_prompt, y_sample = _matmul_residual_rows_norm(op, os_, w_out_att, j, x, norm_final)

    if (n_rec + n_att) % 2:
        y_prompt, y_sample = _final_norm(x, norm_final, mp)
    st = {k: jnp.stack(v) for k, v in outs.items()}
    kp, vp, ks, vs = new_kv
    prompt_shape = (n_att, bp, t, n_heads, HEAD_DIM)
    sample_shape = (n_att, bs, ts, n_heads, HEAD_DIM)
    return (y_prompt.reshape(bp, t, d), y_sample.reshape(bs, ts, d),
            kp.reshape(prompt_shape), vp.reshape(prompt_shape), st["hp"], st["cp"], st["pp"],
            ks.reshape(sample_shape), vs.reshape(sample_shape), st["hs"], st["cs"], st["ps"])


def kernel(x_prompt, x_sample, cache_k, cache_v, state_h, state_conv, state_pool, norm_rec, w_in_rec, conv_w, conv_b, gate_r_w, gate_r_b, gate_i_w, gate_i_b, rg_lambda, pool_w, pool_scale, w_out_rec, norm_att, w_in_att, w_out_att, norm_final):
    return _step(x_prompt, x_sample, cache_k, cache_v, state_h, state_conv, state_pool, norm_rec, w_in_rec,
                 conv_w, conv_b, gate_r_w, gate_r_b, gate_i_w, gate_i_b, rg_lambda, pool_w, pool_scale,
                 w_out_rec, norm_att, w_in_att, w_out_att, norm_final)
```

```python
import functools

import jax
import jax.numpy as jnp
from jax import lax
from jax.experimental import pallas as pl
from jax.experimental.pallas import tpu as pltpu

F32 = jnp.float32
BF16 = jnp.bfloat16

EPS = 1e-6
RG_C = 8.0
CONV_W = 4
POOL_WINDOWS = (2, 4, 8, 16)
POOL_BUF = max(POOL_WINDOWS) - 1
HEAD_DIM = 128
ATT_SCALE = HEAD_DIM ** -0.5
EXP_UNDERFLOW = -105.0

LANES = 128
SUBLANES = 8
VMEM_LIMIT_CAP = 60000 * 1024
VMEM_SLACK = 8 * 1024 * 1024


def _params(semantics, buffer_bytes):
    limit = min(VMEM_LIMIT_CAP, buffer_bytes + VMEM_SLACK)
    return pltpu.CompilerParams(dimension_semantics=semantics, vmem_limit_bytes=limit)


def _tile(n, pref):
    t = min(n, pref)
    while n % t:
        t //= 2
    return t


def _silu(x):
    half = 0.5 * x
    return half + half * jnp.tanh(half)


LOG2_E = 1.4426950408889634


def _softplus(x):
    return jnp.maximum(x, 0.0) + jnp.log(1.0 + jnp.exp2(jnp.abs(x) * -LOG2_E))


NORM_ROWS = 256


def _normalize_rows(x_ref, g_ref, xn_ref):
    @pl.when(pl.program_id(1) == 0)
    def _():
        g = g_ref[...]

        def body(c, _):
            rows = pl.ds(pl.multiple_of(c * NORM_ROWS, NORM_ROWS), NORM_ROWS)
            x = x_ref[rows, :]
            ms = jnp.mean(x * x, axis=-1, keepdims=True)
            xn_ref[rows, :] = (x * lax.rsqrt(ms + EPS) * g).astype(BF16)
            return 0

        lax.fori_loop(0, x_ref.shape[0] // NORM_ROWS, body, 0)


def _norm_mm_kernel(x_ref, g_ref, w_ref, o_ref, xn_ref):
    _normalize_rows(x_ref, g_ref, xn_ref)
    o_ref[...] = jnp.dot(xn_ref[...], w_ref[...], preferred_element_type=F32)


def _norm_mm_rows_kernel(top_ref, bottom_ref, g_ref, w_ref, o_ref, xn_ref, *, n_top):
    i = pl.program_id(0)
    pl.when(i < n_top)(functools.partial(_normalize_rows, top_ref, g_ref, xn_ref))
    pl.when(i >= n_top)(functools.partial(_normalize_rows, bottom_ref, g_ref, xn_ref))
    o_ref[...] = jnp.dot(xn_ref[...], w_ref[...], preferred_element_type=F32)


def _norm_matmul(x, g, w_all, layer, *, tm_pref=1024, tn_pref=1024):
    m, k = x.shape
    n = w_all.shape[2]
    tm, tn = _tile(m, tm_pref), _tile(n, tn_pref)
    vmem = 2 * tm * k * 4 + tm * k * 2 + 2 * k * tn * 2 + 2 * tm * tn * 4
    return pl.pallas_call(
        _norm_mm_kernel,
        grid=(m // tm, n // tn),
        in_specs=[pl.BlockSpec((tm, k), lambda i, j: (i, 0)),
                  pl.BlockSpec((1, k), lambda i, j: (0, 0)),
                  pl.BlockSpec((None, k, tn), lambda i, j: (layer, 0, j))],
        out_specs=pl.BlockSpec((tm, tn), lambda i, j: (i, j)),
        out_shape=jax.ShapeDtypeStruct((m, n), F32),
        scratch_shapes=[pltpu.VMEM((tm, k), BF16)],
        compiler_params=_params(("arbitrary", "arbitrary"), vmem),
        name="norm_matmul",
    )(x, g.reshape(1, k), w_all)


def _norm_matmul_rows(top, bottom, g, w_all, layer, *, tm_pref=1024, tn_pref=1024):
    k = top.shape[1]
    m, n = top.shape[0] + bottom.shape[0], w_all.shape[2]
    tm = _tile(top.shape[0], tm_pref)
    while bottom.shape[0] % tm:
        tm //= 2
    tn = _tile(n, tn_pref)
    n_top = top.shape[0] // tm
    vmem = 3 * tm * k * 4 + tm * k * 2 + 2 * k * tn * 2 + 2 * tm * tn * 4
    return pl.pallas_call(
        functools.partial(_norm_mm_rows_kernel, n_top=n_top),
        grid=(m // tm, n // tn),
        in_specs=[pl.BlockSpec((tm, k), lambda i, j: (jnp.minimum(i, n_top - 1), 0)),
                  pl.BlockSpec((tm, k), lambda i, j: (jnp.maximum(i - n_top, 0), 0),
                               pipeline_mode=pl.Buffered(1)),
                  pl.BlockSpec((1, k), lambda i, j: (0, 0)),
                  pl.BlockSpec((None, k, tn), lambda i, j: (layer, 0, j))],
        out_specs=pl.BlockSpec((tm, tn), lambda i, j: (i, j)),
        out_shape=jax.ShapeDtypeStruct((m, n), F32),
        scratch_shapes=[pltpu.VMEM((tm, k), BF16)],
        compiler_params=_params(("arbitrary", "arbitrary"), vmem),
        name="norm_matmul_rows",
    )(top, bottom, g.reshape(1, k), w_all)


def _norm_mm_att_kernel(x_ref, g_ref, w_ref, *refs, layer, n_top, per):
    qkv_ref, gt_ref, kp_hbm, vp_hbm, ks_hbm, vs_hbm, xn_ref, kv_scr, sem = refs[-9:]
    layers = (layer,) if len(refs) > 9 else range(kp_hbm.shape[0])
    _normalize_rows(x_ref, g_ref, xn_ref)
    i, j = pl.program_id(0), pl.program_id(1)
    tm, tn = qkv_ref.shape
    hpt = tn // HEAD_DIM
    top = i < n_top

    def project():
        return jnp.dot(xn_ref[...], w_ref[...], preferred_element_type=F32)

    def copies(jj, prompt_rows):
        slot = (jj - per) % 2
        dst = ((kp_hbm, ks_hbm), (vp_hbm, vs_hbm))[jj // per - 1][0 if prompt_rows else 1]
        rows = pl.ds(i * tm if prompt_rows else (i - n_top) * tm, tm)
        return [pltpu.make_async_copy(kv_scr.at[slot, :, pl.ds(h * HEAD_DIM, HEAD_DIM)],
                                      dst.at[l, rows, (jj % per) * hpt + h, :], sem.at[slot])
                for l in layers for h in range(hpt)]

    def for_rows(jj, action):
        @pl.when(top)
        def _():
            for c in copies(jj, True):
                action(c)

        @pl.when(jnp.logical_not(top))
        def _():
            for c in copies(jj, False):
                action(c)

    @pl.when(j < per)
    def _():
        qkv_ref[...] = (project() * ATT_SCALE).astype(BF16)

    for jj in range(per, 3 * per + 2):
        @pl.when(j == jj)
        def _(jj=jj):
            if jj - 2 >= per:
                for_rows(jj - 2, lambda c: c.wait())
            if jj < 3 * per:
                slot = (jj - per) % 2
                kv_scr[slot] = project()
                qkv_ref[...] = kv_scr[slot].astype(BF16)
                for_rows(jj, lambda c: c.start())

    @pl.when(j >= 3 * per)
    def _():
        gt_ref[...] = project()


def _norm_matmul_att(x, g, w_all, new_kv, *, layer, mp, n_heads, tm_pref=1024):
    m, k = x.shape
    n_layers = w_all.shape[0]
    d_att = n_heads * HEAD_DIM
    tn = d_att // 2
    per = d_att // tn
    assert per >= 2 and w_all.shape[2] == 4 * d_att
    tm = _tile(mp, tm_pref)
    while (m - mp) % tm:
        tm //= 2
    any_spec = pl.BlockSpec(memory_space=pl.ANY)
    carried = () if new_kv is None else tuple(new_kv)
    kv_shapes = [jax.ShapeDtypeStruct((n_layers, rows, n_heads, HEAD_DIM), F32) for rows in (mp, mp, m - mp, m - mp)]
    vmem = 2 * tm * k * 4 + tm * k * 2 + 2 * k * tn * 2 + 2 * tm * tn * (2 + 4) + 3 * tm * tn * 4
    outs = pl.pallas_call(
        functools.partial(_norm_mm_att_kernel, layer=layer, n_top=mp // tm, per=per),
        grid=(m // tm, 4 * per),
        in_specs=[pl.BlockSpec((tm, k), lambda i, j: (i, 0)),
                  pl.BlockSpec((1, k), lambda i, j: (0, 0)),
                  pl.BlockSpec((None, k, tn), lambda i, j: (layer, 0, j))] + [any_spec] * len(carried),
        out_specs=[pl.BlockSpec((tm, tn), lambda i, j: (i, jnp.minimum(j, 3 * per - 1))),
                   pl.BlockSpec((tm, tn), lambda i, j: (i, jnp.maximum(j - 3 * per, 0))),
                   any_spec, any_spec, any_spec, any_spec],
        out_shape=[jax.ShapeDtypeStruct((m, 3 * d_att), BF16), jax.ShapeDtypeStruct((m, d_att), F32)] + kv_shapes,
        input_output_aliases={3 + n: 2 + n for n in range(len(carried))},
        scratch_shapes=[pltpu.VMEM((tm, k), BF16), pltpu.VMEM((2, tm, tn), F32),
                        pltpu.SemaphoreType.DMA((2,))],
        compiler_params=_params(("arbitrary", "arbitrary"), vmem),
        name="norm_matmul_att",
    )(x, g.reshape(1, k), w_all, *carried)
    return outs[0], outs[1], tuple(outs[2:])


def _mm_res_kernel(*refs, n_pairs, n_top):
    lhs, ws = refs[:n_pairs], refs[n_pairs:2 * n_pairs]
    res_refs, out_ref = refs[2 * n_pairs:-1], refs[-1]

    def emit(res_ref):
        acc = res_ref[...]
        for l, w in zip(lhs, ws):
            acc = acc + jnp.dot(l[...], w[...], preferred_element_type=F32)
        out_ref[...] = acc

    if len(res_refs) == 1:
        emit(res_refs[0])
    else:
        i = pl.program_id(0)
        pl.when(i < n_top)(functools.partial(emit, res_refs[0]))
        pl.when(i >= n_top)(functools.partial(emit, res_refs[1]))


def _matmul_residual(lhs_list, w_all, layer, res, *, tm_pref=1024, tn_pref=1024):
    parts = res if isinstance(res, tuple) else (res,)
    m, n = sum(p.shape[0] for p in parts), parts[0].shape[1]
    tm = _tile(parts[0].shape[0], tm_pref)
    while any(p.shape[0] % tm for p in parts):
        tm //= 2
    tn = _tile(n, tn_pref)
    n_top = parts[0].shape[0] // tm
    ks = [l.shape[1] for l in lhs_list]
    starts = [sum(ks[:p]) for p in range(len(ks))]
    assert all(r % k == 0 for r, k in zip(starts, ks)) and sum(ks) == w_all.shape[1]
    vmem = sum(2 * tm * k * 2 + 2 * k * tn * 2 for k in ks) + (2 + 2 * len(parts)) * tm * tn * 4
    res_specs = ([pl.BlockSpec((tm, tn), lambda i, j: (i, j))] if len(parts) == 1 else
                 [pl.BlockSpec((tm, tn), lambda i, j: (jnp.minimum(i, n_top - 1), j)),
                  pl.BlockSpec((tm, tn), lambda i, j: (jnp.maximum(i - n_top, 0), j))])
    in_specs = ([pl.BlockSpec((tm, k), lambda i, j: (i, 0)) for k in ks]
                + [pl.BlockSpec((None, k, tn), lambda i, j, blk=r // k: (layer, blk, j)) for r, k in zip(starts, ks)]
                + res_specs)
    return pl.pallas_call(
        functools.partial(_mm_res_kernel, n_pairs=len(ks), n_top=n_top),
        grid=(m // tm, n // tn),
        in_specs=in_specs,
        out_specs=pl.BlockSpec((tm, tn), lambda i, j: (i, j)),
        out_shape=jax.ShapeDtypeStruct((m, n), F32),
        compiler_params=_params(("arbitrary", "arbitrary"), vmem),
        name="matmul_residual",
    )(*lhs_list, *([w_all] * len(ks)), *parts)


def _mm_res_rows_kernel(top_ref, bottom_ref, w_ref, res_ref, out_ref, *, n_top):
    i = pl.program_id(0)

    def emit(lhs_ref):
        out_ref[...] = res_ref[...] + jnp.dot(lhs_ref[...], w_ref[...], preferred_element_type=F32)

    pl.when(i < n_top)(functools.partial(emit, top_ref))
    pl.when(i >= n_top)(functools.partial(emit, bottom_ref))


def _mm_res_rows_norm_kernel(top_ref, bottom_ref, w_ref, res_ref, g_ref, top_out, bottom_out, *, n_top):
    i = pl.program_id(0)

    def emit(lhs_ref, out_ref):
        y = res_ref[...] + jnp.dot(lhs_ref[...], w_ref[...], preferred_element_type=F32)
        ms = jnp.mean(y * y, axis=-1, keepdims=True)
        out_ref[...] = y * lax.rsqrt(ms + EPS) * g_ref[...]

    pl.when(i < n_top)(functools.partial(emit, top_ref, top_out))
    pl.when(i >= n_top)(functools.partial(emit, bottom_ref, bottom_out))


def _matmul_residual_rows_norm(top, bottom, w_all, layer, res, g, *, tm_pref=512):
    m, n = res.shape
    k = w_all.shape[1]
    tm = _tile(top.shape[0], tm_pref)
    while bottom.shape[0] % tm:
        tm //= 2
    n_top = top.shape[0] // tm
    vmem = 4 * tm * k * 2 + 2 * k * n * 2 + 6 * tm * n * 4
    return pl.pallas_call(
        functools.partial(_mm_res_rows_norm_kernel, n_top=n_top),
        grid=(m // tm,),
        in_specs=[pl.BlockSpec((tm, k), lambda i: (jnp.minimum(i, n_top - 1), 0)),
                  pl.BlockSpec((tm, k), lambda i: (jnp.maximum(i - n_top, 0), 0)),
                  pl.BlockSpec((None, k, n), lambda i: (layer, 0, 0)),
                  pl.BlockSpec((tm, n), lambda i: (i, 0)),
                  pl.BlockSpec((1, n), lambda i: (0, 0))],
        out_specs=[pl.BlockSpec((tm, n), lambda i: (jnp.minimum(i, n_top - 1), 0)),
                   pl.BlockSpec((tm, n), lambda i: (jnp.maximum(i - n_top, 0), 0))],
        out_shape=[jax.ShapeDtypeStruct((top.shape[0], n), F32), jax.ShapeDtypeStruct((bottom.shape[0], n), F32)],
        compiler_params=_params(("arbitrary",), vmem),
        name="matmul_residual_rows_norm",
    )(top, bottom, w_all, res, g.reshape(1, n))


def _matmul_residual_rows(top, bottom, w_all, layer, res, *, tm_pref=1024, tn_pref=1024):
    m, n = res.shape
    k = w_all.shape[1]
    tm = _tile(top.shape[0], tm_pref)
    while bottom.shape[0] % tm:
        tm //= 2
    tn = _tile(n, tn_pref)
    n_top = top.shape[0] // tm
    vmem = 4 * tm * k * 2 + 2 * k * tn * 2 + 4 * tm * tn * 4
    return pl.pallas_call(
        functools.partial(_mm_res_rows_kernel, n_top=n_top),
        grid=(m // tm, n // tn),
        in_specs=[pl.BlockSpec((tm, k), lambda i, j: (jnp.minimum(i, n_top - 1), 0)),
                  pl.BlockSpec((tm, k), lambda i, j: (jnp.maximum(i - n_top, 0), 0)),
                  pl.BlockSpec((None, k, tn), lambda i, j: (layer, 0, j)),
                  pl.BlockSpec((tm, tn), lambda i, j: (i, j))],
        out_specs=pl.BlockSpec((tm, tn), lambda i, j: (i, j)),
        out_shape=jax.ShapeDtypeStruct((m, n), F32),
        compiler_params=_params(("arbitrary", "arbitrary"), vmem),
        name="matmul_residual_rows",
    )(top, bottom, w_all, res)


def _final_norm_kernel(x_ref, g_ref, top_ref, bottom_ref, *, n_top):
    x = x_ref[...]
    ms = jnp.mean(x * x, axis=-1, keepdims=True)
    y = x * lax.rsqrt(ms + EPS) * g_ref[...]
    i = pl.program_id(0)

    @pl.when(i < n_top)
    def _():
        top_ref[...] = y

    @pl.when(i >= n_top)
    def _():
        bottom_ref[...] = y


def _final_norm(x, g, mp):
    m, d = x.shape
    tm = _tile(mp, 256)
    while (m - mp) % tm:
        tm //= 2
    n_top = mp // tm
    return pl.pallas_call(
        functools.partial(_final_norm_kernel, n_top=n_top),
        grid=(m // tm,),
        in_specs=[pl.BlockSpec((tm, d), lambda i: (i, 0)), pl.BlockSpec((1, d), lambda i: (0, 0))],
        out_specs=[pl.BlockSpec((tm, d), lambda i: (jnp.minimum(i, n_top - 1), 0)),
                   pl.BlockSpec((tm, d), lambda i: (jnp.maximum(i - n_top, 0), 0))],
        out_shape=[jax.ShapeDtypeStruct((mp, d), F32), jax.ShapeDtypeStruct((m - mp, d), F32)],
        compiler_params=_params(("arbitrary",), 6 * tm * d * 4),
        name="final_norm",
    )(x, g.reshape(1, d))


SCAN_ROWS = 256
SCAN_STREAMS = 8


def _scan_tiles(a, u, carry):
    r = a.shape[0]
    nt = r // SUBLANES
    a3 = a.reshape(nt, SUBLANES, LANES)
    u3 = u.reshape(nt, SUBLANES, LANES)
    sub = lax.broadcasted_iota(jnp.int32, a3.shape, 1)
    for d in (1, 2, 4):
        keep = sub >= d
        a_prev = pltpu.roll(a3, d, 1)
        u_prev = pltpu.roll(u3, d, 1)
        u3 = jnp.where(keep, a3 * u_prev + u3, u3)
        a3 = jnp.where(keep, a3 * a_prev, a3)
    hs = []
    for t in range(nt):
        h_t = u3[t] + a3[t] * carry
        carry = h_t[SUBLANES - 1:SUBLANES, :]
        hs.append(h_t)
    return jnp.concatenate(hs, axis=0), carry


def _rglru_kernel(xa_ref, ga_ref, h0_ref, sc_ref, cw_ref, cb_ref, wr_ref, br_ref, wi_ref, bi_ref, lam_ref,
                  out_ref, hp_ref, hs_ref, *, bp, t, bs, ts):
    mp = bp * t
    cw = cw_ref[...]
    cb = cb_ref[...]
    half_wr = (0.5 * wr_ref[...]).astype(BF16)
    half_wi = (0.5 * wi_ref[...]).astype(BF16)
    half_br = 0.5 * br_ref[...]
    half_bi = 0.5 * bi_ref[...]
    half_c = (-0.5 * RG_C) * _softplus(-lam_ref[...])

    def conv_taps(tap):
        y = cb + tap(0) * cw[0:1]
        for k in range(1, CONV_W):
            y = y + tap(k) * cw[k:k + 1]
        return y

    def conv(ext, n):
        def tap(k):
            back = CONV_W - 1 - k
            return (ext if back == 0 else pltpu.roll(ext, back, 0))[SUBLANES:]

        return conv_taps(tap)

    def decay_and_input(xc):
        xb = xc.astype(BF16)
        tr = jnp.tanh(jnp.dot(xb, half_wr, preferred_element_type=F32) + half_br)
        ti = jnp.tanh(jnp.dot(xb, half_wi, preferred_element_type=F32) + half_bi)
        log_a = half_c + half_c * tr
        half_x = 0.5 * xc
        gated_x = half_x + half_x * ti
        a = jnp.exp(log_a)
        v = -jnp.tanh(log_a) * (1.0 + a * a)
        root = jnp.where(v > 0.0, v * lax.rsqrt(v), 0.0)
        return a, root * gated_x

    def emit(rows, a, u, carry):
        h, carry = _scan_tiles(a, u, carry)
        out_ref[rows, :] = (h * _silu(ga_ref[rows, :])).astype(out_ref.dtype)
        return carry

    rp = _tile(t, SCAN_ROWS)

    def first_chunk(b):
        rows = pl.ds(b * t, rp)
        ext = jnp.concatenate([jnp.zeros((SUBLANES, LANES), F32), xa_ref[rows, :]], axis=0)
        a, u = decay_and_input(conv(ext, rp))
        return emit(rows, a, u, jnp.zeros((1, LANES), F32))

    def later_chunk(c, carries):
        new = []
        for b in range(bp):
            r0 = b * t + c * rp
            rows = pl.ds(pl.multiple_of(r0, SUBLANES), rp)
            xc = conv_taps(lambda k: xa_ref[pl.ds(r0 - (CONV_W - 1 - k), rp), :])
            a, u = decay_and_input(xc)
            new.append(emit(rows, a, u, carries[b]))
        return tuple(new)

    carries = lax.fori_loop(1, t // rp, later_chunk, tuple(first_chunk(b) for b in range(bp)))
    for b in range(bp):
        hp_ref[b:b + 1, :] = carries[b]

    ns = _tile(bs, SCAN_STREAMS)

    def sample_group(c, _):
        rows = [pl.ds(pl.multiple_of(mp + (c * ns + k) * ts, SUBLANES), ts) for k in range(ns)]
        xc = [conv(jnp.concatenate([sc_ref[c * ns + k], xa_ref[rows[k], :]], axis=0), ts) for k in range(ns)]
        a, u = decay_and_input(jnp.concatenate(xc, axis=0))
        for k in range(ns):
            seg = slice(k * ts, (k + 1) * ts)
            s = c * ns + k
            hs_ref[pl.ds(s, 1), :] = emit(rows[k], a[seg], u[seg], h0_ref[pl.ds(s, 1), :])
        return 0

    lax.fori_loop(0, bs // ns, sample_group, 0)


def _rglru(proj, h0, sc_pad, cw, cb, wr, br, wi, bi, lam, *, bp, t, bs, ts, d_rnn):
    m = proj.shape[0]
    nb = d_rnn // LANES
    row = lambda v: v.reshape(1, d_rnn)
    vec_spec = pl.BlockSpec((1, LANES), lambda n: (0, n))
    w_spec = pl.BlockSpec((None, LANES, LANES), lambda n: (n, 0, 0))
    vmem = 2 * (2 * m * LANES * 4 + m * LANES * 2)
    return pl.pallas_call(
        functools.partial(_rglru_kernel, bp=bp, t=t, bs=bs, ts=ts),
        grid=(nb,),
        in_specs=[pl.BlockSpec((m, LANES), lambda n: (0, n)),
                  pl.BlockSpec((m, LANES), lambda n: (0, nb + n)),
                  pl.BlockSpec((bs, LANES), lambda n: (0, n)),
                  pl.BlockSpec((bs, SUBLANES, LANES), lambda n: (0, 0, n)),
                  pl.BlockSpec((CONV_W, LANES), lambda n: (0, n)),
                  vec_spec, w_spec, vec_spec, w_spec, vec_spec, vec_spec],
        out_specs=[pl.BlockSpec((m, LANES), lambda n: (0, n)),
                   pl.BlockSpec((bp, LANES), lambda n: (0, n)),
                   pl.BlockSpec((bs, LANES), lambda n: (0, n))],
        out_shape=[jax.ShapeDtypeStruct((m, d_rnn), BF16),
                   jax.ShapeDtypeStruct((bp, d_rnn), F32),
                   jax.ShapeDtypeStruct((bs, d_rnn), F32)],
        compiler_params=_params(("arbitrary",), vmem),
        name="rglru_mixer",
    )(proj, proj, h0, sc_pad, cw, row(cb), wr, row(br), wi, row(bi), row(lam))


POOL_TAIL = 16
POOL_ROWS = 256


def _pool_kernel(xb_ref, gb_ref, sp_ref, pw_ref, ps_ref, out_ref, m_scr, tail_scr, *, npb, bpt, ts):
    g = pl.program_id(0)
    rb = pl.program_id(1)
    rbk, gw = xb_ref.shape
    wf = lax.shift_left(jnp.int32(2), g).astype(F32)

    def window_means(ext, n, pos0):
        s2 = ext[1:] + ext[:-1]
        s4 = s2[2:] + s2[:-2]
        s8 = s4[4:] + s4[:-4]
        s16 = s8[8:] + s8[:-8]
        x = ext[POOL_TAIL:]
        win = jnp.where(g == 0, s2[POOL_TAIL - 1:],
                        jnp.where(g == 1, s4[POOL_TAIL - 3:],
                                  jnp.where(g == 2, s8[POOL_TAIL - 7:], s16[POOL_TAIL - 15:])))
        if pos0 is None:
            cnt = wf
        else:
            pos = pos0 + lax.broadcasted_iota(jnp.int32, (n, gw), 0)
            cnt = jnp.minimum(wf, (pos + 1).astype(F32))
        return win / cnt - x

    @pl.when(rb < npb)
    def _():
        blk = rb % bpt

        @pl.when(blk == 0)
        def _():
            tail_scr[...] = jnp.zeros_like(tail_scr)

        rc = _tile(rbk, POOL_ROWS)

        def body(c, tail):
            rows = pl.ds(pl.multiple_of(c * rc, SUBLANES), rc)
            x = xb_ref[rows, :]
            ext = jnp.concatenate([tail, x], axis=0)
            m_scr[rows, :] = window_means(ext, rc, blk * rbk + c * rc).astype(BF16)
            return x[rc - POOL_TAIL:, :]

        tail_scr[...] = lax.fori_loop(0, rbk // rc, body, tail_scr[...])

    @pl.when(rb >= npb)
    def _():
        s0 = (rb - npb) * (rbk // ts)

        def body(s, _):
            rows = pl.ds(pl.multiple_of(s * ts, SUBLANES), ts)
            ext = jnp.concatenate([sp_ref[s0 + s], xb_ref[rows, :]], axis=0)
            m_scr[rows, :] = window_means(ext, ts, None).astype(BF16)
            return 0

        lax.fori_loop(0, rbk // ts, body, 0)

    y = jnp.dot(m_scr[...], pw_ref[...].astype(BF16), preferred_element_type=F32) * ps_ref[...]
    out_ref[...] = (y * _silu(gb_ref[...])).astype(out_ref.dtype)


def _pool(proj, sp_pad, pw, ps, *, bp, t, bs, ts, d_rnn, d_pool):
    m = proj.shape[0]
    ng = len(POOL_WINDOWS)
    gw = d_pool // ng
    ms = bs * ts
    rbk = min(1024, t, ms)
    while t % rbk or ms % rbk:
        rbk //= 2
    npb = bp * t // rbk
    xcol = 2 * d_rnn // gw
    gcol = (2 * d_rnn + d_pool) // gw
    vmem = 2 * (2 * rbk * gw * 4 + rbk * gw * 2) + rbk * gw * 2 + 2 * bs * POOL_TAIL * gw * 4 + 2 * gw * gw * 4
    return pl.pallas_call(
        functools.partial(_pool_kernel, npb=npb, bpt=t // rbk, ts=ts),
        grid=(ng, m // rbk),
        in_specs=[pl.BlockSpec((rbk, gw), lambda g, r: (r, xcol + g)),
                  pl.BlockSpec((rbk, gw), lambda g, r: (r, gcol + g)),
                  pl.BlockSpec((bs, POOL_TAIL, gw), lambda g, r: (0, 0, g)),
                  pl.BlockSpec((None, gw, gw), lambda g, r: (g, 0, 0)),
                  pl.BlockSpec((1, gw), lambda g, r: (0, g))],
        out_specs=pl.BlockSpec((rbk, gw), lambda g, r: (r, g)),
        out_shape=jax.ShapeDtypeStruct((m, d_pool), BF16),
        scratch_shapes=[pltpu.VMEM((rbk, gw), BF16), pltpu.VMEM((POOL_TAIL, gw), F32)],
        compiler_params=_params(("arbitrary", "arbitrary"), vmem),
        name="pool_mixer",
    )(proj, proj, sp_pad, pw, ps.reshape(1, d_pool))


def _suffix_matrix(n):
    j = lax.broadcasted_iota(jnp.int32, (n, n), 0)
    s = lax.broadcasted_iota(jnp.int32, (n, n), 1)
    u = jnp.where(j > s, -1.0, 0.0).astype(BF16)
    return jnp.concatenate([u, u], axis=0)


def _qk(q, k):
    return lax.dot_general(q, k, (((1,), (1,)), ((), ())), preferred_element_type=F32)


def _strictly_earlier(shape):
    return lax.broadcasted_iota(jnp.int32, shape, 1) < lax.broadcasted_iota(jnp.int32, shape, 0)


def _sb_weights(z, uu, carry=None, mask=None):
    sp = _softplus(z)
    spm = sp if mask is None else jnp.where(mask, sp, 0.0)
    hi = spm.astype(BF16)
    lo = (spm - hi.astype(F32)).astype(BF16)
    e = z - sp + jnp.dot(jnp.concatenate([hi, lo], axis=1), uu, preferred_element_type=F32)
    w = jnp.exp(e if carry is None else e + carry)
    if mask is not None:
        w = jnp.where(mask, w, 0.0)
    return w.astype(BF16), -jnp.sum(spm, axis=1, keepdims=True)


def _sb_tile(q, k, v, uu, carry=None, mask=None):
    w, tot = _sb_weights(_qk(q, k), uu, carry, mask)
    return jnp.dot(w, v, preferred_element_type=F32), tot


PROMPT_TILE = 256
PROMPT_ALWAYS = 2


def _attn_prompt_kernel(q_ref, k_ref, v_ref, gt_ref, uu_ref, o_ref, acc_scr, car_scr, worst_ref, *, tq):
    uu = uu_ref[...]
    mask = _strictly_earlier((tq, tq))
    nq = q_ref.shape[0] // tq

    groups = [range(kj, min(kj + PROMPT_ALWAYS, nq)) for kj in range(nq)]
    log_beta, spm = {}, {}
    for kj, qs in enumerate(groups):
        zb = _qk(q_ref[pl.ds(kj * tq, len(qs) * tq), :], k_ref[pl.ds(kj * tq, tq), :])
        for n, qi in enumerate(qs):
            zt = zb[n * tq:(n + 1) * tq]
            st = _softplus(zt)
            log_beta[qi, kj] = zt - st
            spm[qi, kj] = jnp.where(mask, st, 0.0) if qi == kj else st
    order = list(spm)
    stacked = jnp.concatenate([spm[tile] for tile in order], axis=0)
    hi = stacked.astype(BF16)
    lo = (stacked - hi.astype(F32)).astype(BF16)
    suffix = jnp.dot(jnp.concatenate([hi, lo], axis=1), uu, preferred_element_type=F32)
    suffix = {tile: suffix[n * tq:(n + 1) * tq] for n, tile in enumerate(order)}
    total = {tile: -jnp.sum(spm[tile], axis=1, keepdims=True) for tile in order}
    acc = {}
    for kj, qs in enumerate(groups):
        ws = []
        for qi in qs:
            e = log_beta[qi, kj] + suffix[qi, kj]
            for newer in range(qi, kj, -1):
                e = e + total[qi, newer]
            w = jnp.exp(e)
            ws.append((jnp.where(mask, w, 0.0) if qi == kj else w).astype(BF16))
        pv = jnp.dot(jnp.concatenate(ws, axis=0), v_ref[pl.ds(kj * tq, tq), :], preferred_element_type=F32)
        for n, qi in enumerate(qs):
            part = pv[n * tq:(n + 1) * tq]
            acc[qi] = acc[qi] + part if qi in acc else part
    for qi in range(nq):
        rows = pl.ds(qi * tq, tq)
        o_ref[rows, :] = (acc[qi] * _silu(gt_ref[rows, :])).astype(o_ref.dtype)
        if qi >= PROMPT_ALWAYS:
            carry = total[qi, qi]
            for kj in range(qi - 1, qi - PROMPT_ALWAYS, -1):
                carry = carry + total[qi, kj]
            acc_scr[qi] = acc[qi]
            car_scr[qi] = carry
            worst_ref[qi] = jnp.max(carry)

    def finish(qi, _):
        @pl.when(worst_ref[qi] >= EXP_UNDERFLOW)
        def _():
            rows = pl.ds(pl.multiple_of(qi * tq, tq), tq)
            q = q_ref[rows, :]

            def more(s):
                return jnp.logical_and(s[0] >= 0, jnp.max(s[2]) >= EXP_UNDERFLOW)

            def older(s):
                j, acc, carry = s
                old = pl.ds(pl.multiple_of(j * tq, tq), tq)
                pv, tot = _sb_tile(q, k_ref[old, :], v_ref[old, :], uu, carry)
                return j - 1, acc + pv, carry + tot

            _, acc, _ = lax.while_loop(more, older, (qi - PROMPT_ALWAYS, acc_scr[qi], car_scr[qi]))
            o_ref[rows, :] = (acc * _silu(gt_ref[rows, :])).astype(o_ref.dtype)

        return 0

    lax.fori_loop(PROMPT_ALWAYS, nq, finish, 0)


def _attn_prompt(qkv, gate, *, bp, t, n_heads):
    tq = _tile(t, PROMPT_TILE)
    d_att = n_heads * HEAD_DIM
    nq = t // tq
    state = pltpu.VMEM((nq, tq, LANES), F32), pltpu.VMEM((nq, tq, 1), F32), pltpu.SMEM((nq,), F32)
    always = PROMPT_ALWAYS * nq
    vmem = (2 * (3 * t * LANES * 2 + t * LANES * 4 + 2 * tq * tq * 2 + t * LANES * 2) + 2 * t * LANES * 4
            + 5 * always * tq * tq * 4)
    col = lambda c: pl.BlockSpec((t, LANES), lambda b, h: (b, c * n_heads + h))
    return pl.pallas_call(
        functools.partial(_attn_prompt_kernel, tq=tq),
        grid=(bp, n_heads),
        in_specs=[col(0), col(1), col(2), col(0), pl.BlockSpec((2 * tq, tq), lambda b, h: (0, 0))],
        out_specs=pl.BlockSpec((t, LANES), lambda b, h: (b, h)),
        out_shape=jax.ShapeDtypeStruct((bp * t, d_att), BF16),
        scratch_shapes=list(state),
        compiler_params=_params(("arbitrary", "arbitrary"), vmem),
        name="attn_prompt",
    )(qkv, qkv, qkv, gate, _suffix_matrix(tq))


def _attn_sample_kernel(q_ref, kn_ref, vn_ref, gt_ref, kc_hbm, vc_hbm, un_ref, uc_ref, o_ref,
                        kfirst, vfirst, kmore, vmore, acc_scr, car_scr, sem, *, layer, pc):
    b = pl.program_id(0)
    n_heads, ts = kmore.shape[0], q_ref.shape[0]
    newest = kc_hbm.shape[2] // pc - 1
    slot = b % 2

    def copies(stream, chunk, kdst, vdst, ksem, vsem):
        pos = pl.ds(chunk * pc, pc)
        out = []
        for h in range(n_heads):
            out.append(pltpu.make_async_copy(kc_hbm.at[layer, stream, pos, h, :], kdst.at[h], ksem))
            out.append(pltpu.make_async_copy(vc_hbm.at[layer, stream, pos, h, :], vdst.at[h], vsem))
        return out

    def first_copies(stream, s):
        return copies(stream, newest, kfirst.at[s], vfirst.at[s], sem.at[s, 0], sem.at[s, 1])

    @pl.when(b == 0)
    def _():
        for n, c in enumerate(first_copies(0, 0)):
            c.start(priority=n % 2)

    @pl.when(b + 1 < pl.num_programs(0))
    def _():
        for n, c in enumerate(first_copies(b + 1, 1 - slot)):
            c.start(priority=n % 2)

    def head(ref, h):
        return ref[:, h * HEAD_DIM:(h + 1) * HEAD_DIM]

    def scores(keys):
        return jnp.concatenate([_qk(head(q_ref, h), keys(h)) for h in range(n_heads)], axis=0)

    def weighted(w, values):
        return jnp.concatenate([jnp.dot(w[h * ts:(h + 1) * ts], values(h), preferred_element_type=F32)
                                for h in range(n_heads)], axis=0)

    pad = jnp.zeros((LANES - ts, HEAD_DIM), BF16)
    query = lax.broadcasted_iota(jnp.int32, (n_heads, ts, LANES), 1).reshape(n_heads * ts, LANES)
    mask = lax.broadcasted_iota(jnp.int32, (n_heads * ts, LANES), 1) < query
    w, carry = _sb_weights(scores(lambda h: jnp.concatenate([head(kn_ref, h), pad], axis=0)),
                           un_ref[...], None, mask)
    acc = weighted(w, lambda h: jnp.concatenate([head(vn_ref, h), pad], axis=0))

    def chunk(kbuf, vbuf, acc, carry):
        w, tot = _sb_weights(scores(lambda h: kbuf[h].astype(BF16)), uc_ref[...], carry)
        return acc + weighted(w, lambda h: vbuf[h].astype(BF16)), carry + tot

    for c in first_copies(b, slot):
        c.wait()
    acc_scr[...], car_scr[...] = chunk(kfirst.at[slot], vfirst.at[slot], acc, carry)

    def more(s):
        return jnp.logical_and(s[0] >= 0, s[1] >= EXP_UNDERFLOW)

    def older(s):
        cs = copies(b, s[0], kmore, vmore, sem.at[2, 0], sem.at[2, 1])
        for c in cs:
            c.start()
        for c in cs:
            c.wait()
        acc_scr[...], car_scr[...] = chunk(kmore, vmore, acc_scr[...], car_scr[...])
        return s[0] - 1, jnp.max(car_scr[...])

    lax.while_loop(more, older, (newest - 1, jnp.max(car_scr[...])))

    for h in range(n_heads):
        o_ref[:, h * HEAD_DIM:(h + 1) * HEAD_DIM] = (acc_scr[pl.ds(h * ts, ts), :]
                                                     * _silu(head(gt_ref, h))).astype(o_ref.dtype)


def _attn_sample(qkv, gate, cache_k, cache_v, *, layer, mp, bs, ts, n_heads):
    p = cache_k.shape[2]
    d_att = n_heads * HEAD_DIM
    pc = _tile(p, 256)
    r0 = mp // ts
    chunk_bytes = n_heads * pc * HEAD_DIM * 4
    vmem = (2 * (3 * ts * d_att * 2 + ts * d_att * 4 + ts * d_att * 2) + 6 * chunk_bytes
            + 2 * n_heads * ts * LANES * 4 + 2 * 2 * (LANES * LANES + pc * pc) * 2)
    new_spec = lambda c: pl.BlockSpec((ts, d_att), lambda b: (r0 + b, c))
    chunk_buf = lambda n: pltpu.VMEM(n + (n_heads, pc, HEAD_DIM), F32)
    return pl.pallas_call(
        functools.partial(_attn_sample_kernel, layer=layer, pc=pc),
        grid=(bs,),
        in_specs=[new_spec(0), new_spec(1), new_spec(2), new_spec(0),
                  pl.BlockSpec(memory_space=pl.ANY), pl.BlockSpec(memory_space=pl.ANY),
                  pl.BlockSpec((2 * LANES, LANES), lambda b: (0, 0)),
                  pl.BlockSpec((2 * pc, pc), lambda b: (0, 0))],
        out_specs=pl.BlockSpec((ts, d_att), lambda b: (b, 0)),
        out_shape=jax.ShapeDtypeStruct((bs * ts, d_att), BF16),
        scratch_shapes=[chunk_buf((2,)), chunk_buf((2,)), chunk_buf(()), chunk_buf(()),
                        pltpu.VMEM((n_heads * ts, LANES), F32), pltpu.VMEM((n_heads * ts, 1), F32),
                        pltpu.SemaphoreType.DMA((3, 2))],
        compiler_params=_params(("arbitrary",), vmem),
        name="attn_sample",
    )(qkv, qkv, qkv, gate, cache_k, cache_v, _suffix_matrix(LANES), _suffix_matrix(pc))


@jax.jit
def _step(x_prompt, x_sample, cache_k, cache_v, state_h, state_conv, state_pool,
          norm_rec, w_in_rec, conv_w, conv_b, gate_r_w, gate_r_b, gate_i_w, gate_i_b, rg_lambda,
          pool_w, pool_scale, w_out_rec, norm_att, w_in_att, w_out_att, norm_final):
    bp, t, d = x_prompt.shape
    bs, ts, _ = x_sample.shape
    n_rec, n_att = norm_rec.shape[0], norm_att.shape[0]
    d_rnn = state_h.shape[-1]
    d_pool = state_pool.shape[-1]
    n_heads = cache_k.shape[3]
    mp = bp * t
    assert ts >= POOL_BUF and ts % SUBLANES == 0 and t % ts == 0 and cache_k.shape[4] == HEAD_DIM
    assert gate_r_w.shape[2] == LANES and d_pool // len(POOL_WINDOWS) == 2 * LANES

    x = (x_prompt.reshape(mp, d), x_sample.reshape(bs * ts, d))
    sc_pad = jnp.pad(state_conv, ((0, 0), (0, 0), (SUBLANES - (CONV_W - 1), 0), (0, 0)))
    sp_pad = jnp.pad(state_pool, ((0, 0), (0, 0), (POOL_TAIL - POOL_BUF, 0), (0, 0)))
    w_in_rec, w_out_rec, w_in_att, w_out_att = (w.astype(BF16) for w in (w_in_rec, w_out_rec, w_in_att, w_out_att))

    outs = {k: [] for k in ("hp", "cp", "pp", "hs", "cs", "ps")}
    new_kv = None
    for layer in range(n_rec + n_att):
        j = layer // 2
        if layer % 2 == 0:
            proj = (_norm_matmul_rows(*x, norm_rec[j], w_in_rec, j) if isinstance(x, tuple)
                    else _norm_matmul(x, norm_rec[j], w_in_rec, j))
            ya, hp, hs = _rglru(proj, state_h[j], sc_pad[j], conv_w[j], conv_b[j], gate_r_w[j], gate_r_b[j],
                                gate_i_w[j], gate_i_b[j], rg_lambda[j], bp=bp, t=t, bs=bs, ts=ts, d_rnn=d_rnn)
            yb = _pool(proj, sp_pad[j], pool_w[j], pool_scale[j], bp=bp, t=t, bs=bs, ts=ts,
                       d_rnn=d_rnn, d_pool=d_pool)
            x = _matmul_residual([ya, yb], w_out_rec, j, x)
            frames = proj.reshape(-1, ts, proj.shape[1])

            def last_rows(n, c0, c1):
                prompt = lax.slice(frames, (t // ts - 1, ts - n, c0), (mp // ts, ts, c1), (t // ts, 1, 1))
                sample = lax.slice(frames, (mp // ts, ts - n, c0), (frames.shape[0], ts, c1))
                return prompt, sample

            cp, cs = last_rows(CONV_W - 1, 0, d_rnn)
            pp, ps = last_rows(POOL_BUF, 2 * d_rnn, 2 * d_rnn + d_pool)
            for key, val in (("hp", hp), ("hs", hs), ("cp", cp), ("cs", cs), ("pp", pp), ("ps", ps)):
                outs[key].append(val)
        else:
            qkv, gate, new_kv = _norm_matmul_att(x, norm_att[j], w_in_att, new_kv, layer=j, mp=mp, n_heads=n_heads)
            op = _attn_prompt(qkv, gate, bp=bp, t=t, n_heads=n_heads)
            os_ = _attn_sample(qkv, gate, cache_k, cache_v, layer=j, mp=mp, bs=bs, ts=ts, n_heads=n_heads)
            if layer + 1 < n_rec + n_att:
                x = _matmul_residual_rows(op, os_, w_out_att, j, x)
            else:
                y_prompt, y_sample = _matmul_residual_rows_norm(op, os_, w_out_att, j, x, norm_final)

    if (n_rec + n_att) % 2:
        y_prompt, y_sample = _final_norm(x, norm_final, mp)
    st = {k: jnp.stack(v) for k, v in outs.items()}
    kp, vp, ks, vs = new_kv
    prompt_shape = (n_att, bp, t, n_heads, HEAD_DIM)
    sample_shape = (n_att, bs, ts, n_heads, HEAD_DIM)
    return (y_prompt.reshape(bp, t, d), y_sample.reshape(bs, ts, d),
            kp.reshape(prompt_shape), vp.reshape(prompt_shape), st["hp"], st["cp"], st["pp"],
            ks.reshape(sample_shape), vs.reshape(sample_shape), st["hs"], st["cs"], st["ps"])


def kernel(x_prompt, x_sample, cache_k, cache_v, state_h, state_conv, state_pool, norm_rec, w_in_rec, conv_w, conv_b, gate_r_w, gate_r_b, gate_i_w, gate_i_b, rg_lambda, pool_w, pool_scale, w_out_rec, norm_att, w_in_att, w_out_att, norm_final):
    return _step(x_prompt, x_sample, cache_k, cache_v, state_h, state_conv, state_pool, norm_rec, w_in_rec,
                 conv_w, conv_b, gate_r_w, gate_r_b, gate_i_w, gate_i_b, rg_lambda, pool_w, pool_scale,
                 w_out_rec, norm_att, w_in_att, w_out_att, norm_final)
```
